```python
import jax, jax.numpy as jnp
from jax import lax
import numpy as np

D_MODEL = 1024
BATCH = 16
SEQ = 2048
DEPTH = 2
DEC_BATCH = 16
DEC_SEQ = 16
PAST_LEN = 1024

CHUNK = 64
H_A = 4
DH_A = 64
H_B = 4
DK_B = 64
DV_B = 128
GATE_RANK = 16
GATE_TAU = 16.0
GLA_BLOCK = 64
H_C = 4
DH_C = 64
H_IDX = 4
D_IDX = 64
TOPK_MAX = 256
D_FF = 2816
SB_BLOCK = 128
DSA_BLOCK = 64
MACARON_W = 0.5
EPS = 1e-6
MIX_WIDTH = H_A * DH_A + H_B * DV_B + H_C * DH_C
IN_COLS = (H_A * DH_A, H_A * DH_A, H_A * DH_A,
           H_B * DK_B, H_B * DK_B, H_B * DV_B, H_B * DV_B, GATE_RANK,
           H_C * DH_C, DH_C, DH_C, H_IDX * D_IDX, D_IDX, H_IDX)
D_IN = sum(IN_COLS)
SPLIT_POINTS = tuple(int(v) for v in np.cumsum(IN_COLS)[:-1])

kernel_name = 'hybrid_stream_encoder_step'


def rms_norm(x, g):
    xf = x.astype(jnp.float32)
    y = xf * lax.rsqrt(jnp.mean(xf * xf, axis=-1, keepdims=True) + EPS)
    return (y * g.astype(jnp.float32)).astype(x.dtype)


def swiglu(h, w_up, w_down):
    gate, up = jnp.split(h @ w_up, 2, axis=-1)
    return (jax.nn.silu(gate) * up) @ w_down


def sweep_query_blocks(fn, qs, qpos, blk):
    B, T = qs[0].shape[:2]
    if T <= blk or T % blk:
        return fn(qs, qpos)
    nb = T // blk
    split = lambda a: jnp.moveaxis(a.reshape(B, nb, blk, *a.shape[2:]), 1, 0)
    out = lax.map(lambda args: fn(args[0], args[1]), (tuple(split(a) for a in qs), qpos.reshape(nb, blk)))
    return jnp.moveaxis(out, 0, 1).reshape(B, T, *out.shape[3:])


def stick_breaking_block(q, qpos, k, v, kpos):
    z = jnp.einsum('bqhd,blhd->bhql', q, k).astype(jnp.float32) * (DH_A ** -0.5)
    mask = kpos[None, :] < qpos[:, None]
    log_beta = jax.nn.log_sigmoid(z)
    log_keep = jnp.where(mask, jax.nn.log_sigmoid(-z), 0.0)
    cum = jnp.cumsum(log_keep, axis=-1)
    log_w = log_beta + cum[..., -1:] - cum
    w = jnp.where(mask, jnp.exp(log_w), 0.0)
    return jnp.einsum('bhql,blhd->bqhd', w.astype(v.dtype), v)


def dsa_block(qs, qpos, k, v, k_idx, kpos, n_sel):
    q, q_idx, w_idx = qs
    admissible = (kpos[None, :] // CHUNK) <= (qpos[:, None] // CHUNK)
    s = jnp.einsum('bqhd,bld->bqhl', q_idx, k_idx).astype(jnp.float32) * (D_IDX ** -0.5)
    score = jnp.einsum('bqh,bqhl->bql', w_idx.astype(jnp.float32), jax.nn.relu(s))
    score = jnp.where(admissible[None], score, -jnp.inf)
    _, idx = lax.top_k(score, n_sel)
    gather = jax.vmap(lambda rows, ids: rows[ids])
    k_sel = gather(k, idx)
    v_sel = gather(v, idx)
    valid = (kpos[idx] // CHUNK) <= (qpos[None, :, None] // CHUNK)
    logits = jnp.einsum('bqhd,bqnd->bqhn', q, k_sel).astype(jnp.float32) * (DH_C ** -0.5)
    logits = jnp.where(valid[:, :, None, :], logits, -jnp.inf)
    p = jax.nn.softmax(logits, axis=-1)
    return jnp.einsum('bqhn,bqnd->bqhd', p.astype(v.dtype), v_sel)


def gla_recurrence(q, k, v, log_a, s0, blk):
    B, T, H, _ = q.shape
    DV = v.shape[-1]
    if T % blk:
        blk = T
    nb = T // blk
    f32 = jnp.float32
    rs = lambda a: jnp.moveaxis(a.astype(f32).reshape(B, nb, blk, *a.shape[2:]), 1, 0)
    tri = jnp.tril(jnp.ones((blk, blk), bool))[None, :, :, None, None]

    def step(S, xs):
        qb, kb, vb, ab = xs
        b = jnp.cumsum(ab, axis=1)
        decay = jnp.exp(jnp.where(tri, b[:, :, None] - b[:, None, :], -jnp.inf))
        att = jnp.einsum('bthd,bshd,btshd->bhts', qb, kb, decay)
        o = jnp.einsum('bhts,bshv->bthv', att, vb) + jnp.einsum('bthd,bhdv->bthv', qb * jnp.exp(b), S)
        b_last = b[:, -1]
        S_new = jnp.exp(b_last)[..., None] * S + jnp.einsum('bshd,bshv->bhdv', kb * jnp.exp(b_last[:, None] - b), vb)
        return S_new, o

    s_fin, o = lax.scan(step, s0.astype(f32), (rs(q), rs(k), rs(v), rs(log_a)))
    return jnp.moveaxis(o, 0, 1).reshape(B, T, H, DV), s_fin.astype(s0.dtype)


def token_mixing(h, past_k_sb, past_v_sb, past_k_dsa, past_v_dsa, past_k_idx, gla_state,
                 w_in, w_gate, b_gate, g_gla, w_out):
    B, T, _ = h.shape
    P = past_k_sb.shape[1]
    (qa, ka, va, qb, kb, vb, rb, gb, qc, kc, vc, qi, ki, wi) = jnp.split(h @ w_in, SPLIT_POINTS, axis=-1)
    heads = lambda a, n: a.reshape(B, T, n, -1)
    kpos = jnp.arange(P + T, dtype=jnp.int32)
    qpos = P + jnp.arange(T, dtype=jnp.int32)
    ka, va = heads(ka, H_A), heads(va, H_A)
    k_all = jnp.concatenate([past_k_sb, ka], axis=1)
    v_all = jnp.concatenate([past_v_sb, va], axis=1)
    o_a = sweep_query_blocks(lambda qs, pos: stick_breaking_block(qs[0], pos, k_all, v_all, kpos),
                             (heads(qa, H_A),), qpos, SB_BLOCK)
    log_a = jax.nn.log_sigmoid((gb @ w_gate + b_gate).astype(jnp.float32)) / GATE_TAU
    o_b, s_new = gla_recurrence(heads(qb, H_B) * (DK_B ** -0.5), heads(kb, H_B), heads(vb, H_B),
                                heads(log_a, H_B), gla_state, GLA_BLOCK)
    o_b = rms_norm(o_b, g_gla.reshape(H_B, DV_B)).reshape(B, T, -1).astype(h.dtype) * jax.nn.silu(rb)
    kc_all = jnp.concatenate([past_k_dsa, kc], axis=1)
    vc_all = jnp.concatenate([past_v_dsa, vc], axis=1)
    ki_all = jnp.concatenate([past_k_idx, ki], axis=1)
    n_sel = min(TOPK_MAX, (P + T) // 4)
    o_c = sweep_query_blocks(lambda qs, pos: dsa_block(qs, pos, kc_all, vc_all, ki_all, kpos, n_sel),
                             (heads(qc, H_C), heads(qi, H_IDX), wi * (H_IDX ** -0.5)), qpos, DSA_BLOCK)
    o = jnp.concatenate([o_a.reshape(B, T, -1), o_b, o_c.reshape(B, T, -1)], axis=-1)
    return o @ w_out, (ka, va, kc, vc, ki, s_new)


def encoder_layer(x, c, past_k_sb, past_v_sb, past_k_dsa, past_v_dsa, past_k_idx, gla_state,
                  w_ada, b_ada, gains, w_f1_in, w_f1_out, w_f2_in, w_f2_out, w_in, w_gate, b_gate, g_gla, w_out):
    mod = jax.nn.silu(c) @ w_ada + b_ada
    sh1, sc1, ga1, sh2, sc2, ga2, sh3, sc3, ga3 = [m[:, None, :] for m in jnp.split(mod, 9, axis=-1)]
    h = rms_norm(x, gains[0]) * (1 + sc1) + sh1
    x = x + MACARON_W * ga1 * rms_norm(swiglu(h, w_f1_in, w_f1_out), gains[1])
    h = rms_norm(x, gains[2]) * (1 + sc2) + sh2
    y, rows = token_mixing(h, past_k_sb, past_v_sb, past_k_dsa, past_v_dsa, past_k_idx, gla_state,
                           w_in, w_gate, b_gate, g_gla, w_out)
    x = x + ga2 * rms_norm(y, gains[3])
    h = rms_norm(x, gains[4]) * (1 + sc3) + sh3
    x = x + MACARON_W * ga3 * rms_norm(swiglu(h, w_f2_in, w_f2_out), gains[5])
    return x, rows


def setup_inputs(seed: int = 0) -> dict:
    key = jax.random.key(seed)
    ks = jax.random.split(key, 24)
    nrm = lambda k, shape, s: jax.random.normal(k, shape, jnp.float32) * s
    return {
        'x_prompt': nrm(ks[0], (BATCH, SEQ, D_MODEL), 1.0),
        'x_sample': nrm(ks[1], (DEC_BATCH, DEC_SEQ, D_MODEL), 1.0),
        'cache_k_sb': nrm(ks[2], (DEPTH, DEC_BATCH, PAST_LEN, H_A, DH_A), 1.0),
        'cache_v_sb': nrm(ks[3], (DEPTH, DEC_BATCH, PAST_LEN, H_A, DH_A), 1.0),
        'cache_k_dsa': nrm(ks[4], (DEPTH, DEC_BATCH, PAST_LEN, DH_C), 1.0),
        'cache_v_dsa': nrm(ks[5], (DEPTH, DEC_BATCH, PAST_LEN, DH_C), 1.0),
        'cache_k_idx': nrm(ks[6], (DEPTH, DEC_BATCH, PAST_LEN, D_IDX), 1.0),
        'state_gla': nrm(ks[7], (DEPTH, DEC_BATCH, H_B, DK_B, DV_B), 0.5),
        'c_prompt': nrm(ks[8], (BATCH, D_MODEL), 1.0),
        'c_sample': nrm(ks[9], (DEC_BATCH, D_MODEL), 1.0),
        'w_ada': nrm(ks[10], (DEPTH, D_MODEL, 9 * D_MODEL), 0.5 * D_MODEL ** -0.5),
        'b_ada': nrm(ks[11], (DEPTH, 9 * D_MODEL), 0.01),
        'norm_gains': 1.0 + nrm(ks[12], (DEPTH, 6, D_MODEL), 0.05),
        'w_ffn1_in': nrm(ks[13], (DEPTH, D_MODEL, 2 * D_FF), D_MODEL ** -0.5),
        'w_ffn1_out': nrm(ks[14], (DEPTH, D_FF, D_MODEL), D_FF ** -0.5),
        'w_ffn2_in': nrm(ks[15], (DEPTH, D_MODEL, 2 * D_FF), D_MODEL ** -0.5),
        'w_ffn2_out': nrm(ks[16], (DEPTH, D_FF, D_MODEL), D_FF ** -0.5),
        'w_in': nrm(ks[17], (DEPTH, D_MODEL, D_IN), D_MODEL ** -0.5),
        'w_gla_gate': nrm(ks[18], (DEPTH, GATE_RANK, H_B * DK_B), GATE_RANK ** -0.5),
        'b_gla_gate': nrm(ks[19], (DEPTH, H_B * DK_B), 0.1),
        'gla_norm': 1.0 + nrm(ks[20], (DEPTH, H_B * DV_B), 0.05),
        'w_out': nrm(ks[21], (DEPTH, MIX_WIDTH, D_MODEL), MIX_WIDTH ** -0.5),
    }


def reference(x_prompt, x_sample, cache_k_sb, cache_v_sb, cache_k_dsa, cache_v_dsa, cache_k_idx, state_gla,
              c_prompt, c_sample, w_ada, b_ada, norm_gains, w_ffn1_in, w_ffn1_out, w_ffn2_in, w_ffn2_out,
              w_in, w_gla_gate, b_gla_gate, gla_norm, w_out):
    bp = x_prompt.shape[0]
    dt = x_prompt.dtype
    empty_past = (jnp.zeros((bp, 0, H_A, DH_A), dt), jnp.zeros((bp, 0, H_A, DH_A), dt),
                  jnp.zeros((bp, 0, DH_C), dt), jnp.zeros((bp, 0, DH_C), dt),
                  jnp.zeros((bp, 0, D_IDX), dt), jnp.zeros((bp, H_B, DK_B, DV_B), state_gla.dtype))
    xp, xs = x_prompt, x_sample
    acc_p, acc_s = [], []
    for l in range(DEPTH):
        w_l = (w_ada[l], b_ada[l], norm_gains[l], w_ffn1_in[l], w_ffn1_out[l], w_ffn2_in[l], w_ffn2_out[l],
               w_in[l], w_gla_gate[l], b_gla_gate[l], gla_norm[l], w_out[l])
        xp, rows_p = encoder_layer(xp, c_prompt, *empty_past, *w_l)
        xs, rows_s = encoder_layer(xs, c_sample, cache_k_sb[l], cache_v_sb[l], cache_k_dsa[l], cache_v_dsa[l],
                                   cache_k_idx[l], state_gla[l], *w_l)
        acc_p.append(rows_p)
        acc_s.append(rows_s)
    field = lambda acc, i: jnp.stack([r[i] for r in acc], axis=0)
    new_k_sb_p, new_v_sb_p = field(acc_p, 0), field(acc_p, 1)
    new_k_dsa_p, new_v_dsa_p, new_k_idx_p = field(acc_p, 2), field(acc_p, 3), field(acc_p, 4)
    new_gla_p = field(acc_p, 5)
    new_k_sb_s, new_v_sb_s = field(acc_s, 0), field(acc_s, 1)
    new_k_dsa_s, new_v_dsa_s, new_k_idx_s = field(acc_s, 2), field(acc_s, 3), field(acc_s, 4)
    new_gla_s = field(acc_s, 5)
    return (xp, xs, new_k_sb_p, new_v_sb_p, new_k_dsa_p, new_v_dsa_p, new_k_idx_p, new_gla_p,
            new_k_sb_s, new_v_sb_s, new_k_dsa_s, new_v_dsa_s, new_k_idx_s, new_gla_s)
```

```python
import functools
import math

import numpy as np
import jax
import jax.numpy as jnp
from jax import lax
from jax.experimental import pallas as pl
from jax.experimental.pallas import tpu as pltpu

F32 = jnp.float32
BF16 = jnp.bfloat16

CHUNK = 64
N_HEADS = 4
HEAD_DIM = 64
HW = N_HEADS * HEAD_DIM
DV_B = 128
GATE_RANK = 16
GATE_TAU = 16.0
TOPK_MAX = 256
EPS = 1e-6
MACARON_W = 0.5
N_MOD = 9

LANE = 128
SUBLANE = 8
VMEM_LIMIT_BYTES = 56 * 1024 * 1024

_NEG_INF = float("-inf")
_KEY_NEG_INF = -2**31 + 0x7FFFFF
_INT_MIN = -2**31


def _pick_tile(n, target, mult):
    if n <= target:
        return n
    t = (target // mult) * mult
    while t >= mult:
        if n % t == 0:
            return t
        t -= mult
    return n


def _cparams(n_axes):
    return pltpu.CompilerParams(dimension_semantics=("arbitrary",) * n_axes,
                                vmem_limit_bytes=VMEM_LIMIT_BYTES)


def _rms_rows(x, g):
    ms = jnp.mean(x * x, axis=-1, keepdims=True)
    return x * lax.rsqrt(ms + EPS) * g


def _silu(x):
    return x * (1.0 / (1.0 + jnp.exp(-x)))


def _log_sigmoid(x):
    return jnp.minimum(x, 0.0) - jnp.log(1.0 + jnp.exp(-jnp.abs(x)))


def _dot(a, b):
    return jnp.dot(a, b, preferred_element_type=F32)


def _dot_nt(a, b):
    return lax.dot_general(a, b, (((1,), (1,)), ((), ())), preferred_element_type=F32)


def _dot_tn(a, b):
    return lax.dot_general(a, b, (((0,), (0,)), ((), ())), preferred_element_type=F32)


def _split_bf16(x):
    hi = x.astype(BF16)
    lo = (x - hi.astype(F32)).astype(BF16)
    return hi, lo


def _head_lane_id(width, per_head):
    return lax.broadcasted_iota(jnp.int32, (1, width), 1) // per_head


def _mod_kernel(c_ref, w_ref, b_ref, o_ref):
    a = _silu(c_ref[...]).astype(BF16)
    o_ref[...] = _dot(a, w_ref[...]) + b_ref[...]


def _mod_call(c, w_ada, b_ada):
    depth, d, nd = w_ada.shape
    n = c.shape[0]
    tn = _pick_tile(nd, 1536, LANE)
    return pl.pallas_call(
        _mod_kernel,
        grid=(depth, nd // tn),
        in_specs=[pl.BlockSpec((n, d), lambda l, j: (0, 0)),
                  pl.BlockSpec((None, d, tn), lambda l, j: (l, 0, j)),
                  pl.BlockSpec((None, 1, tn), lambda l, j: (l, 0, j))],
        out_specs=pl.BlockSpec((None, n, tn), lambda l, j: (l, 0, j)),
        out_shape=jax.ShapeDtypeStruct((depth, n, nd), F32),
        compiler_params=_cparams(2),
        name="adaln_mod",
    )(c, w_ada, b_ada.reshape(depth, 1, nd))


class _Mod:
    def __init__(self, arr, per_token, tiles_per_seq=None):
        self.arr, self.per_token, self.tiles_per_seq = arr, per_token, tiles_per_seq

    def spec(self, k, tm, d):
        if self.per_token:
            return pl.BlockSpec((None, tm, d), lambda i, *_: (k, i, 0))
        tps = self.tiles_per_seq
        return pl.BlockSpec((None, 1, d), lambda i, *_: ((i // tps) * N_MOD + k, 0, 0))


def _gain_spec(k, d):
    return pl.BlockSpec((None, 1, d), lambda i, *_: (k, 0, 0))


def _ffn_kernel(x_ref, sh_ref, sc_ref, ga_ref, gin_ref, gout_ref, wg_ref, wu_ref, wd_ref, o_ref,
                h_scr, acc_scr, *, n_f):
    j = pl.program_id(1)

    @pl.when(j == 0)
    def _():
        h = _rms_rows(x_ref[...], gin_ref[...]) * (1.0 + sc_ref[...]) + sh_ref[...]
        h_scr[...] = h.astype(BF16)
        acc_scr[...] = jnp.zeros_like(acc_scr)

    h = h_scr[...]
    gate = _dot(h, wg_ref[...])
    up = _dot(h, wu_ref[...])
    act = (_silu(gate) * up).astype(BF16)
    acc_scr[...] += _dot(act, wd_ref[...])

    @pl.when(j == n_f - 1)
    def _():
        y = _rms_rows(acc_scr[...], gout_ref[...])
        o_ref[...] = x_ref[...] + MACARON_W * ga_ref[...] * y


def _ffn_call(x, mod, mod_k, gains, gain_k, w_in, w_out, tm):
    rows, d = x.shape
    f = w_out.shape[0]
    tf = _pick_tile(f, 1408, LANE)
    n_f = f // tf
    row_spec = pl.BlockSpec((tm, d), lambda i, j: (i, 0))
    return pl.pallas_call(
        functools.partial(_ffn_kernel, n_f=n_f),
        grid=(rows // tm, n_f),
        in_specs=[row_spec,
                  mod.spec(mod_k, tm, d), mod.spec(mod_k + 1, tm, d), mod.spec(mod_k + 2, tm, d),
                  _gain_spec(gain_k, d), _gain_spec(gain_k + 1, d),
                  pl.BlockSpec((d, tf), lambda i, j: (0, j)),
                  pl.BlockSpec((d, tf), lambda i, j: (0, j + n_f)),
                  pl.BlockSpec((tf, d), lambda i, j: (j, 0))],
        out_specs=row_spec,
        out_shape=jax.ShapeDtypeStruct((rows, d), F32),
        scratch_shapes=[pltpu.VMEM((tm, d), BF16), pltpu.VMEM((tm, d), F32)],
        compiler_params=_cparams(2),
        name="macaron_ffn",
    )(x, mod.arr, mod.arr, mod.arr, gains, gains, w_in, w_in, w_out)


_PROJ_GROUPS = (("sbq", HW), ("sbk", HW), ("sbv", HW),
                ("gq", HW), ("gk", HW), ("gv", N_HEADS * DV_B), ("gr", N_HEADS * DV_B), ("gb", LANE),
                ("dq", HW), ("dqi", HW), ("dkc4", HW), ("dvc4", HW), ("dki4", HW), ("dwi", LANE))
_PROJ_OFFSETS = tuple(int(v) for v in np.cumsum([0] + [w for _, w in _PROJ_GROUPS]))
_PROJ_WIDTH = _PROJ_OFFSETS[-1]


def _relayout_w_in(w_in):
    d = w_in.shape[0]
    widths = (HW, HW, HW, HW, HW, N_HEADS * DV_B, N_HEADS * DV_B, GATE_RANK,
              HW, HEAD_DIM, HEAD_DIM, HW, HEAD_DIM, N_HEADS)
    offs = np.cumsum((0,) + widths)
    col = lambda i: w_in[:, offs[i]:offs[i + 1]]
    pad = lambda a, w: jnp.pad(a, ((0, 0), (0, w - a.shape[1])))
    rep = lambda a: jnp.tile(a, (1, N_HEADS))
    parts = [col(0), col(1), col(2), col(3), col(4), col(5), col(6), pad(col(7), LANE),
             col(8), col(11), rep(col(9)), rep(col(10)), rep(col(12)), pad(col(13), LANE)]
    out = jnp.concatenate(parts, axis=1)
    assert out.shape == (d, _PROJ_WIDTH)
    return out


def _proj_kernel(x_ref, sh_ref, sc_ref, g_ref, w_ref, wgate_ref, bgate_ref, *o_refs):
    h = (_rms_rows(x_ref[...], g_ref[...]) * (1.0 + sc_ref[...]) + sh_ref[...]).astype(BF16)
    k = 0
    for gi, (name, _) in enumerate(_PROJ_GROUPS):
        p = _dot(h, w_ref[:, _PROJ_OFFSETS[gi]:_PROJ_OFFSETS[gi + 1]])
        if name == "gb":
            pre = _dot(p.astype(BF16), wgate_ref[...]) + bgate_ref[...]
            p = _log_sigmoid(pre) * (1.0 / GATE_TAU)
        o_refs[k][...] = p
        k += 1


def _proj_call(x, mod, gains, w_in, w_gate, b_gate, tm):
    rows, d = x.shape
    out_w = [HW if n == "gb" else w for n, w in _PROJ_GROUPS]
    return pl.pallas_call(
        _proj_kernel,
        grid=(rows // tm,),
        in_specs=[pl.BlockSpec((tm, d), lambda i: (i, 0)),
                  mod.spec(3, tm, d), mod.spec(4, tm, d), _gain_spec(2, d),
                  pl.BlockSpec((d, _PROJ_WIDTH), lambda i: (0, 0)),
                  pl.BlockSpec((LANE, HW), lambda i: (0, 0)),
                  pl.BlockSpec((1, HW), lambda i: (0, 0))],
        out_specs=[pl.BlockSpec((tm, w), lambda i: (i, 0)) for w in out_w],
        out_shape=[jax.ShapeDtypeStruct((rows, w), F32) for w in out_w],
        compiler_params=_cparams(1),
        name="mix_in_proj",
    )(x, mod.arr, mod.arr, gains, w_in, w_gate, b_gate)


def _sb_kernel(q_ref, k_ref, v_ref, u_ref, o_ref, acc_scr, c_scr, *, tq, tk, past):
    i = pl.program_id(1)
    head = _head_lane_id(HW, HEAD_DIM)
    q = q_ref[...] * (HEAD_DIM ** -0.5)
    qm = [jnp.where(head == h, q, 0.0).astype(BF16) for h in range(N_HEADS)]
    qpos = past + i * tq + lax.broadcasted_iota(jnp.int32, (tq, 1), 0)
    last_blk = (past + (i + 1) * tq - 2) // tk
    acc_scr[...] = jnp.zeros_like(acc_scr)
    c_scr[...] = jnp.zeros_like(c_scr)
    u = u_ref[...]

    def body(n, carry):
        j = last_blk - n
        k0 = pl.multiple_of(j * tk, tk)
        kb = k_ref[pl.ds(k0, tk), :].astype(BF16)
        vb = v_ref[pl.ds(k0, tk), :].astype(BF16)
        kpos = k0 + lax.broadcasted_iota(jnp.int32, (1, tk), 1)
        valid = kpos < qpos
        for h in range(N_HEADS):
            z = _dot_nt(qm[h], kb)
            log_beta = _log_sigmoid(z)
            log_keep = jnp.where(valid, log_beta - z, 0.0)
            hi, lo = _split_bf16(log_keep)
            newer = _dot(hi, u) + _dot(lo, u)
            c = c_scr[h]
            w = jnp.where(valid, jnp.exp(log_beta + newer + c), 0.0).astype(BF16)
            pv = _dot(w, vb)
            acc_scr[...] += jnp.where(head == h, pv, 0.0)
            c_scr[h] = c + jnp.sum(log_keep, axis=1, keepdims=True)
        return carry

    lax.fori_loop(0, last_blk + 1, body, 0)
    o_ref[...] = acc_scr[...]


def _sb_call(q, k_all, v_all, past, tq, tk):
    b, t, _ = q.shape
    l_pad = k_all.shape[1]
    u = (np.arange(tk)[:, None] > np.arange(tk)[None, :]).astype(np.float32)
    return pl.pallas_call(
        functools.partial(_sb_kernel, tq=tq, tk=tk, past=past),
        grid=(b, t // tq),
        in_specs=[pl.BlockSpec((None, tq, HW), lambda bi, i: (bi, i, 0)),
                  pl.BlockSpec((None, l_pad, HW), lambda bi, i: (bi, 0, 0)),
                  pl.BlockSpec((None, l_pad, HW), lambda bi, i: (bi, 0, 0)),
                  pl.BlockSpec((tk, tk), lambda bi, i: (0, 0))],
        out_specs=pl.BlockSpec((None, tq, HW), lambda bi, i: (bi, i, 0)),
        out_shape=jax.ShapeDtypeStruct((b, t, HW), F32),
        scratch_shapes=[pltpu.VMEM((tq, HW), F32), pltpu.VMEM((N_HEADS, tq, 1), F32)],
        compiler_params=_cparams(2),
        name="stick_breaking_attn",
    )(q, k_all, v_all, jnp.asarray(u, BF16))


def _gla_tables(c):
    n_lev = int(math.log2(c))
    assert 1 << n_lev == c
    t = np.arange(c)
    rows = []
    for lev in range(1, n_lev + 1):
        blk = c >> (lev - 1)
        ref = (t // blk) * blk + blk // 2 - 1
        lo, hi = np.minimum(t, ref), np.maximum(t, ref)
        rows.append(((t[None, :] > lo[:, None]) & (t[None, :] <= hi[:, None])).astype(np.float32))
    rows.append((t[None, :] <= t[:, None]).astype(np.float32))
    rows.append((t[None, :] > t[:, None]).astype(np.float32))
    table = np.concatenate(rows, axis=0)
    lvl = np.full((c, c), n_lev + 1, np.int32)
    for lev in range(1, n_lev + 1):
        blk = c >> (lev - 1)
        half = blk // 2
        same = (t[:, None] // blk) == (t[None, :] // blk)
        pair = same & ((t[:, None] % blk) >= half) & ((t[None, :] % blk) < half)
        lvl[pair] = lev
    lvl[t, t] = 0
    return table, np.tile(lvl, (N_HEADS, 1)), n_lev


def _gla_kernel(q_ref, k_ref, v_ref, r_ref, la_ref, s0_ref, tab_ref, lvl_ref, g_ref,
                o_ref, s_ref, s_scr, *, c, n_lev, n_chunks, n_steps):
    j = pl.program_id(1)

    @pl.when(j == 0)
    def _():
        s_scr[...] = s0_ref[...]

    head = _head_lane_id(HW, HEAD_DIM)
    tab = tab_ref[...]
    lvl = lvl_ref[...]
    ones = jnp.ones((c, LANE), BF16)

    def chunk(ci, carry):
        r0 = pl.multiple_of(ci * c, c)
        rows = pl.ds(r0, c)
        q = q_ref[rows, :] * (HEAD_DIM ** -0.5)
        k = k_ref[rows, :]
        la_hi, la_lo = _split_bf16(la_ref[rows, :])
        decays = jnp.exp(_dot(tab, la_hi) + _dot(tab, la_lo))
        att = jnp.zeros((N_HEADS * c, c), F32)
        for lev in range(n_lev + 1):
            if lev == 0:
                ql, kl = q, k
            else:
                e = decays[(lev - 1) * c:lev * c]
                ql, kl = q * e, k * e
            lhs = jnp.concatenate([jnp.where(head == h, ql, 0.0) for h in range(N_HEADS)], axis=0)
            a = _dot_nt(lhs.astype(BF16), kl.astype(BF16))
            att = jnp.where(lvl == lev, a, att)
        q_in = q * decays[n_lev * c:(n_lev + 1) * c]
        k_out = (k * decays[(n_lev + 1) * c:(n_lev + 2) * c]).astype(BF16)
        s = s_scr[...]
        s_b = s.astype(BF16)
        ds_parts = []
        for h in range(N_HEADS):
            vh = v_ref[rows, h * DV_B:(h + 1) * DV_B].astype(BF16)
            o = (_dot(att[h * c:(h + 1) * c].astype(BF16), vh)
                 + _dot(jnp.where(head == h, q_in, 0.0).astype(BF16), s_b))
            y = _rms_rows(o, g_ref[:, h * DV_B:(h + 1) * DV_B])
            o_ref[rows, h * DV_B:(h + 1) * DV_B] = y * _silu(r_ref[rows, h * DV_B:(h + 1) * DV_B])
            ds_parts.append(_dot_tn(k_out, vh)[h * HEAD_DIM:(h + 1) * HEAD_DIM])
        total = _dot_tn(la_hi, ones) + _dot_tn(la_lo, ones)
        s_scr[...] = jnp.exp(total) * s + jnp.concatenate(ds_parts, axis=0)
        return carry

    lax.fori_loop(0, n_chunks, chunk, 0)

    @pl.when(j == n_steps - 1)
    def _():
        s_ref[...] = s_scr[...]


def _gla_call(q, k, v, r, la, s0, g_gla, c):
    b, t, _ = q.shape
    vw = N_HEADS * DV_B
    tc = _pick_tile(t, 512, c)
    n_steps = t // tc
    table, lvl, n_lev = _gla_tables(c)
    qk_spec = pl.BlockSpec((None, tc, HW), lambda bi, j: (bi, j, 0))
    vr_spec = pl.BlockSpec((None, tc, vw), lambda bi, j: (bi, j, 0))
    st_spec = pl.BlockSpec((None, HW, DV_B), lambda bi, j: (bi, 0, 0))
    const = lambda shape: pl.BlockSpec(shape, lambda bi, j: (0, 0))
    return pl.pallas_call(
        functools.partial(_gla_kernel, c=c, n_lev=n_lev, n_chunks=tc // c, n_steps=n_steps),
        grid=(b, n_steps),
        in_specs=[qk_spec, qk_spec, vr_spec, vr_spec, qk_spec, st_spec,
                  const(table.shape), const(lvl.shape), const((1, vw))],
        out_specs=[vr_spec, st_spec],
        out_shape=[jax.ShapeDtypeStruct((b, t, vw), F32), jax.ShapeDtypeStruct((b, HW, DV_B), F32)],
        scratch_shapes=[pltpu.VMEM((HW, DV_B), F32)],
        compiler_params=_cparams(2),
        name="gated_linear_attn",
    )(q, k, v, r, la, s0, jnp.asarray(table, BF16), jnp.asarray(lvl), g_gla)


def _key_to_f32(key):
    key = jnp.maximum(key, _KEY_NEG_INF)
    bits = jnp.where(key < 0, key ^ 0x7FFFFFFF, key)
    return lax.bitcast_convert_type(bits, F32)


def _lane_fold(m, tk):
    acc = m[:, 0:LANE]
    for s in range(1, tk // LANE):
        acc = acc + m[:, s * LANE:(s + 1) * LANE]
    return acc


def _dsa_kernel(q_ref, qi_ref, wi_ref, kc_ref, vc_ref, ki_ref, o_ref, sc_scr, bias_scr, acc_scr,
                *, tq, tk, past, l_real, n_sel, n_idx_bits):
    i = pl.program_id(1)
    head = _head_lane_id(HW, HEAD_DIM)
    scale = HEAD_DIM ** -0.5
    qi = qi_ref[...] * scale
    qim = [jnp.where(head == h, qi, 0.0).astype(BF16) for h in range(N_HEADS)]
    wi = wi_ref[...] * (N_HEADS ** -0.5)
    wih = [wi[:, h:h + 1] for h in range(N_HEADS)]
    qpos = past + i * tq + lax.broadcasted_iota(jnp.int32, (tq, 1), 0)
    qchunk = lax.shift_right_logical(qpos, int(math.log2(CHUNK)))
    k_end = ((past + (i + 1) * tq - 1) // CHUNK + 1) * CHUNK
    n_blk = jnp.minimum((k_end + tk - 1) // tk, sc_scr.shape[1] // tk)

    def key_pos(j):
        return j * tk + lax.broadcasted_iota(jnp.int32, (1, tk), 1)

    def admissible(kpos):
        return (lax.shift_right_logical(kpos, int(math.log2(CHUNK))) <= qchunk) & (kpos < l_real)

    def score_blk(j, carry):
        k0 = pl.multiple_of(j * tk, tk)
        kib = ki_ref[pl.ds(k0, tk), :].astype(BF16)
        s = jnp.zeros((tq, tk), F32)
        for h in range(N_HEADS):
            s = s + wih[h] * jnp.maximum(_dot_nt(qim[h], kib), 0.0)
        sc_scr[:, pl.ds(k0, tk)] = jnp.where(admissible(key_pos(j)), s, _NEG_INF)
        return carry

    lax.fori_loop(0, n_blk, score_blk, 0)

    def count(pred):
        def blk(j, acc):
            k0 = pl.multiple_of(j * tk, tk)
            m = jnp.where(pred(sc_scr[:, pl.ds(k0, tk)], key_pos(j)), 1.0, 0.0)
            return acc + _lane_fold(m, tk)
        acc = lax.fori_loop(0, n_blk, blk, jnp.zeros((tq, LANE), F32))
        return jnp.sum(acc, axis=1, keepdims=True)

    def value_bit(it, tau):
        cand = tau + lax.shift_left(jnp.int32(1), 31 - it)
        thr = _key_to_f32(cand)
        cnt = count(lambda s, kp: s >= thr)
        return jnp.where(cnt >= n_sel, cand, tau)

    tau = lax.fori_loop(0, 32, value_bit, jnp.full((tq, 1), _INT_MIN, jnp.int32))
    thr = _key_to_f32(tau)

    need = n_sel - count(lambda s, kp: s > thr)

    def index_bit(it, lo):
        cand = lo + lax.shift_left(jnp.int32(1), n_idx_bits - 1 - it)
        cnt = count(lambda s, kp: (s == thr) & (kp < cand))
        return jnp.where(cnt < need, cand, lo)

    last_tie = lax.fori_loop(0, n_idx_bits, index_bit, jnp.zeros((tq, 1), jnp.int32))

    def bias_blk(j, carry):
        k0 = pl.multiple_of(j * tk, tk)
        s = sc_scr[:, pl.ds(k0, tk)]
        kp = key_pos(j)
        sel = admissible(kp) & ((s > thr) | ((s == thr) & (kp <= last_tie)))
        bias_scr[:, pl.ds(k0, tk)] = jnp.where(sel, 0.0, _NEG_INF)
        return carry

    lax.fori_loop(0, n_blk, bias_blk, 0)

    q = q_ref[...] * scale
    out = jnp.zeros((tq, HW), F32)
    for h in range(N_HEADS):
        qh = jnp.where(head == h, q, 0.0).astype(BF16)

        def logit_blk(j, mx):
            k0 = pl.multiple_of(j * tk, tk)
            lg = _dot_nt(qh, kc_ref[pl.ds(k0, tk), :].astype(BF16)) + bias_scr[:, pl.ds(k0, tk)]
            sc_scr[:, pl.ds(k0, tk)] = lg
            return jnp.maximum(mx, jnp.max(lg, axis=1, keepdims=True))

        mx = lax.fori_loop(0, n_blk, logit_blk, jnp.full((tq, 1), _NEG_INF, F32))
        acc_scr[...] = jnp.zeros_like(acc_scr)

        def pv_blk(j, den):
            k0 = pl.multiple_of(j * tk, tk)
            p = jnp.exp(sc_scr[:, pl.ds(k0, tk)] - mx)
            acc_scr[...] += _dot(p.astype(BF16), vc_ref[pl.ds(k0, tk), :].astype(BF16))
            return den + jnp.sum(p, axis=1, keepdims=True)

        den = lax.fori_loop(0, n_blk, pv_blk, jnp.zeros((tq, 1), F32))
        out = jnp.where(head == h, acc_scr[...] / den, out)
    o_ref[...] = out


def _dsa_call(q, qi, wi, kc4, vc4, ki4, past, l_real, tq, tk):
    b, t, _ = q.shape
    l_pad = kc4.shape[1]
    n_sel = min(TOPK_MAX, l_real // 4)
    assert tk >= n_sel and l_pad % tk == 0
    row = lambda w: pl.BlockSpec((None, tq, w), lambda bi, i: (bi, i, 0))
    full = pl.BlockSpec((None, l_pad, HW), lambda bi, i: (bi, 0, 0))
    return pl.pallas_call(
        functools.partial(_dsa_kernel, tq=tq, tk=tk, past=past, l_real=l_real, n_sel=n_sel,
                          n_idx_bits=max(1, int(math.ceil(math.log2(l_pad))))),
        grid=(b, t // tq),
        in_specs=[row(HW), row(HW), row(LANE), full, full, full],
        out_specs=row(HW),
        out_shape=jax.ShapeDtypeStruct((b, t, HW), F32),
        scratch_shapes=[pltpu.VMEM((tq, l_pad), F32), pltpu.VMEM((tq, l_pad), F32),
                        pltpu.VMEM((tq, HW), F32)],
        compiler_params=_cparams(2),
        name="indexer_sparse_attn",
    )(q, qi, wi, kc4, vc4, ki4)


def _mix_out_kernel(x_ref, oa_ref, ob_ref, oc_ref, ga_ref, g_ref, w_ref, o_ref):
    wa = HW
    wb = wa + N_HEADS * DV_B
    y = (_dot(oa_ref[...].astype(BF16), w_ref[0:wa, :])
         + _dot(ob_ref[...].astype(BF16), w_ref[wa:wb, :])
         + _dot(oc_ref[...].astype(BF16), w_ref[wb:wb + HW, :]))
    o_ref[...] = x_ref[...] + ga_ref[...] * _rms_rows(y, g_ref[...])


def _mix_out_call(x, oa, ob, oc, mod, gains, w_out, tm):
    rows, d = x.shape
    mixw = w_out.shape[0]
    row = lambda w: pl.BlockSpec((tm, w), lambda i: (i, 0))
    return pl.pallas_call(
        _mix_out_kernel,
        grid=(rows // tm,),
        in_specs=[row(d), row(HW), row(N_HEADS * DV_B), row(HW),
                  mod.spec(5, tm, d), _gain_spec(3, d),
                  pl.BlockSpec((mixw, d), lambda i: (0, 0))],
        out_specs=row(d),
        out_shape=jax.ShapeDtypeStruct((rows, d), F32),
        compiler_params=_cparams(1),
        name="mix_out_proj",
    )(x, oa, ob, oc, mod.arr, gains, w_out)


def _pad_rows(a, l_pad):
    return jnp.pad(a, ((0, 0), (0, l_pad - a.shape[1]), (0, 0)))


def _encoder_layer(x, n_seq, t, mod, past, w):
    rows, d = x.shape
    tm = _pick_tile(t, 512, SUBLANE) if not mod.per_token else rows
    x = _ffn_call(x, mod, 0, w["gains"], 0, w["w_f1_in"], w["w_f1_out"], tm)
    (sbq, sbk, sbv, gq, gk, gv, gr, la, dq, dqi, dkc4, dvc4, dki4, dwi) = _proj_call(
        x, mod, w["gains"], w["w_in"], w["w_gate"], w["b_gate"], tm)
    seq = lambda a: a.reshape(n_seq, t, a.shape[-1])
    p_len = 0 if past is None else past["k_sb"].shape[1]
    l_real = p_len + t

    tq = _pick_tile(t, 256, SUBLANE)
    tk = 256 if l_real >= 256 else LANE
    l_pad = -(-l_real // tk) * tk
    cat = (lambda new, old: _pad_rows(seq(new) if old is None else jnp.concatenate([old, seq(new)], axis=1),
                                      l_pad))
    k_all = cat(sbk, None if past is None else past["k_sb"])
    v_all = cat(sbv, None if past is None else past["v_sb"])
    o_a = _sb_call(seq(sbq), k_all, v_all, p_len, tq, tk)

    c = CHUNK if t % CHUNK == 0 else t
    s0 = jnp.zeros((n_seq, HW, DV_B), F32) if past is None else past["gla"]
    o_b, s_new = _gla_call(seq(gq), seq(gk), seq(gv), seq(gr), seq(la), s0, w["g_gla"], c)

    rep = lambda a: jnp.tile(a, (1, 1, N_HEADS))
    kc_all = cat(dkc4, None if past is None else rep(past["k_dsa"]))
    vc_all = cat(dvc4, None if past is None else rep(past["v_dsa"]))
    ki_all = cat(dki4, None if past is None else rep(past["k_idx"]))
    o_c = _dsa_call(seq(dq), seq(dqi), seq(dwi), kc_all, vc_all, ki_all, p_len, l_real, tq, tk)

    flat = lambda a: a.reshape(rows, a.shape[-1])
    x = _mix_out_call(x, flat(o_a), flat(o_b), flat(o_c), mod, w["gains"], w["w_out"], tm)
    x = _ffn_call(x, mod, 6, w["gains"], 4, w["w_f2_in"], w["w_f2_out"], tm)
    new_rows = (seq(sbk).reshape(n_seq, t, N_HEADS, HEAD_DIM), seq(sbv).reshape(n_seq, t, N_HEADS, HEAD_DIM),
                seq(dkc4)[..., :HEAD_DIM], seq(dvc4)[..., :HEAD_DIM], seq(dki4)[..., :HEAD_DIM],
                s_new.reshape(n_seq, N_HEADS, HEAD_DIM, DV_B))
    return x, new_rows


def kernel(x_prompt, x_sample, cache_k_sb, cache_v_sb, cache_k_dsa, cache_v_dsa, cache_k_idx, state_gla,
           c_prompt, c_sample, w_ada, b_ada, norm_gains, w_ffn1_in, w_ffn1_out, w_ffn2_in, w_ffn2_out,
           w_in, w_gla_gate, b_gla_gate, gla_norm, w_out):
    bp, tp, d = x_prompt.shape
    bs, ts, _ = x_sample.shape
    depth = w_ada.shape[0]
    p_len = cache_k_sb.shape[2]

    mods = _mod_call(jnp.concatenate([c_prompt, c_sample], axis=0), w_ada.astype(BF16), b_ada)
    xp = x_prompt.reshape(bp * tp, d)
    xs = x_sample.reshape(bs * ts, d)
    tm_p = _pick_tile(tp, 512, SUBLANE)
    acc_p, acc_s = [], []
    for l in range(depth):
        w_gate = jnp.pad(w_gla_gate[l], ((0, LANE - GATE_RANK), (0, 0))).astype(BF16)
        w = dict(gains=norm_gains[l].reshape(-1, 1, d),
                 w_f1_in=w_ffn1_in[l].astype(BF16), w_f1_out=w_ffn1_out[l].astype(BF16),
                 w_f2_in=w_ffn2_in[l].astype(BF16), w_f2_out=w_ffn2_out[l].astype(BF16),
                 w_in=_relayout_w_in(w_in[l]).astype(BF16), w_gate=w_gate,
                 b_gate=b_gla_gate[l].reshape(1, HW), g_gla=gla_norm[l].reshape(1, N_HEADS * DV_B),
                 w_out=w_out[l].astype(BF16))
        mod_p = _Mod(mods[l, :bp].reshape(bp * N_MOD, 1, d), False, tp // tm_p)
        mod_s_rows = jnp.repeat(mods[l, bp:].reshape(bs, N_MOD, d), ts, axis=0)
        mod_s = _Mod(jnp.transpose(mod_s_rows, (1, 0, 2)), True)
        past = dict(k_sb=cache_k_sb[l].reshape(bs, p_len, HW), v_sb=cache_v_sb[l].reshape(bs, p_len, HW),
                    k_dsa=cache_k_dsa[l], v_dsa=cache_v_dsa[l], k_idx=cache_k_idx[l],
                    gla=state_gla[l].reshape(bs, HW, DV_B))
        xp, rows_p = _encoder_layer(xp, bp, tp, mod_p, None, w)
        xs, rows_s = _encoder_layer(xs, bs, ts, mod_s, past, w)
        acc_p.append(rows_p)
        acc_s.append(rows_s)
    field = lambda acc, i: jnp.stack([r[i] for r in acc], axis=0)
    return (xp.reshape(bp, tp, d), xs.reshape(bs, ts, d),
            *(field(acc_p, i) for i in range(6)), *(field(acc_s, i) for i in range(6)))
```

```python
import functools
import math

import numpy as np
import jax
import jax.numpy as jnp
from jax import lax
from jax.experimental import pallas as pl
from jax.experimental.pallas import tpu as pltpu

F32 = jnp.float32
BF16 = jnp.bfloat16

CHUNK = 64
N_HEADS = 4
HEAD_DIM = 64
HW = N_HEADS * HEAD_DIM
DV_B = 128
GATE_RANK = 16
GATE_TAU = 16.0
TOPK_MAX = 256
EPS = 1e-6
MACARON_W = 0.5
N_MOD = 9

LANE = 128
SUBLANE = 8
VMEM_LIMIT_BYTES = 56 * 1024 * 1024

_NEG_INF = float("-inf")
_KEY_NEG_INF = -2**31 + 0x7FFFFF
_INT_MIN = -2**31


def _pick_tile(n, target, mult):
    if n <= target:
        return n
    t = (target // mult) * mult
    while t >= mult:
        if n % t == 0:
            return t
        t -= mult
    return n


def _cparams(n_axes):
    return pltpu.CompilerParams(dimension_semantics=("arbitrary",) * n_axes,
                                vmem_limit_bytes=VMEM_LIMIT_BYTES)


def _rms_rows(x, g):
    ms = jnp.mean(x * x, axis=-1, keepdims=True)
    return x * lax.rsqrt(ms + EPS) * g


def _silu(x):
    return x * (1.0 / (1.0 + jnp.exp(-x)))


def _log_sigmoid(x):
    return jnp.minimum(x, 0.0) - jnp.log(1.0 + jnp.exp(-jnp.abs(x)))


def _dot(a, b):
    return jnp.dot(a, b, preferred_element_type=F32)


def _dot_nt(a, b):
    return lax.dot_general(a, b, (((1,), (1,)), ((), ())), preferred_element_type=F32)


def _dot_tn(a, b):
    return lax.dot_general(a, b, (((0,), (0,)), ((), ())), preferred_element_type=F32)


def _split_bf16(x):
    hi = x.astype(BF16)
    lo = (x - hi.astype(F32)).astype(BF16)
    return hi, lo


def _head_lane_id(width, per_head):
    return lax.broadcasted_iota(jnp.int32, (1, width), 1) // per_head


def _mod_kernel(c_ref, w_ref, b_ref, o_ref):
    a = _silu(c_ref[...]).astype(BF16)
    o_ref[...] = _dot(a, w_ref[...]) + b_ref[...]


def _mod_call(c, w_ada, b_ada):
    depth, d, nd = w_ada.shape
    n = c.shape[0]
    tn = _pick_tile(nd, 1536, LANE)
    return pl.pallas_call(
        _mod_kernel,
        grid=(depth, nd // tn),
        in_specs=[pl.BlockSpec((n, d), lambda l, j: (0, 0)),
                  pl.BlockSpec((None, d, tn), lambda l, j: (l, 0, j)),
                  pl.BlockSpec((None, 1, tn), lambda l, j: (l, 0, j))],
        out_specs=pl.BlockSpec((None, n, tn), lambda l, j: (l, 0, j)),
        out_shape=jax.ShapeDtypeStruct((depth, n, nd), F32),
        compiler_params=_cparams(2),
        name="adaln_mod",
    )(c, w_ada, b_ada.reshape(depth, 1, nd))


class _Mod:
    def __init__(self, arr, per_token, tiles_per_seq=None):
        self.arr, self.per_token, self.tiles_per_seq = arr, per_token, tiles_per_seq

    def spec(self, k, tm, d):
        if self.per_token:
            return pl.BlockSpec((None, tm, d), lambda i, *_: (k, i, 0))
        tps = self.tiles_per_seq
        return pl.BlockSpec((None, 1, d), lambda i, *_: ((i // tps) * N_MOD + k, 0, 0))


def _gain_spec(k, d):
    return pl.BlockSpec((None, 1, d), lambda i, *_: (k, 0, 0))


def _ffn_kernel(x_ref, sh_ref, sc_ref, ga_ref, gin_ref, gout_ref, wg_ref, wu_ref, wd_ref, o_ref,
                h_scr, acc_scr, *, n_f):
    j = pl.program_id(1)

    @pl.when(j == 0)
    def _():
        h = _rms_rows(x_ref[...], gin_ref[...]) * (1.0 + sc_ref[...]) + sh_ref[...]
        h_scr[...] = h.astype(BF16)
        acc_scr[...] = jnp.zeros_like(acc_scr)

    h = h_scr[...]
    gate = _dot(h, wg_ref[...])
    up = _dot(h, wu_ref[...])
    act = (_silu(gate) * up).astype(BF16)
    acc_scr[...] += _dot(act, wd_ref[...])

    @pl.when(j == n_f - 1)
    def _():
        y = _rms_rows(acc_scr[...], gout_ref[...])
        o_ref[...] = x_ref[...] + MACARON_W * ga_ref[...] * y


def _ffn_call(x, mod, mod_k, gains, gain_k, w_in, w_out, tm):
    rows, d = x.shape
    f = w_out.shape[0]
    tf = _pick_tile(f, 1408, LANE)
    n_f = f // tf
    row_spec = pl.BlockSpec((tm, d), lambda i, j: (i, 0))
    return pl.pallas_call(
        functools.partial(_ffn_kernel, n_f=n_f),
        grid=(rows // tm, n_f),
        in_specs=[row_spec,
                  mod.spec(mod_k, tm, d), mod.spec(mod_k + 1, tm, d), mod.spec(mod_k + 2, tm, d),
                  _gain_spec(gain_k, d), _gain_spec(gain_k + 1, d),
                  pl.BlockSpec((d, tf), lambda i, j: (0, j)),
                  pl.BlockSpec((d, tf), lambda i, j: (0, j + n_f)),
                  pl.BlockSpec((tf, d), lambda i, j: (j, 0))],
        out_specs=row_spec,
        out_shape=jax.ShapeDtypeStruct((rows, d), F32),
        scratch_shapes=[pltpu.VMEM((tm, d), BF16), pltpu.VMEM((tm, d), F32)],
        compiler_params=_cparams(2),
        name="macaron_ffn",
    )(x, mod.arr, mod.arr, mod.arr, gains, gains, w_in, w_in, w_out)


_PROJ_GROUPS = (("sbq", HW), ("sbk", HW), ("sbv", HW),
                ("gq", HW), ("gk", HW), ("gv", N_HEADS * DV_B), ("gr", N_HEADS * DV_B), ("gb", LANE),
                ("dq", HW), ("dqi", HW), ("dkc", LANE), ("dvc", LANE), ("dkiw", LANE))
_PROJ_OFFSETS = tuple(int(v) for v in np.cumsum([0] + [w for _, w in _PROJ_GROUPS]))
_PROJ_WIDTH = _PROJ_OFFSETS[-1]
_PROJ_OUTPUTS = (("sbq", HW), ("sbk", HW), ("sbv", HW), ("gq", HW), ("gk", HW),
                 ("gv", N_HEADS * DV_B), ("gr", N_HEADS * DV_B), ("gb", HW),
                 ("dq", HW), ("dqi", HW), ("dkc", HEAD_DIM), ("dvc", HEAD_DIM),
                 ("dkiw", HEAD_DIM), ("dkiw", LANE))
_WI_LANE = HEAD_DIM


def _relayout_w_in(w_in):
    d = w_in.shape[0]
    widths = (HW, HW, HW, HW, HW, N_HEADS * DV_B, N_HEADS * DV_B, GATE_RANK,
              HW, HEAD_DIM, HEAD_DIM, HW, HEAD_DIM, N_HEADS)
    offs = np.cumsum((0,) + widths)
    col = lambda i: w_in[:, offs[i]:offs[i + 1]]
    pad = lambda a, w: jnp.pad(a, ((0, 0), (0, w - a.shape[1])))
    parts = [col(0), col(1), col(2), col(3), col(4), col(5), col(6), pad(col(7), LANE),
             col(8), col(11), pad(col(9), LANE), pad(col(10), LANE),
             pad(jnp.concatenate([col(12), col(13)], axis=1), LANE)]
    out = jnp.concatenate(parts, axis=1)
    assert out.shape == (d, _PROJ_WIDTH)
    return out


def _proj_kernel(x_ref, sh_ref, sc_ref, g_ref, w_ref, wgate_ref, bgate_ref, *o_refs):
    h = (_rms_rows(x_ref[...], g_ref[...]) * (1.0 + sc_ref[...]) + sh_ref[...]).astype(BF16)
    for gi, (name, _) in enumerate(_PROJ_GROUPS):
        p = _dot(h, w_ref[:, _PROJ_OFFSETS[gi]:_PROJ_OFFSETS[gi + 1]])
        if name == "gb":
            pre = _dot(p.astype(BF16), wgate_ref[...]) + bgate_ref[...]
            p = _log_sigmoid(pre) * (1.0 / GATE_TAU)
        for o_ref, (out_group, width) in zip(o_refs, _PROJ_OUTPUTS):
            if out_group == name:
                o_ref[...] = p[:, :width]


def _proj_call(x, mod, gains, w_in, w_gate, b_gate, tm):
    rows, d = x.shape
    out_w = [w for _, w in _PROJ_OUTPUTS]
    return pl.pallas_call(
        _proj_kernel,
        grid=(rows // tm,),
        in_specs=[pl.BlockSpec((tm, d), lambda i: (i, 0)),
                  mod.spec(3, tm, d), mod.spec(4, tm, d), _gain_spec(2, d),
                  pl.BlockSpec((d, _PROJ_WIDTH), lambda i: (0, 0)),
                  pl.BlockSpec((LANE, HW), lambda i: (0, 0)),
                  pl.BlockSpec((1, HW), lambda i: (0, 0))],
        out_specs=[pl.BlockSpec((tm, w), lambda i: (i, 0)) for w in out_w],
        out_shape=[jax.ShapeDtypeStruct((rows, w), F32) for w in out_w],
        compiler_params=_cparams(1),
        name="mix_in_proj",
    )(x, mod.arr, mod.arr, gains, w_in, w_gate, b_gate)


def _sb_kernel(q_ref, k_ref, v_ref, u_ref, o_ref, acc_scr, c_scr, *, tq, tk, past):
    i = pl.program_id(1)
    head = _head_lane_id(HW, HEAD_DIM)
    q = q_ref[...] * (HEAD_DIM ** -0.5)
    q_st = jnp.concatenate([jnp.where(head == h, q, 0.0) for h in range(N_HEADS)], axis=0).astype(BF16)
    qpos = past + i * tq + lax.broadcasted_iota(jnp.int32, (tq, 1), 0)
    last_blk = (past + (i + 1) * tq - 2) // tk
    acc_scr[...] = jnp.zeros_like(acc_scr)
    c_scr[...] = jnp.zeros_like(c_scr)
    u = u_ref[...]
    rows = N_HEADS * tq

    def body(n, carry):
        j = last_blk - n
        k0 = pl.multiple_of(j * tk, tk)
        kb = k_ref[pl.ds(k0, tk), :].astype(BF16)
        vb = v_ref[pl.ds(k0, tk), :].astype(BF16)
        kpos = k0 + lax.broadcasted_iota(jnp.int32, (1, tk), 1)
        valid = (kpos < qpos)[None]
        z = _dot_nt(q_st, kb).reshape(N_HEADS, tq, tk)
        log_beta = _log_sigmoid(z)
        log_keep = jnp.where(valid, log_beta - z, 0.0)
        hi, lo = _split_bf16(log_keep.reshape(rows, tk))
        both = _dot(jnp.concatenate([hi, lo], axis=0), u)
        newer = (both[:rows] + both[rows:]).reshape(N_HEADS, tq, tk)
        c = c_scr[...]
        w = jnp.where(valid, jnp.exp(log_beta + newer + c), 0.0).astype(BF16)
        pv = _dot(w.reshape(rows, tk), vb).reshape(N_HEADS, tq, HW)
        acc = acc_scr[...]
        for h in range(N_HEADS):
            acc = acc + jnp.where(head == h, pv[h], 0.0)
        acc_scr[...] = acc
        c_scr[...] = c + newer[:, :, 0:1] + log_keep[:, :, 0:1]
        return carry

    lax.fori_loop(0, last_blk + 1, body, 0)
    o_ref[...] = acc_scr[...]


def _sb_call(q, k_all, v_all, past, tq, tk):
    b, t, _ = q.shape
    l_pad = k_all.shape[1]
    u = (np.arange(tk)[:, None] > np.arange(tk)[None, :]).astype(np.float32)
    return pl.pallas_call(
        functools.partial(_sb_kernel, tq=tq, tk=tk, past=past),
        grid=(b, t // tq),
        in_specs=[pl.BlockSpec((None, tq, HW), lambda bi, i: (bi, i, 0)),
                  pl.BlockSpec((None, l_pad, HW), lambda bi, i: (bi, 0, 0)),
                  pl.BlockSpec((None, l_pad, HW), lambda bi, i: (bi, 0, 0)),
                  pl.BlockSpec((tk, tk), lambda bi, i: (0, 0))],
        out_specs=pl.BlockSpec((None, tq, HW), lambda bi, i: (bi, i, 0)),
        out_shape=jax.ShapeDtypeStruct((b, t, HW), F32),
        scratch_shapes=[pltpu.VMEM((tq, HW), F32), pltpu.VMEM((N_HEADS, tq, 1), F32)],
        compiler_params=_cparams(2),
        name="stick_breaking_attn",
    )(q, k_all, v_all, jnp.asarray(u, BF16))


def _gla_tables(c):
    n_lev = int(math.log2(c))
    assert 1 << n_lev == c
    t = np.arange(c)
    rows = []
    for lev in range(1, n_lev + 1):
        blk = c >> (lev - 1)
        ref = (t // blk) * blk + blk // 2 - 1
        lo, hi = np.minimum(t, ref), np.maximum(t, ref)
        rows.append(((t[None, :] > lo[:, None]) & (t[None, :] <= hi[:, None])).astype(np.float32))
    rows.append((t[None, :] <= t[:, None]).astype(np.float32))
    rows.append((t[None, :] > t[:, None]).astype(np.float32))
    table = np.concatenate(rows, axis=0)
    lvl = np.full((c, c), n_lev + 1, np.int32)
    for lev in range(1, n_lev + 1):
        blk = c >> (lev - 1)
        half = blk // 2
        same = (t[:, None] // blk) == (t[None, :] // blk)
        pair = same & ((t[:, None] % blk) >= half) & ((t[None, :] % blk) < half)
        lvl[pair] = lev
    lvl[t, t] = 0
    return table, np.tile(lvl, (N_HEADS, 1)), n_lev


def _gla_kernel(q_ref, k_ref, v_ref, r_ref, la_ref, s0_ref, tab_ref, lvl_ref, g_ref,
                o_ref, s_ref, s_scr, *, c, n_lev, n_chunks, n_steps):
    j = pl.program_id(1)

    @pl.when(j == 0)
    def _():
        s_scr[...] = s0_ref[...]

    head = _head_lane_id(HW, HEAD_DIM)
    tab = tab_ref[...]
    lvl = lvl_ref[...]
    ones = jnp.ones((c, LANE), BF16)

    def chunk(ci, carry):
        r0 = pl.multiple_of(ci * c, c)
        rows = pl.ds(r0, c)
        q = q_ref[rows, :] * (HEAD_DIM ** -0.5)
        k = k_ref[rows, :]
        la_hi, la_lo = _split_bf16(la_ref[rows, :])
        decays = jnp.exp(_dot(tab, la_hi) + _dot(tab, la_lo))
        att = jnp.zeros((N_HEADS * c, c), F32)
        for lev in range(n_lev + 1):
            if lev == 0:
                ql, kl = q, k
            else:
                e = decays[(lev - 1) * c:lev * c]
                ql, kl = q * e, k * e
            lhs = jnp.concatenate([jnp.where(head == h, ql, 0.0) for h in range(N_HEADS)], axis=0)
            a = _dot_nt(lhs.astype(BF16), kl.astype(BF16))
            att = jnp.where(lvl == lev, a, att)
        q_in = q * decays[n_lev * c:(n_lev + 1) * c]
        k_out = (k * decays[(n_lev + 1) * c:(n_lev + 2) * c]).astype(BF16)
        s = s_scr[...]
        s_b = s.astype(BF16)
        ds_parts = []
        for h in range(N_HEADS):
            vh = v_ref[rows, h * DV_B:(h + 1) * DV_B].astype(BF16)
            o = (_dot(att[h * c:(h + 1) * c].astype(BF16), vh)
                 + _dot(jnp.where(head == h, q_in, 0.0).astype(BF16), s_b))
            y = _rms_rows(o, g_ref[:, h * DV_B:(h + 1) * DV_B])
            o_ref[rows, h * DV_B:(h + 1) * DV_B] = y * _silu(r_ref[rows, h * DV_B:(h + 1) * DV_B])
            ds_parts.append(_dot_tn(k_out, vh)[h * HEAD_DIM:(h + 1) * HEAD_DIM])
        total = _dot_tn(la_hi, ones) + _dot_tn(la_lo, ones)
        s_scr[...] = jnp.exp(total) * s + jnp.concatenate(ds_parts, axis=0)
        return carry

    lax.fori_loop(0, n_chunks, chunk, 0)

    @pl.when(j == n_steps - 1)
    def _():
        s_ref[...] = s_scr[...]


def _gla_call(q, k, v, r, la, s0, g_gla, c):
    b, t, _ = q.shape
    vw = N_HEADS * DV_B
    tc = _pick_tile(t, 512, c)
    n_steps = t // tc
    table, lvl, n_lev = _gla_tables(c)
    qk_spec = pl.BlockSpec((None, tc, HW), lambda bi, j: (bi, j, 0))
    vr_spec = pl.BlockSpec((None, tc, vw), lambda bi, j: (bi, j, 0))
    st_spec = pl.BlockSpec((None, HW, DV_B), lambda bi, j: (bi, 0, 0))
    const = lambda shape: pl.BlockSpec(shape, lambda bi, j: (0, 0))
    return pl.pallas_call(
        functools.partial(_gla_kernel, c=c, n_lev=n_lev, n_chunks=tc // c, n_steps=n_steps),
        grid=(b, n_steps),
        in_specs=[qk_spec, qk_spec, vr_spec, vr_spec, qk_spec, st_spec,
                  const(table.shape), const(lvl.shape), const((1, vw))],
        out_specs=[vr_spec, st_spec],
        out_shape=[jax.ShapeDtypeStruct((b, t, vw), F32), jax.ShapeDtypeStruct((b, HW, DV_B), F32)],
        scratch_shapes=[pltpu.VMEM((HW, DV_B), F32)],
        compiler_params=_cparams(2),
        name="gated_linear_attn",
    )(q, k, v, r, la, s0, jnp.asarray(table, BF16), jnp.asarray(lvl), g_gla)


_COUNT_ROWS = 32


def _key_to_f32(key):
    key = jnp.maximum(key, _KEY_NEG_INF)
    bits = jnp.where(key < 0, key ^ 0x7FFFFFFF, key)
    return lax.bitcast_convert_type(bits, F32)


def _dsa_kernel(q_ref, qi_ref, wi_ref, kc_ref, vc_ref, ki_ref, eye_ref, tri_ref, o_ref,
                sc_scr, lg_scr, vct_scr, acc_scr, *, tq, tk, past, l_real, n_sel):
    i = pl.program_id(1)
    scale = HEAD_DIM ** -0.5
    eye = eye_ref[...]
    l_pad = sc_scr.shape[0]

    @pl.when(i == 0)
    def _():
        vct_scr[...] = _dot_nt(eye[:HEAD_DIM, :HEAD_DIM], vc_ref[...].astype(BF16)).astype(BF16)

    q_t = _dot_nt(eye, (q_ref[...] * scale).astype(BF16)).astype(BF16)
    qi_t = _dot_nt(eye, (qi_ref[...] * scale).astype(BF16)).astype(BF16)
    w_t = wi_ref[...].T * (N_HEADS ** -0.5)
    qpos = past + i * tq + lax.broadcasted_iota(jnp.int32, (1, tq), 1)
    chunk_shift = int(math.log2(CHUNK))
    qchunk = lax.shift_right_logical(qpos, chunk_shift)
    k_end = ((past + (i + 1) * tq - 1) // CHUNK + 1) * CHUNK
    n_blk = jnp.minimum((k_end + tk - 1) // tk, l_pad // tk)

    def rows_of(j):
        return pl.ds(pl.multiple_of(j * tk, tk), tk)

    def admissible(j):
        kpos = j * tk + lax.broadcasted_iota(jnp.int32, (tk, 1), 0)
        return (lax.shift_right_logical(kpos, chunk_shift) <= qchunk) & (kpos < l_real)

    def score_blk(j, carry):
        kib = ki_ref[rows_of(j), :].astype(BF16)
        s = jnp.zeros((tk, tq), F32)
        for h in range(N_HEADS):
            sh = _dot(kib, qi_t[h * HEAD_DIM:(h + 1) * HEAD_DIM])
            s = s + w_t[_WI_LANE + h:_WI_LANE + h + 1] * jnp.maximum(sh, 0.0)
        sc_scr[rows_of(j), :] = jnp.where(admissible(j), s, _NEG_INF)
        return carry

    lax.fori_loop(0, n_blk, score_blk, 0)

    def count(pred):
        def blk(j, acc):
            m = jnp.where(pred(sc_scr[rows_of(j), :]), 1.0, 0.0)
            return acc + jnp.sum(m.reshape(tk // _COUNT_ROWS, _COUNT_ROWS, tq), axis=0)
        acc = lax.fori_loop(0, n_blk, blk, jnp.zeros((_COUNT_ROWS, tq), F32))
        return jnp.sum(acc, axis=0, keepdims=True)

    def value_bit(it, tau):
        cand = tau + lax.shift_left(jnp.int32(1), 31 - it)
        thr = _key_to_f32(cand)
        return jnp.where(count(lambda s: s >= thr) >= n_sel, cand, tau)

    tau = lax.fori_loop(0, 32, value_bit, jnp.full((1, tq), _INT_MIN, jnp.int32))
    thr = _key_to_f32(tau)
    need = n_sel - count(lambda s: s > thr)

    tri = tri_ref[...]
    fold = lambda a, op: op(a.reshape(tk // SUBLANE, SUBLANE, tq), axis=0)
    parts = lambda v: tuple(jnp.full((SUBLANE, tq), v, F32) for _ in range(N_HEADS))

    def logit_blk(j, carry):
        m_parts, ties_before = carry
        s = sc_scr[rows_of(j), :]
        tie = s == thr
        rank = _dot(tri, jnp.where(tie, 1.0, 0.0).astype(BF16)) + ties_before
        sel = admissible(j) & ((s > thr) | (tie & (rank <= need)))
        bias = jnp.where(sel, 0.0, _NEG_INF)
        kcb = kc_ref[rows_of(j), :].astype(BF16)
        new_parts = []
        for h in range(N_HEADS):
            lg = _dot(kcb, q_t[h * HEAD_DIM:(h + 1) * HEAD_DIM]) + bias
            lg_scr[h, rows_of(j), :] = lg
            new_parts.append(jnp.maximum(m_parts[h], fold(lg, jnp.max)))
        return tuple(new_parts), rank[tk - 1:tk, :]

    m_parts, _ = lax.fori_loop(0, n_blk, logit_blk, (parts(_NEG_INF), jnp.zeros((1, tq), F32)))
    m_use = []
    for h in range(N_HEADS):
        m = jnp.max(m_parts[h], axis=0, keepdims=True)
        m_use.append(jnp.where(m == _NEG_INF, 0.0, m))
    acc_scr[...] = jnp.zeros_like(acc_scr)

    def pv_blk(j, l_parts):
        vct = vct_scr[:, rows_of(j)]
        new_parts = []
        for h in range(N_HEADS):
            hd = slice(h * HEAD_DIM, (h + 1) * HEAD_DIM)
            p = jnp.exp(lg_scr[h, rows_of(j), :] - m_use[h])
            new_parts.append(l_parts[h] + fold(p, jnp.sum))
            acc_scr[hd, :] += _dot(vct, p.astype(BF16))
        return tuple(new_parts)

    l_parts = lax.fori_loop(0, n_blk, pv_blk, parts(0.0))
    out_t = jnp.concatenate(
        [acc_scr[h * HEAD_DIM:(h + 1) * HEAD_DIM, :] / jnp.sum(l_parts[h], axis=0, keepdims=True)
         for h in range(N_HEADS)], axis=0)
    o_ref[...] = out_t.T


def _dsa_call(q, qi, wi, kc, vc, ki, past, l_real, tq, tk):
    b, t, _ = q.shape
    l_pad = kc.shape[1]
    n_sel = min(TOPK_MAX, l_real // 4)
    assert tk >= n_sel and l_pad % tk == 0 and tk % _COUNT_ROWS == 0
    row = lambda w: pl.BlockSpec((None, tq, w), lambda bi, i: (bi, i, 0))
    full = pl.BlockSpec((None, l_pad, HEAD_DIM), lambda bi, i: (bi, 0, 0))
    const = lambda n: pl.BlockSpec((n, n), lambda bi, i: (0, 0))
    eye = np.eye(HW, dtype=np.float32)
    tri = (np.arange(tk)[:, None] >= np.arange(tk)[None, :]).astype(np.float32)
    return pl.pallas_call(
        functools.partial(_dsa_kernel, tq=tq, tk=tk, past=past, l_real=l_real, n_sel=n_sel),
        grid=(b, t // tq),
        in_specs=[row(HW), row(HW), row(LANE), full, full, full, const(HW), const(tk)],
        out_specs=row(HW),
        out_shape=jax.ShapeDtypeStruct((b, t, HW), F32),
        scratch_shapes=[pltpu.VMEM((l_pad, tq), F32), pltpu.VMEM((N_HEADS, l_pad, tq), F32),
                        pltpu.VMEM((HEAD_DIM, l_pad), BF16), pltpu.VMEM((HW, tq), F32)],
        compiler_params=_cparams(2),
        name="indexer_sparse_attn",
    )(q, qi, wi, kc, vc, ki, jnp.asarray(eye, BF16), jnp.asarray(tri, BF16))


def _mix_out_kernel(x_ref, oa_ref, ob_ref, oc_ref, ga_ref, g_ref, w_ref, o_ref):
    wa = HW
    wb = wa + N_HEADS * DV_B
    y = (_dot(oa_ref[...].astype(BF16), w_ref[0:wa, :])
         + _dot(ob_ref[...].astype(BF16), w_ref[wa:wb, :])
         + _dot(oc_ref[...].astype(BF16), w_ref[wb:wb + HW, :]))
    o_ref[...] = x_ref[...] + ga_ref[...] * _rms_rows(y, g_ref[...])


def _mix_out_call(x, oa, ob, oc, mod, gains, w_out, tm):
    rows, d = x.shape
    mixw = w_out.shape[0]
    row = lambda w: pl.BlockSpec((tm, w), lambda i: (i, 0))
    return pl.pallas_call(
        _mix_out_kernel,
        grid=(rows // tm,),
        in_specs=[row(d), row(HW), row(N_HEADS * DV_B), row(HW),
                  mod.spec(5, tm, d), _gain_spec(3, d),
                  pl.BlockSpec((mixw, d), lambda i: (0, 0))],
        out_specs=row(d),
        out_shape=jax.ShapeDtypeStruct((rows, d), F32),
        compiler_params=_cparams(1),
        name="mix_out_proj",
    )(x, oa, ob, oc, mod.arr, gains, w_out)


def _pad_rows(a, n):
    return jnp.pad(a, ((0, 0), (0, n - a.shape[1]), (0, 0)))


def _encoder_layer(x, n_seq, t, mod, past, w):
    rows, d = x.shape
    tm = _pick_tile(t, 512, SUBLANE) if not mod.per_token else rows
    x = _ffn_call(x, mod, 0, w["gains"], 0, w["w_f1_in"], w["w_f1_out"], tm)
    (sbq, sbk, sbv, gq, gk, gv, gr, la, dq, dqi, dkc, dvc, dki, dwi) = _proj_call(
        x, mod, w["gains"], w["w_in"], w["w_gate"], w["b_gate"], tm)
    seq = lambda a: a.reshape(n_seq, t, a.shape[-1])
    p_len = 0 if past is None else past["k_sb"].shape[1]
    l_real = p_len + t
    tk = 256 if l_real >= 256 else LANE
    l_pad = -(-l_real // tk) * tk
    cat = (lambda new, old: _pad_rows(seq(new) if old is None else jnp.concatenate([old, seq(new)], axis=1),
                                      l_pad))

    tq = _pick_tile(t, 256, SUBLANE)
    k_all = cat(sbk, None if past is None else past["k_sb"])
    v_all = cat(sbv, None if past is None else past["v_sb"])
    o_a = _sb_call(seq(sbq), k_all, v_all, p_len, tq, tk)

    c = CHUNK if t % CHUNK == 0 else t
    s0 = jnp.zeros((n_seq, HW, DV_B), F32) if past is None else past["gla"]
    o_b, s_new = _gla_call(seq(gq), seq(gk), seq(gv), seq(gr), seq(la), s0, w["g_gla"], c)

    t_c = t if t % LANE == 0 else -(-t // LANE) * LANE
    tq_c = _pick_tile(t_c, 256, LANE)
    qpad = lambda a: _pad_rows(seq(a), t_c)
    o_c = _dsa_call(qpad(dq), qpad(dqi), qpad(dwi),
                    cat(dkc, None if past is None else past["k_dsa"]),
                    cat(dvc, None if past is None else past["v_dsa"]),
                    cat(dki, None if past is None else past["k_idx"]),
                    p_len, l_real, tq_c, tk)[:, :t]

    flat = lambda a: a.reshape(rows, a.shape[-1])
    x = _mix_out_call(x, flat(o_a), flat(o_b), flat(o_c), mod, w["gains"], w["w_out"], tm)
    x = _ffn_call(x, mod, 6, w["gains"], 4, w["w_f2_in"], w["w_f2_out"], tm)
    new_rows = (seq(sbk).reshape(n_seq, t, N_HEADS, HEAD_DIM), seq(sbv).reshape(n_seq, t, N_HEADS, HEAD_DIM),
                seq(dkc), seq(dvc), seq(dki), s_new.reshape(n_seq, N_HEADS, HEAD_DIM, DV_B))
    return x, new_rows


def kernel(x_prompt, x_sample, cache_k_sb, cache_v_sb, cache_k_dsa, cache_v_dsa, cache_k_idx, state_gla,
           c_prompt, c_sample, w_ada, b_ada, norm_gains, w_ffn1_in, w_ffn1_out, w_ffn2_in, w_ffn2_out,
           w_in, w_gla_gate, b_gla_gate, gla_norm, w_out):
    bp, tp, d = x_prompt.shape
    bs, ts, _ = x_sample.shape
    depth = w_ada.shape[0]
    p_len = cache_k_sb.shape[2]

    mods = _mod_call(jnp.concatenate([c_prompt, c_sample], axis=0), w_ada.astype(BF16), b_ada)
    xp = x_prompt.reshape(bp * tp, d)
    xs = x_sample.reshape(bs * ts, d)
    tm_p = _pick_tile(tp, 512, SUBLANE)
    acc_p, acc_s = [], []
    for l in range(depth):
        w_gate = jnp.pad(w_gla_gate[l], ((0, LANE - GATE_RANK), (0, 0))).astype(BF16)
        w = dict(gains=norm_gains[l].reshape(-1, 1, d),
                 w_f1_in=w_ffn1_in[l].astype(BF16), w_f1_out=w_ffn1_out[l].astype(BF16),
                 w_f2_in=w_ffn2_in[l].astype(BF16), w_f2_out=w_ffn2_out[l].astype(BF16),
                 w_in=_relayout_w_in(w_in[l]).astype(BF16), w_gate=w_gate,
                 b_gate=b_gla_gate[l].reshape(1, HW), g_gla=gla_norm[l].reshape(1, N_HEADS * DV_B),
                 w_out=w_out[l].astype(BF16))
        mod_p = _Mod(mods[l, :bp].reshape(bp * N_MOD, 1, d), False, tp // tm_p)
        mod_s_rows = jnp.repeat(mods[l, bp:].reshape(bs, N_MOD, d), ts, axis=0)
        mod_s = _Mod(jnp.transpose(mod_s_rows, (1, 0, 2)), True)
        past = dict(k_sb=cache_k_sb[l].reshape(bs, p_len, HW), v_sb=cache_v_sb[l].reshape(bs, p_len, HW),
                    k_dsa=cache_k_dsa[l], v_dsa=cache_v_dsa[l], k_idx=cache_k_idx[l],
                    gla=state_gla[l].reshape(bs, HW, DV_B))
        xp, rows_p = _encoder_layer(xp, bp, tp, mod_p, None, w)
        xs, rows_s = _encoder_layer(xs, bs, ts, mod_s, past, w)
        acc_p.append(rows_p)
        acc_s.append(rows_s)
    field = lambda acc, i: jnp.stack([r[i] for r in acc], axis=0)
    return (xp.reshape(bp, tp, d), xs.reshape(bs, ts, d),
            *(field(acc_p, i) for i in range(6)), *(field(acc_s, i) for i in range(6)))
```

```python
import functools
import math

import numpy as np
import jax
import jax.numpy as jnp
from jax import lax
from jax.experimental import pallas as pl
from jax.experimental.pallas import tpu as pltpu

F32 = jnp.float32
BF16 = jnp.bfloat16

CHUNK = 64
N_HEADS = 4
HEAD_DIM = 64
HW = N_HEADS * HEAD_DIM
DV_B = 128
GATE_RANK = 16
GATE_TAU = 16.0
TOPK_MAX = 256
EPS = 1e-6
MACARON_W = 0.5
N_MOD = 9

LANE = 128
SUBLANE = 8
VMEM_LIMIT_BYTES = 56 * 1024 * 1024

_NEG_INF = float("-inf")
_KEY_NEG_INF = -2**31 + 0x7FFFFF
_INT_MIN = -2**31


def _pick_tile(n, target, mult):
    if n <= target:
        return n
    t = (target // mult) * mult
    while t >= mult:
        if n % t == 0:
            return t
        t -= mult
    return n


def _cparams(n_axes):
    return pltpu.CompilerParams(dimension_semantics=("arbitrary",) * n_axes,
                                vmem_limit_bytes=VMEM_LIMIT_BYTES)


def _rms_rows(x, g):
    ms = jnp.mean(x * x, axis=-1, keepdims=True)
    return x * lax.rsqrt(ms + EPS) * g


def _silu(x):
    return x * (1.0 / (1.0 + jnp.exp(-x)))


def _log_sigmoid(x):
    return jnp.minimum(x, 0.0) - jnp.log(1.0 + jnp.exp(-jnp.abs(x)))


def _dot(a, b):
    return jnp.dot(a, b, preferred_element_type=F32)


def _dot_nt(a, b):
    return lax.dot_general(a, b, (((1,), (1,)), ((), ())), preferred_element_type=F32)


def _dot_tn(a, b):
    return lax.dot_general(a, b, (((0,), (0,)), ((), ())), preferred_element_type=F32)


def _split_bf16(x):
    hi = x.astype(BF16)
    lo = (x - hi.astype(F32)).astype(BF16)
    return hi, lo


def _head_lane_id(width, per_head):
    return lax.broadcasted_iota(jnp.int32, (1, width), 1) // per_head


def _mod_kernel(c_ref, w_ref, b_ref, o_ref):
    a = _silu(c_ref[...]).astype(BF16)
    o_ref[...] = _dot(a, w_ref[...]) + b_ref[...]


def _mod_call(c, w_ada, b_ada):
    depth, d, nd = w_ada.shape
    n = c.shape[0]
    tn = _pick_tile(nd, 1536, LANE)
    return pl.pallas_call(
        _mod_kernel,
        grid=(depth, nd // tn),
        in_specs=[pl.BlockSpec((n, d), lambda l, j: (0, 0)),
                  pl.BlockSpec((None, d, tn), lambda l, j: (l, 0, j)),
                  pl.BlockSpec((None, 1, tn), lambda l, j: (l, 0, j))],
        out_specs=pl.BlockSpec((None, n, tn), lambda l, j: (l, 0, j)),
        out_shape=jax.ShapeDtypeStruct((depth, n, nd), F32),
        compiler_params=_cparams(2),
        name="adaln_mod",
    )(c, w_ada, b_ada.reshape(depth, 1, nd))


class _Mod:
    def __init__(self, arr, per_token, tiles_per_seq=None):
        self.arr, self.per_token, self.tiles_per_seq = arr, per_token, tiles_per_seq

    def spec(self, k, tm, d):
        if self.per_token:
            return pl.BlockSpec((None, tm, d), lambda i, *_: (k, i, 0))
        tps = self.tiles_per_seq
        return pl.BlockSpec((None, 1, d), lambda i, *_: ((i // tps) * N_MOD + k, 0, 0))


def _gain_spec(k, d):
    return pl.BlockSpec((None, 1, d), lambda i, *_: (k, 0, 0))


def _ffn_kernel(x_ref, sh_ref, sc_ref, ga_ref, gin_ref, gout_ref, wi_ref, wo_ref, o_ref, *, f, tf):
    x = x_ref[...]
    h = (_rms_rows(x, gin_ref[...]) * (1.0 + sc_ref[...]) + sh_ref[...]).astype(BF16)
    y = None
    for c0 in range(0, f, tf):
        gate = _dot(h, wi_ref[:, c0:c0 + tf])
        up = _dot(h, wi_ref[:, f + c0:f + c0 + tf])
        part = _dot((_silu(gate) * up).astype(BF16), wo_ref[c0:c0 + tf, :])
        y = part if y is None else y + part
    o_ref[...] = x + MACARON_W * ga_ref[...] * _rms_rows(y, gout_ref[...])


def _ffn_call(x, mod, mod_k, gains, gain_k, w_in, w_out, tm):
    rows, d = x.shape
    f = w_out.shape[0]
    tf = _pick_tile(f, 1408, LANE)
    row_spec = pl.BlockSpec((tm, d), lambda i: (i, 0))
    resident = lambda shape: pl.BlockSpec(shape, lambda i: (0, 0), pipeline_mode=pl.Buffered(1))
    return pl.pallas_call(
        functools.partial(_ffn_kernel, f=f, tf=tf),
        grid=(rows // tm,),
        in_specs=[row_spec,
                  mod.spec(mod_k, tm, d), mod.spec(mod_k + 1, tm, d), mod.spec(mod_k + 2, tm, d),
                  _gain_spec(gain_k, d), _gain_spec(gain_k + 1, d),
                  resident((d, 2 * f)), resident((f, d))],
        out_specs=row_spec,
        out_shape=jax.ShapeDtypeStruct((rows, d), F32),
        compiler_params=_cparams(1),
        name="macaron_ffn",
    )(x, mod.arr, mod.arr, mod.arr, gains, gains, w_in, w_out)


_PROJ_GROUPS = (("sbq", HW), ("sbk", HW), ("sbv", HW),
                ("gq", HW), ("gk", HW), ("gv", N_HEADS * DV_B), ("gr", N_HEADS * DV_B), ("gb", LANE),
                ("dq", HW), ("dqi", HW), ("dkc", LANE), ("dvc", LANE), ("dkiw", LANE))
_PROJ_OFFSETS = tuple(int(v) for v in np.cumsum([0] + [w for _, w in _PROJ_GROUPS]))
_PROJ_WIDTH = _PROJ_OFFSETS[-1]
_PROJ_OUTPUTS = (("sbq", HW), ("sbk", HW), ("sbv", HW), ("gq", HW), ("gk", HW),
                 ("gv", N_HEADS * DV_B), ("gr", N_HEADS * DV_B), ("gb", HW),
                 ("dq", HW), ("dqi", HW), ("dkc", HEAD_DIM), ("dvc", HEAD_DIM),
                 ("dkiw", HEAD_DIM), ("dkiw", LANE))
_WI_LANE = HEAD_DIM


def _relayout_w_in(w_in):
    d = w_in.shape[0]
    widths = (HW, HW, HW, HW, HW, N_HEADS * DV_B, N_HEADS * DV_B, GATE_RANK,
              HW, HEAD_DIM, HEAD_DIM, HW, HEAD_DIM, N_HEADS)
    offs = np.cumsum((0,) + widths)
    col = lambda i: w_in[:, offs[i]:offs[i + 1]]
    pad = lambda a, w: jnp.pad(a, ((0, 0), (0, w - a.shape[1])))
    parts = [col(0), col(1), col(2), col(3), col(4), col(5), col(6), pad(col(7), LANE),
             col(8), col(11), pad(col(9), LANE), pad(col(10), LANE),
             pad(jnp.concatenate([col(12), col(13)], axis=1), LANE)]
    out = jnp.concatenate(parts, axis=1)
    assert out.shape == (d, _PROJ_WIDTH)
    return out


def _proj_kernel(x_ref, sh_ref, sc_ref, g_ref, w_ref, wgate_ref, bgate_ref, *o_refs):
    h = (_rms_rows(x_ref[...], g_ref[...]) * (1.0 + sc_ref[...]) + sh_ref[...]).astype(BF16)
    for gi, (name, _) in enumerate(_PROJ_GROUPS):
        p = _dot(h, w_ref[:, _PROJ_OFFSETS[gi]:_PROJ_OFFSETS[gi + 1]])
        if name == "gb":
            pre = _dot(p.astype(BF16), wgate_ref[...]) + bgate_ref[...]
            p = _log_sigmoid(pre) * (1.0 / GATE_TAU)
        for o_ref, (out_group, width) in zip(o_refs, _PROJ_OUTPUTS):
            if out_group == name:
                o_ref[...] = p[:, :width]


def _proj_call(x, mod, gains, w_in, w_gate, b_gate, tm):
    rows, d = x.shape
    out_w = [w for _, w in _PROJ_OUTPUTS]
    return pl.pallas_call(
        _proj_kernel,
        grid=(rows // tm,),
        in_specs=[pl.BlockSpec((tm, d), lambda i: (i, 0)),
                  mod.spec(3, tm, d), mod.spec(4, tm, d), _gain_spec(2, d),
                  pl.BlockSpec((d, _PROJ_WIDTH), lambda i: (0, 0)),
                  pl.BlockSpec((LANE, HW), lambda i: (0, 0)),
                  pl.BlockSpec((1, HW), lambda i: (0, 0))],
        out_specs=[pl.BlockSpec((tm, w), lambda i: (i, 0)) for w in out_w],
        out_shape=[jax.ShapeDtypeStruct((rows, w), F32) for w in out_w],
        compiler_params=_cparams(1),
        name="mix_in_proj",
    )(x, mod.arr, mod.arr, gains, w_in, w_gate, b_gate)


def _sb_kernel(q_ref, k_ref, v_ref, u_ref, o_ref, acc_scr, c_scr, *, tq, tk, past):
    i = pl.program_id(1)
    head = _head_lane_id(HW, HEAD_DIM)
    q = q_ref[...] * (HEAD_DIM ** -0.5)
    q_st = jnp.concatenate([jnp.where(head == h, q, 0.0) for h in range(N_HEADS)], axis=0).astype(BF16)
    qpos = past + i * tq + lax.broadcasted_iota(jnp.int32, (tq, 1), 0)
    last_blk = (past + (i + 1) * tq - 2) // tk
    acc_scr[...] = jnp.zeros_like(acc_scr)
    c_scr[...] = jnp.zeros_like(c_scr)
    u = u_ref[...]
    rows = N_HEADS * tq

    def body(n, carry):
        j = last_blk - n
        k0 = pl.multiple_of(j * tk, tk)
        kb = k_ref[pl.ds(k0, tk), :].astype(BF16)
        vb = v_ref[pl.ds(k0, tk), :].astype(BF16)
        kpos = k0 + lax.broadcasted_iota(jnp.int32, (1, tk), 1)
        valid = (kpos < qpos)[None]
        z = _dot_nt(q_st, kb).reshape(N_HEADS, tq, tk)
        log_beta = _log_sigmoid(z)
        log_keep = jnp.where(valid, log_beta - z, 0.0)
        hi, lo = _split_bf16(log_keep.reshape(rows, tk))
        both = _dot(jnp.concatenate([hi, lo], axis=0), u)
        newer = (both[:rows] + both[rows:]).reshape(N_HEADS, tq, tk)
        c = c_scr[...]
        w = jnp.where(valid, jnp.exp(log_beta + newer + c), 0.0).astype(BF16)
        pv = _dot(w.reshape(rows, tk), vb).reshape(N_HEADS, tq, HW)
        acc = acc_scr[...]
        for h in range(N_HEADS):
            acc = acc + jnp.where(head == h, pv[h], 0.0)
        acc_scr[...] = acc
        c_scr[...] = c + newer[:, :, 0:1] + log_keep[:, :, 0:1]
        return carry

    lax.fori_loop(0, last_blk + 1, body, 0)
    o_ref[...] = acc_scr[...]


def _sb_call(q, k_all, v_all, past, tq, tk):
    b, t, _ = q.shape
    l_pad = k_all.shape[1]
    u = (np.arange(tk)[:, None] > np.arange(tk)[None, :]).astype(np.float32)
    return pl.pallas_call(
        functools.partial(_sb_kernel, tq=tq, tk=tk, past=past),
        grid=(b, t // tq),
        in_specs=[pl.BlockSpec((None, tq, HW), lambda bi, i: (bi, i, 0)),
                  pl.BlockSpec((None, l_pad, HW), lambda bi, i: (bi, 0, 0)),
                  pl.BlockSpec((None, l_pad, HW), lambda bi, i: (bi, 0, 0)),
                  pl.BlockSpec((tk, tk), lambda bi, i: (0, 0))],
        out_specs=pl.BlockSpec((None, tq, HW), lambda bi, i: (bi, i, 0)),
        out_shape=jax.ShapeDtypeStruct((b, t, HW), F32),
        scratch_shapes=[pltpu.VMEM((tq, HW), F32), pltpu.VMEM((N_HEADS, tq, 1), F32)],
        compiler_params=_cparams(2),
        name="stick_breaking_attn",
    )(q, k_all, v_all, jnp.asarray(u, BF16))


def _gla_tables(c):
    n_lev = int(math.log2(c))
    assert 1 << n_lev == c
    t = np.arange(c)
    rows = []
    for lev in range(1, n_lev + 1):
        blk = c >> (lev - 1)
        ref = (t // blk) * blk + blk // 2 - 1
        lo, hi = np.minimum(t, ref), np.maximum(t, ref)
        rows.append(((t[None, :] > lo[:, None]) & (t[None, :] <= hi[:, None])).astype(np.float32))
    rows.append((t[None, :] <= t[:, None]).astype(np.float32))
    rows.append((t[None, :] > t[:, None]).astype(np.float32))
    table = np.concatenate(rows, axis=0)
    lvl = np.full((c, c), n_lev + 1, np.int32)
    for lev in range(1, n_lev + 1):
        blk = c >> (lev - 1)
        half = blk // 2
        same = (t[:, None] // blk) == (t[None, :] // blk)
        pair = same & ((t[:, None] % blk) >= half) & ((t[None, :] % blk) < half)
        lvl[pair] = lev
    lvl[t, t] = 0
    return table, np.tile(lvl, (N_HEADS, 1)), n_lev


def _gla_kernel(q_ref, k_ref, v_ref, r_ref, la_ref, s0_ref, tab_ref, lvl_ref, g_ref,
                o_ref, s_ref, s_scr, *, c, n_lev, n_chunks, n_steps):
    j = pl.program_id(1)

    @pl.when(j == 0)
    def _():
        s_scr[...] = s0_ref[...]

    head = _head_lane_id(HW, HEAD_DIM)
    tab = tab_ref[...]
    lvl = lvl_ref[...]
    ones = jnp.ones((c, LANE), BF16)

    def chunk(ci, carry):
        r0 = pl.multiple_of(ci * c, c)
        rows = pl.ds(r0, c)
        q = q_ref[rows, :] * (HEAD_DIM ** -0.5)
        k = k_ref[rows, :]
        la_hi, la_lo = _split_bf16(la_ref[rows, :])
        decays = jnp.exp(_dot(tab, la_hi) + _dot(tab, la_lo))
        att = jnp.zeros((N_HEADS * c, c), F32)
        for lev in range(n_lev + 1):
            if lev == 0:
                ql, kl = q, k
            else:
                e = decays[(lev - 1) * c:lev * c]
                ql, kl = q * e, k * e
            lhs = jnp.concatenate([jnp.where(head == h, ql, 0.0) for h in range(N_HEADS)], axis=0)
            a = _dot_nt(lhs.astype(BF16), kl.astype(BF16))
            att = jnp.where(lvl == lev, a, att)
        q_in = q * decays[n_lev * c:(n_lev + 1) * c]
        k_out = (k * decays[(n_lev + 1) * c:(n_lev + 2) * c]).astype(BF16)
        s = s_scr[...]
        s_b = s.astype(BF16)
        ds_parts = []
        for h in range(N_HEADS):
            vh = v_ref[rows, h * DV_B:(h + 1) * DV_B].astype(BF16)
            o = (_dot(att[h * c:(h + 1) * c].astype(BF16), vh)
                 + _dot(jnp.where(head == h, q_in, 0.0).astype(BF16), s_b))
            y = _rms_rows(o, g_ref[:, h * DV_B:(h + 1) * DV_B])
            o_ref[rows, h * DV_B:(h + 1) * DV_B] = y * _silu(r_ref[rows, h * DV_B:(h + 1) * DV_B])
            ds_parts.append(_dot_tn(k_out, vh)[h * HEAD_DIM:(h + 1) * HEAD_DIM])
        total = _dot_tn(la_hi, ones) + _dot_tn(la_lo, ones)
        s_scr[...] = jnp.exp(total) * s + jnp.concatenate(ds_parts, axis=0)
        return carry

    lax.fori_loop(0, n_chunks, chunk, 0)

    @pl.when(j == n_steps - 1)
    def _():
        s_ref[...] = s_scr[...]


def _gla_call(q, k, v, r, la, s0, g_gla, c):
    b, t, _ = q.shape
    vw = N_HEADS * DV_B
    tc = _pick_tile(t, 512, c)
    n_steps = t // tc
    table, lvl, n_lev = _gla_tables(c)
    qk_spec = pl.BlockSpec((None, tc, HW), lambda bi, j: (bi, j, 0))
    vr_spec = pl.BlockSpec((None, tc, vw), lambda bi, j: (bi, j, 0))
    st_spec = pl.BlockSpec((None, HW, DV_B), lambda bi, j: (bi, 0, 0))
    const = lambda shape: pl.BlockSpec(shape, lambda bi, j: (0, 0))
    return pl.pallas_call(
        functools.partial(_gla_kernel, c=c, n_lev=n_lev, n_chunks=tc // c, n_steps=n_steps),
        grid=(b, n_steps),
        in_specs=[qk_spec, qk_spec, vr_spec, vr_spec, qk_spec, st_spec,
                  const(table.shape), const(lvl.shape), const((1, vw))],
        out_specs=[vr_spec, st_spec],
        out_shape=[jax.ShapeDtypeStruct((b, t, vw), F32), jax.ShapeDtypeStruct((b, HW, DV_B), F32)],
        scratch_shapes=[pltpu.VMEM((HW, DV_B), F32)],
        compiler_params=_cparams(2),
        name="gated_linear_attn",
    )(q, k, v, r, la, s0, jnp.asarray(table, BF16), jnp.asarray(lvl), g_gla)


_COUNT_ROWS = 32


def _key_to_f32(key):
    key = jnp.maximum(key, _KEY_NEG_INF)
    bits = jnp.where(key < 0, key ^ 0x7FFFFFFF, key)
    return lax.bitcast_convert_type(bits, F32)


def _dsa_kernel(q_ref, qi_ref, wi_ref, kc_ref, vc_ref, ki_ref, eye_ref, tri_ref, o_ref,
                sc_scr, hi_scr, lo_scr, lg_scr, vct_scr, acc_scr, *, tq, tk, past, l_real, n_sel):
    i = pl.program_id(1)
    scale = HEAD_DIM ** -0.5
    eye = eye_ref[...]
    l_pad = sc_scr.shape[0]

    @pl.when(i == 0)
    def _():
        vct_scr[...] = _dot_nt(eye[:HEAD_DIM, :HEAD_DIM], vc_ref[...].astype(BF16)).astype(BF16)

    q_t = _dot_nt(eye, (q_ref[...] * scale).astype(BF16)).astype(BF16)
    qi_t = _dot_nt(eye, (qi_ref[...] * scale).astype(BF16)).astype(BF16)
    w_t = wi_ref[...].T * (N_HEADS ** -0.5)
    qpos = past + i * tq + lax.broadcasted_iota(jnp.int32, (1, tq), 1)
    chunk_shift = int(math.log2(CHUNK))
    qchunk = lax.shift_right_logical(qpos, chunk_shift)
    k_end = ((past + (i + 1) * tq - 1) // CHUNK + 1) * CHUNK
    n_blk = jnp.minimum((k_end + tk - 1) // tk, l_pad // tk)

    def rows_of(j):
        return pl.ds(pl.multiple_of(j * tk, tk), tk)

    def admissible(j):
        kpos = j * tk + lax.broadcasted_iota(jnp.int32, (tk, 1), 0)
        return (lax.shift_right_logical(kpos, chunk_shift) <= qchunk) & (kpos < l_real)

    def score_blk(j, carry):
        kib = ki_ref[rows_of(j), :].astype(BF16)
        s = jnp.zeros((tk, tq), F32)
        for h in range(N_HEADS):
            sh = _dot(kib, qi_t[h * HEAD_DIM:(h + 1) * HEAD_DIM])
            s = s + w_t[_WI_LANE + h:_WI_LANE + h + 1] * jnp.maximum(sh, 0.0)
        s = jnp.where(admissible(j), jnp.where(s == 0.0, 0.0, s), _NEG_INF)
        sc_scr[rows_of(j), :] = s
        bits = lax.bitcast_convert_type(s, jnp.int32)
        key = jnp.where(bits < 0, bits ^ 0x7FFFFFFF, bits)
        hi_scr[rows_of(j), :] = lax.shift_right_arithmetic(key, 16).astype(jnp.int16)
        lo_scr[rows_of(j), :] = ((key & 0xFFFF) - 32768).astype(jnp.int16)
        return carry

    lax.fori_loop(0, n_blk, score_blk, 0)

    def count(ref, pred, one, zero):
        def blk(j, acc):
            m = jnp.where(pred(ref[rows_of(j), :]), one, zero).reshape(tk // _COUNT_ROWS, _COUNT_ROWS, tq)
            for r in range(tk // _COUNT_ROWS):
                acc = acc + m[r]
            return acc
        acc = lax.fori_loop(0, n_blk, blk, jnp.zeros((_COUNT_ROWS, tq), one.dtype))
        return jnp.sum(acc.astype(jnp.int32), axis=0, keepdims=True)

    one16, zero16 = jnp.int16(1), jnp.int16(0)

    def bisect16(ref, want):
        def bit(it, tau):
            cand = tau + lax.shift_left(jnp.int32(1), 15 - it)
            c16 = cand.astype(jnp.int16)
            return jnp.where(count(ref, lambda a: a >= c16, one16, zero16) >= want, cand, tau)
        return lax.fori_loop(0, 16, bit, jnp.full((1, tq), -32768, jnp.int32))

    hi_star = bisect16(hi_scr, n_sel)
    h16 = hi_star.astype(jnp.int16)
    want_lo = n_sel - count(hi_scr, lambda a: a > h16, one16, zero16)

    def mask_lo(j, carry):
        lo_scr[rows_of(j), :] = jnp.where(hi_scr[rows_of(j), :] == h16, lo_scr[rows_of(j), :],
                                          jnp.int16(-32768))
        return carry

    lax.fori_loop(0, n_blk, mask_lo, 0)
    lo_star = bisect16(lo_scr, want_lo)
    thr = _key_to_f32(lax.shift_left(hi_star, 16) + (lo_star + 32768))
    need = (n_sel - count(sc_scr, lambda a: a > thr, jnp.float32(1.0), jnp.float32(0.0))).astype(F32)

    tri = tri_ref[...]
    fold = lambda a, op: op(a.reshape(tk // SUBLANE, SUBLANE, tq), axis=0)
    parts = lambda v: tuple(jnp.full((SUBLANE, tq), v, F32) for _ in range(N_HEADS))

    def logit_blk(j, carry):
        m_parts, ties_before = carry
        s = sc_scr[rows_of(j), :]
        tie = s == thr
        rank = _dot(tri, jnp.where(tie, 1.0, 0.0).astype(BF16)) + ties_before
        sel = admissible(j) & ((s > thr) | (tie & (rank <= need)))
        bias = jnp.where(sel, 0.0, _NEG_INF)
        kcb = kc_ref[rows_of(j), :].astype(BF16)
        new_parts = []
        for h in range(N_HEADS):
            lg = _dot(kcb, q_t[h * HEAD_DIM:(h + 1) * HEAD_DIM]) + bias
            lg_scr[h, rows_of(j), :] = lg
            new_parts.append(jnp.maximum(m_parts[h], fold(lg, jnp.max)))
        return tuple(new_parts), rank[tk - 1:tk, :]

    m_parts, _ = lax.fori_loop(0, n_blk, logit_blk, (parts(_NEG_INF), jnp.zeros((1, tq), F32)))
    m_use = []
    for h in range(N_HEADS):
        m = jnp.max(m_parts[h], axis=0, keepdims=True)
        m_use.append(jnp.where(m == _NEG_INF, 0.0, m))
    acc_scr[...] = jnp.zeros_like(acc_scr)

    def pv_blk(j, l_parts):
        vct = vct_scr[:, rows_of(j)]
        new_parts = []
        for h in range(N_HEADS):
            hd = slice(h * HEAD_DIM, (h + 1) * HEAD_DIM)
            p = jnp.exp(lg_scr[h, rows_of(j), :] - m_use[h])
            new_parts.append(l_parts[h] + fold(p, jnp.sum))
            acc_scr[hd, :] += _dot(vct, p.astype(BF16))
        return tuple(new_parts)

    l_parts = lax.fori_loop(0, n_blk, pv_blk, parts(0.0))
    out_t = jnp.concatenate(
        [acc_scr[h * HEAD_DIM:(h + 1) * HEAD_DIM, :] / jnp.sum(l_parts[h], axis=0, keepdims=True)
         for h in range(N_HEADS)], axis=0)
    o_ref[...] = out_t.T


def _dsa_call(q, qi, wi, kc, vc, ki, past, l_real, tq, tk):
    b, t, _ = q.shape
    l_pad = kc.shape[1]
    n_sel = min(TOPK_MAX, l_real // 4)
    assert tk >= n_sel and l_pad % tk == 0 and tk % _COUNT_ROWS == 0
    row = lambda w: pl.BlockSpec((None, tq, w), lambda bi, i: (bi, i, 0))
    full = pl.BlockSpec((None, l_pad, HEAD_DIM), lambda bi, i: (bi, 0, 0))
    const = lambda n: pl.BlockSpec((n, n), lambda bi, i: (0, 0))
    eye = np.eye(HW, dtype=np.float32)
    tri = (np.arange(tk)[:, None] >= np.arange(tk)[None, :]).astype(np.float32)
    return pl.pallas_call(
        functools.partial(_dsa_kernel, tq=tq, tk=tk, past=past, l_real=l_real, n_sel=n_sel),
        grid=(b, t // tq),
        in_specs=[row(HW), row(HW), row(LANE), full, full, full, const(HW), const(tk)],
        out_specs=row(HW),
        out_shape=jax.ShapeDtypeStruct((b, t, HW), F32),
        scratch_shapes=[pltpu.VMEM((l_pad, tq), F32),
                        pltpu.VMEM((l_pad, tq), jnp.int16), pltpu.VMEM((l_pad, tq), jnp.int16),
                        pltpu.VMEM((N_HEADS, l_pad, tq), F32),
                        pltpu.VMEM((HEAD_DIM, l_pad), BF16), pltpu.VMEM((HW, tq), F32)],
        compiler_params=_cparams(2),
        name="indexer_sparse_attn",
    )(q, qi, wi, kc, vc, ki, jnp.asarray(eye, BF16), jnp.asarray(tri, BF16))


def _mix_out_kernel(x_ref, oa_ref, ob_ref, oc_ref, ga_ref, g_ref, w_ref, o_ref):
    wa = HW
    wb = wa + N_HEADS * DV_B
    y = (_dot(oa_ref[...].astype(BF16), w_ref[0:wa, :])
         + _dot(ob_ref[...].astype(BF16), w_ref[wa:wb, :])
         + _dot(oc_ref[...].astype(BF16), w_ref[wb:wb + HW, :]))
    o_ref[...] = x_ref[...] + ga_ref[...] * _rms_rows(y, g_ref[...])


def _mix_out_call(x, oa, ob, oc, mod, gains, w_out, tm):
    rows, d = x.shape
    mixw = w_out.shape[0]
    row = lambda w: pl.BlockSpec((tm, w), lambda i: (i, 0))
    return pl.pallas_call(
        _mix_out_kernel,
        grid=(rows // tm,),
        in_specs=[row(d), row(HW), row(N_HEADS * DV_B), row(HW),
                  mod.spec(5, tm, d), _gain_spec(3, d),
                  pl.BlockSpec((mixw, d), lambda i: (0, 0))],
        out_specs=row(d),
        out_shape=jax.ShapeDtypeStruct((rows, d), F32),
        compiler_params=_cparams(1),
        name="mix_out_proj",
    )(x, oa, ob, oc, mod.arr, gains, w_out)


def _pad_rows(a, n):
    return jnp.pad(a, ((0, 0), (0, n - a.shape[1]), (0, 0)))


def _encoder_layer(x, n_seq, t, mod, past, w):
    rows, d = x.shape
    tm = _pick_tile(t, 512, SUBLANE) if not mod.per_token else rows
    x = _ffn_call(x, mod, 0, w["gains"], 0, w["w_f1_in"], w["w_f1_out"], tm)
    (sbq, sbk, sbv, gq, gk, gv, gr, la, dq, dqi, dkc, dvc, dki, dwi) = _proj_call(
        x, mod, w["gains"], w["w_in"], w["w_gate"], w["b_gate"], tm)
    seq = lambda a: a.reshape(n_seq, t, a.shape[-1])
    p_len = 0 if past is None else past["k_sb"].shape[1]
    l_real = p_len + t
    tk = 256 if l_real >= 256 else LANE
    l_pad = -(-l_real // tk) * tk
    cat = (lambda new, old: _pad_rows(seq(new) if old is None else jnp.concatenate([old, seq(new)], axis=1),
                                      l_pad))

    tq = _pick_tile(t, 256, SUBLANE)
    k_all = cat(sbk, None if past is None else past["k_sb"])
    v_all = cat(sbv, None if past is None else past["v_sb"])
    o_a = _sb_call(seq(sbq), k_all, v_all, p_len, tq, tk)

    c = CHUNK if t % CHUNK == 0 else t
    s0 = jnp.zeros((n_seq, HW, DV_B), F32) if past is None else past["gla"]
    o_b, s_new = _gla_call(seq(gq), seq(gk), seq(gv), seq(gr), seq(la), s0, w["g_gla"], c)

    t_c = t if t % LANE == 0 else -(-t // LANE) * LANE
    tq_c = _pick_tile(t_c, 256, LANE)
    qpad = lambda a: _pad_rows(seq(a), t_c)
    o_c = _dsa_call(qpad(dq), qpad(dqi), qpad(dwi),
                    cat(dkc, None if past is None else past["k_dsa"]),
                    cat(dvc, None if past is None else past["v_dsa"]),
                    cat(dki, None if past is None else past["k_idx"]),
                    p_len, l_real, tq_c, tk)[:, :t]

    flat = lambda a: a.reshape(rows, a.shape[-1])
    x = _mix_out_call(x, flat(o_a), flat(o_b), flat(o_c), mod, w["gains"], w["w_out"], tm)
    x = _ffn_call(x, mod, 6, w["gains"], 4, w["w_f2_in"], w["w_f2_out"], tm)
    new_rows = (seq(sbk).reshape(n_seq, t, N_HEADS, HEAD_DIM), seq(sbv).reshape(n_seq, t, N_HEADS, HEAD_DIM),
                seq(dkc), seq(dvc), seq(dki), s_new.reshape(n_seq, N_HEADS, HEAD_DIM, DV_B))
    return x, new_rows


def kernel(x_prompt, x_sample, cache_k_sb, cache_v_sb, cache_k_dsa, cache_v_dsa, cache_k_idx, state_gla,
           c_prompt, c_sample, w_ada, b_ada, norm_gains, w_ffn1_in, w_ffn1_out, w_ffn2_in, w_ffn2_out,
           w_in, w_gla_gate, b_gla_gate, gla_norm, w_out):
    bp, tp, d = x_prompt.shape
    bs, ts, _ = x_sample.shape
    depth = w_ada.shape[0]
    p_len = cache_k_sb.shape[2]

    mods = _mod_call(jnp.concatenate([c_prompt, c_sample], axis=0), w_ada.astype(BF16), b_ada)
    xp = x_prompt.reshape(bp * tp, d)
    xs = x_sample.reshape(bs * ts, d)
    tm_p = _pick_tile(tp, 512, SUBLANE)
    acc_p, acc_s = [], []
    for l in range(depth):
        w_gate = jnp.pad(w_gla_gate[l], ((0, LANE - GATE_RANK), (0, 0))).astype(BF16)
        w = dict(gains=norm_gains[l].reshape(-1, 1, d),
                 w_f1_in=w_ffn1_in[l].astype(BF16), w_f1_out=w_ffn1_out[l].astype(BF16),
                 w_f2_in=w_ffn2_in[l].astype(BF16), w_f2_out=w_ffn2_out[l].astype(BF16),
                 w_in=_relayout_w_in(w_in[l]).astype(BF16), w_gate=w_gate,
                 b_gate=b_gla_gate[l].reshape(1, HW), g_gla=gla_norm[l].reshape(1, N_HEADS * DV_B),
                 w_out=w_out[l].astype(BF16))
        mod_p = _Mod(mods[l, :bp].reshape(bp * N_MOD, 1, d), False, tp // tm_p)
        mod_s_rows = jnp.repeat(mods[l, bp:].reshape(bs, N_MOD, d), ts, axis=0)
        mod_s = _Mod(jnp.transpose(mod_s_rows, (1, 0, 2)), True)
        past = dict(k_sb=cache_k_sb[l].reshape(bs, p_len, HW), v_sb=cache_v_sb[l].reshape(bs, p_len, HW),
                    k_dsa=cache_k_dsa[l], v_dsa=cache_v_dsa[l], k_idx=cache_k_idx[l],
                    gla=state_gla[l].reshape(bs, HW, DV_B))
        xp, rows_p = _encoder_layer(xp, bp, tp, mod_p, None, w)
        xs, rows_s = _encoder_layer(xs, bs, ts, mod_s, past, w)
        acc_p.append(rows_p)
        acc_s.append(rows_s)
    field = lambda acc, i: jnp.stack([r[i] for r in acc], axis=0)
    return (xp.reshape(bp, tp, d), xs.reshape(bs, ts, d),
            *(field(acc_p, i) for i in range(6)), *(field(acc_s, i) for i in range(6)))
```

```python
import functools
import math

import numpy as np
import jax
import jax.numpy as jnp
from jax import lax
from jax.experimental import pallas as pl
from jax.experimental.pallas import tpu as pltpu

F32 = jnp.float32
BF16 = jnp.bfloat16

CHUNK = 64
N_HEADS = 4
HEAD_DIM = 64
HW = N_HEADS * HEAD_DIM
DV_B = 128
GATE_RANK = 16
GATE_TAU = 16.0
TOPK_MAX = 256
EPS = 1e-6
MACARON_W = 0.5
N_MOD = 9

LANE = 128
SUBLANE = 8
VMEM_LIMIT_BYTES = 56 * 1024 * 1024

_NEG_INF = float("-inf")
_KEY_NEG_INF = -2**31 + 0x7FFFFF
_INT_MIN = -2**31


def _pick_tile(n, target, mult):
    if n <= target:
        return n
    t = (target // mult) * mult
    while t >= mult:
        if n % t == 0:
            return t
        t -= mult
    return n


def _cparams(n_axes):
    return pltpu.CompilerParams(dimension_semantics=("arbitrary",) * n_axes,
                                vmem_limit_bytes=VMEM_LIMIT_BYTES)


def _rms_rows(x, g):
    ms = jnp.mean(x * x, axis=-1, keepdims=True)
    return x * lax.rsqrt(ms + EPS) * g


def _silu(x):
    return x * (1.0 / (1.0 + jnp.exp(-x)))


def _log_sigmoid(x):
    return jnp.minimum(x, 0.0) - jnp.log(1.0 + jnp.exp(-jnp.abs(x)))


_LOG2_E = 1.4426950408889634


def _neg_abs(x):
    bits = lax.bitcast_convert_type(x, jnp.int32) | _INT_MIN
    return lax.bitcast_convert_type(bits, F32)


def _dot(a, b):
    return jnp.dot(a, b, preferred_element_type=F32)


def _dot_nt(a, b):
    return lax.dot_general(a, b, (((1,), (1,)), ((), ())), preferred_element_type=F32)


def _dot_tn(a, b):
    return lax.dot_general(a, b, (((0,), (0,)), ((), ())), preferred_element_type=F32)


def _split_bf16(x):
    hi = x.astype(BF16)
    lo = (x - hi.astype(F32)).astype(BF16)
    return hi, lo


def _head_lane_id(width, per_head):
    return lax.broadcasted_iota(jnp.int32, (1, width), 1) // per_head


def _mod_kernel(c_ref, w_ref, b_ref, o_ref):
    a = _silu(c_ref[...]).astype(BF16)
    o_ref[...] = _dot(a, w_ref[...]) + b_ref[...]


def _mod_call(c, w_ada, b_ada):
    depth, d, nd = w_ada.shape
    n = c.shape[0]
    tn = _pick_tile(nd, 1536, LANE)
    return pl.pallas_call(
        _mod_kernel,
        grid=(depth, nd // tn),
        in_specs=[pl.BlockSpec((n, d), lambda l, j: (0, 0)),
                  pl.BlockSpec((None, d, tn), lambda l, j: (l, 0, j)),
                  pl.BlockSpec((None, 1, tn), lambda l, j: (l, 0, j))],
        out_specs=pl.BlockSpec((None, n, tn), lambda l, j: (l, 0, j)),
        out_shape=jax.ShapeDtypeStruct((depth, n, nd), F32),
        compiler_params=_cparams(2),
        name="adaln_mod",
    )(c, w_ada, b_ada.reshape(depth, 1, nd))


class _Mod:
    def __init__(self, arr, per_token, tiles_per_seq=None):
        self.arr, self.per_token, self.tiles_per_seq = arr, per_token, tiles_per_seq

    def spec(self, k, tm, d):
        if self.per_token:
            return pl.BlockSpec((None, tm, d), lambda i, *_: (k, i, 0))
        tps = self.tiles_per_seq
        return pl.BlockSpec((None, 1, d), lambda i, *_: ((i // tps) * N_MOD + k, 0, 0))


def _gain_spec(k, d):
    return pl.BlockSpec((None, 1, d), lambda i, *_: (k, 0, 0))


def _ffn_kernel(x_ref, sh_ref, sc_ref, ga_ref, gin_ref, gout_ref, wi_ref, wo_ref, o_ref, *, f, tf):
    x = x_ref[...]
    h = (_rms_rows(x, gin_ref[...]) * (1.0 + sc_ref[...]) + sh_ref[...]).astype(BF16)
    y = None
    for c0 in range(0, f, tf):
        gate = _dot(h, wi_ref[:, c0:c0 + tf])
        up = _dot(h, wi_ref[:, f + c0:f + c0 + tf])
        part = _dot((_silu(gate) * up).astype(BF16), wo_ref[c0:c0 + tf, :])
        y = part if y is None else y + part
    o_ref[...] = x + MACARON_W * ga_ref[...] * _rms_rows(y, gout_ref[...])


def _ffn_call(x, mod, mod_k, gains, gain_k, w_in, w_out, layer, tm):
    rows, d = x.shape
    f = w_out.shape[1]
    tf = _pick_tile(f, 1408, LANE)
    row_spec = pl.BlockSpec((tm, d), lambda i: (i, 0))
    resident = lambda shape: pl.BlockSpec((None,) + shape, lambda i: (layer, 0, 0),
                                          pipeline_mode=pl.Buffered(1))
    return pl.pallas_call(
        functools.partial(_ffn_kernel, f=f, tf=tf),
        grid=(rows // tm,),
        in_specs=[row_spec,
                  mod.spec(mod_k, tm, d), mod.spec(mod_k + 1, tm, d), mod.spec(mod_k + 2, tm, d),
                  _gain_spec(gain_k, d), _gain_spec(gain_k + 1, d),
                  resident((d, 2 * f)), resident((f, d))],
        out_specs=row_spec,
        out_shape=jax.ShapeDtypeStruct((rows, d), F32),
        compiler_params=_cparams(1),
        name="macaron_ffn",
    )(x, mod.arr, mod.arr, mod.arr, gains, gains, w_in, w_out)


_PROJ_GROUPS = (("sbq", HW), ("sbk", HW), ("sbv", HW),
                ("gq", HW), ("gk", HW), ("gv", N_HEADS * DV_B), ("gr", N_HEADS * DV_B), ("gb", LANE),
                ("dq", HW), ("dqi", HW), ("dkc", LANE), ("dvc", LANE), ("dkiw", LANE))
_PROJ_OFFSETS = tuple(int(v) for v in np.cumsum([0] + [w for _, w in _PROJ_GROUPS]))
_PROJ_WIDTH = _PROJ_OFFSETS[-1]
_PROJ_OUTPUTS = (("sbq", HW), ("sbk", HW), ("sbv", HW), ("gq", HW), ("gk", HW),
                 ("gv", N_HEADS * DV_B), ("gr", N_HEADS * DV_B), ("gb", HW),
                 ("dq", HW), ("dqi", HW), ("dkc", HEAD_DIM), ("dvc", HEAD_DIM),
                 ("dkiw", HEAD_DIM), ("dkiw", LANE))
_WI_LANE = HEAD_DIM
_PROJ_ROW_OUTPUTS = (("sbq", HW), ("gq", HW), ("gk", HW), ("gv", N_HEADS * DV_B), ("gr", N_HEADS * DV_B),
                     ("gb", HW), ("dq", HW), ("dqi", HW), ("dkc", HEAD_DIM), ("dkiw", HEAD_DIM),
                     ("dkiw", LANE))
_PROJ_T_OUTPUTS = (("sbk", HW), ("sbv", HW), ("dkc", HEAD_DIM), ("dvc", HEAD_DIM), ("dkiw", HEAD_DIM))


def _relayout_w_in(w_in):
    d = w_in.shape[0]
    widths = (HW, HW, HW, HW, HW, N_HEADS * DV_B, N_HEADS * DV_B, GATE_RANK,
              HW, HEAD_DIM, HEAD_DIM, HW, HEAD_DIM, N_HEADS)
    offs = np.cumsum((0,) + widths)
    col = lambda i: w_in[:, offs[i]:offs[i + 1]]
    pad = lambda a, w: jnp.pad(a, ((0, 0), (0, w - a.shape[1])))
    parts = [col(0), col(1), col(2), col(3), col(4), col(5), col(6), pad(col(7), LANE),
             col(8), col(11), pad(col(9), LANE), pad(col(10), LANE),
             pad(jnp.concatenate([col(12), col(13)], axis=1), LANE)]
    out = jnp.concatenate(parts, axis=1)
    assert out.shape == (d, _PROJ_WIDTH)
    return out


def _proj_kernel(x_ref, sh_ref, sc_ref, g_ref, w_ref, wgate_ref, bgate_ref, *refs, row_outs, t_outs, n_alias):
    row_refs = refs[n_alias:n_alias + len(row_outs)]
    t_refs = refs[n_alias + len(row_outs):]
    h = (_rms_rows(x_ref[...], g_ref[...]) * (1.0 + sc_ref[...]) + sh_ref[...]).astype(BF16)
    for gi, (name, _) in enumerate(_PROJ_GROUPS):
        p = _dot(h, w_ref[:, _PROJ_OFFSETS[gi]:_PROJ_OFFSETS[gi + 1]])
        if name == "gb":
            pre = _dot(p.astype(BF16), wgate_ref[...]) + bgate_ref[...]
            p = _log_sigmoid(pre) * (1.0 / GATE_TAU)
        for o_ref, (out_group, width) in zip(row_refs, row_outs):
            if out_group == name:
                o_ref[...] = p[:, :width]
        for o_ref, (out_group, width) in zip(t_refs, t_outs):
            if out_group == name:
                o_ref[...] = p.T[:width, :]


def _proj_call(x, mod, gains, w_in, w_gate, b_gate, tm, stacked=None):
    rows, d = x.shape
    in_specs = [pl.BlockSpec((tm, d), lambda i: (i, 0)),
                mod.spec(3, tm, d), mod.spec(4, tm, d), _gain_spec(2, d),
                pl.BlockSpec((d, _PROJ_WIDTH), lambda i: (0, 0)),
                pl.BlockSpec((LANE, HW), lambda i: (0, 0)),
                pl.BlockSpec((1, HW), lambda i: (0, 0))]
    args = [x, mod.arr, mod.arr, gains, w_in, w_gate, b_gate]
    if stacked is None:
        row_outs, t_outs, prev, aliases = _PROJ_OUTPUTS, (), (), {}
        t_specs, t_shapes = [], []
    else:
        layer, depth, n_seq, prev = stacked
        row_outs, t_outs = _PROJ_ROW_OUTPUTS, _PROJ_T_OUTPUTS
        t = rows // n_seq
        tps = t // tm
        t_specs = [pl.BlockSpec((None, None, w, tm), lambda i: (layer, i // tps, 0, i % tps)) for _, w in t_outs]
        t_shapes = [jax.ShapeDtypeStruct((depth, n_seq, w, t), F32) for _, w in t_outs]
        prev = () if prev is None else tuple(prev)
        in_specs += [pl.BlockSpec(memory_space=pl.ANY)] * len(prev)
        aliases = {len(args) + k: len(row_outs) + k for k in range(len(prev))}
    outs = pl.pallas_call(
        functools.partial(_proj_kernel, row_outs=row_outs, t_outs=t_outs, n_alias=len(prev)),
        grid=(rows // tm,),
        in_specs=in_specs,
        out_specs=[pl.BlockSpec((tm, w), lambda i: (i, 0)) for _, w in row_outs] + t_specs,
        out_shape=[jax.ShapeDtypeStruct((rows, w), F32) for _, w in row_outs] + t_shapes,
        input_output_aliases=aliases,
        compiler_params=_cparams(1),
        name="mix_in_proj",
    )(*args, *prev)
    return outs[:len(row_outs)], outs[len(row_outs):]


def _sb_kernel(q_ref, k_ref, v_ref, u_ref, o_ref, acc_scr, c_scr, *, tq, tk, past):
    i = pl.program_id(1)
    head = _head_lane_id(HW, HEAD_DIM)
    q = q_ref[...] * (HEAD_DIM ** -0.5)
    q_st = jnp.concatenate([jnp.where(head == h, q, 0.0) for h in range(N_HEADS)], axis=0).astype(BF16)
    qpos = past + i * tq + lax.broadcasted_iota(jnp.int32, (tq, 1), 0)
    last_blk = (past + (i + 1) * tq - 2) // tk
    acc_scr[...] = jnp.zeros_like(acc_scr)
    c_scr[...] = jnp.zeros_like(c_scr)
    u = u_ref[...]
    rows = N_HEADS * tq

    heads_per_tile = LANE // HEAD_DIM
    low_half = [(lax.broadcasted_iota(jnp.int32, (1, LANE), 1) // HEAD_DIM) == r
                for r in range(heads_per_tile)]

    def body(n, carry, *, masked):
        j = last_blk - n
        k0 = pl.multiple_of(j * tk, tk)
        kb = k_ref[:, pl.ds(k0, tk)].astype(BF16)
        vb = v_ref[:, pl.ds(k0, tk)].astype(BF16)
        z = _dot(q_st, kb).reshape(N_HEADS, tq, tk) * _LOG2_E
        soft = jnp.log2(1.0 + jnp.exp2(_neg_abs(z)))
        log_beta = jnp.minimum(z, 0.0) - soft
        log_keep = log_beta - z
        if masked:
            kpos = k0 + lax.broadcasted_iota(jnp.int32, (1, tk), 1)
            valid = (kpos < qpos)[None]
            log_keep = jnp.where(valid, log_keep, 0.0)
        hi, lo = _split_bf16(log_keep.reshape(rows, tk))
        both = _dot(jnp.concatenate([hi, lo], axis=0), u)
        newer = (both[:rows] + both[rows:]).reshape(N_HEADS, tq, tk)
        c = c_scr[...]
        w = jnp.exp2(log_beta + newer + c)
        if masked:
            w = jnp.where(valid, w, 0.0)
        pv = _dot_nt(w.astype(BF16).reshape(rows, tk), vb).reshape(N_HEADS, tq, HW)
        tiles = []
        for t0 in range(HW // LANE):
            sl = slice(t0 * LANE, (t0 + 1) * LANE)
            col = pv[t0 * heads_per_tile][:, sl]
            for r in range(1, heads_per_tile):
                col = jnp.where(low_half[r], pv[t0 * heads_per_tile + r][:, sl], col)
            tiles.append(col)
        acc_scr[...] += jnp.concatenate(tiles, axis=1)
        c_scr[...] = c + newer[:, :, 0:1] + log_keep[:, :, 0:1]
        return carry

    n_masked = last_blk + 1 - jnp.minimum((past + i * tq) // tk, last_blk + 1)
    lax.fori_loop(0, n_masked, functools.partial(body, masked=True), 0)
    lax.fori_loop(n_masked, last_blk + 1, functools.partial(body, masked=False), 0)
    o_ref[...] = acc_scr[...]


def _sb_call(q, k_t, v_t, layer, past, tq, tk):
    b, t, _ = q.shape
    l_pad = k_t.shape[3]
    u = (np.arange(tk)[:, None] > np.arange(tk)[None, :]).astype(np.float32)
    kv_spec = pl.BlockSpec((None, None, HW, l_pad), lambda bi, i: (layer, bi, 0, 0))
    return pl.pallas_call(
        functools.partial(_sb_kernel, tq=tq, tk=tk, past=past),
        grid=(b, t // tq),
        in_specs=[pl.BlockSpec((None, tq, HW), lambda bi, i: (bi, i, 0)), kv_spec, kv_spec,
                  pl.BlockSpec((tk, tk), lambda bi, i: (0, 0))],
        out_specs=pl.BlockSpec((None, tq, HW), lambda bi, i: (bi, i, 0)),
        out_shape=jax.ShapeDtypeStruct((b, t, HW), F32),
        scratch_shapes=[pltpu.VMEM((tq, HW), F32), pltpu.VMEM((N_HEADS, tq, 1), F32)],
        compiler_params=_cparams(2),
        name="stick_breaking_attn",
    )(q, k_t, v_t, jnp.asarray(u, BF16))


def _gla_tables(c):
    n_lev = int(math.log2(c))
    assert 1 << n_lev == c
    t = np.arange(c)
    rows = []
    for lev in range(1, n_lev + 1):
        blk = c >> (lev - 1)
        ref = (t // blk) * blk + blk // 2 - 1
        lo, hi = np.minimum(t, ref), np.maximum(t, ref)
        rows.append(((t[None, :] > lo[:, None]) & (t[None, :] <= hi[:, None])).astype(np.float32))
    rows.append((t[None, :] <= t[:, None]).astype(np.float32))
    rows.append((t[None, :] > t[:, None]).astype(np.float32))
    table = np.concatenate(rows, axis=0)
    lvl = np.full((c, c), n_lev + 1, np.int32)
    for lev in range(1, n_lev + 1):
        blk = c >> (lev - 1)
        half = blk // 2
        same = (t[:, None] // blk) == (t[None, :] // blk)
        pair = same & ((t[:, None] % blk) >= half) & ((t[None, :] % blk) < half)
        lvl[pair] = lev
    lvl[t, t] = 0
    return table, np.tile(lvl, (N_HEADS, 1)), n_lev


def _gla_kernel(q_ref, k_ref, v_ref, r_ref, la_ref, s0_ref, tab_ref, lvl_ref, g_ref,
                o_ref, s_ref, s_scr, *, c, n_lev, n_chunks, n_steps):
    j = pl.program_id(1)

    @pl.when(j == 0)
    def _():
        s_scr[...] = s0_ref[...]

    head = _head_lane_id(HW, HEAD_DIM)
    tab = tab_ref[...]
    lvl = lvl_ref[...]
    ones = jnp.ones((c, LANE), BF16)

    def chunk(ci, carry):
        r0 = pl.multiple_of(ci * c, c)
        rows = pl.ds(r0, c)
        q = q_ref[rows, :] * (HEAD_DIM ** -0.5)
        k = k_ref[rows, :]
        la_hi, la_lo = _split_bf16(la_ref[rows, :])
        decays = jnp.exp(_dot(tab, la_hi) + _dot(tab, la_lo))
        att = jnp.zeros((N_HEADS * c, c), F32)
        for lev in range(n_lev + 1):
            if lev == 0:
                ql, kl = q, k
            else:
                e = decays[(lev - 1) * c:lev * c]
                ql, kl = q * e, k * e
            lhs = jnp.concatenate([jnp.where(head == h, ql, 0.0) for h in range(N_HEADS)], axis=0)
            a = _dot_nt(lhs.astype(BF16), kl.astype(BF16))
            att = jnp.where(lvl == lev, a, att)
        q_in = q * decays[n_lev * c:(n_lev + 1) * c]
        k_out = (k * decays[(n_lev + 1) * c:(n_lev + 2) * c]).astype(BF16)
        s = s_scr[...]
        s_b = s.astype(BF16)
        ds_parts = []
        for h in range(N_HEADS):
            vh = v_ref[rows, h * DV_B:(h + 1) * DV_B].astype(BF16)
            o = (_dot(att[h * c:(h + 1) * c].astype(BF16), vh)
                 + _dot(jnp.where(head == h, q_in, 0.0).astype(BF16), s_b))
            y = _rms_rows(o, g_ref[:, h * DV_B:(h + 1) * DV_B])
            o_ref[rows, h * DV_B:(h + 1) * DV_B] = y * _silu(r_ref[rows, h * DV_B:(h + 1) * DV_B])
            ds_parts.append(_dot_tn(k_out, vh)[h * HEAD_DIM:(h + 1) * HEAD_DIM])
        total = _dot_tn(la_hi, ones) + _dot_tn(la_lo, ones)
        s_scr[...] = jnp.exp(total) * s + jnp.concatenate(ds_parts, axis=0)
        return carry

    lax.fori_loop(0, n_chunks, chunk, 0)

    @pl.when(j == n_steps - 1)
    def _():
        s_ref[...] = s_scr[...]


def _gla_call(q, k, v, r, la, s0, g_gla, c):
    b, t, _ = q.shape
    vw = N_HEADS * DV_B
    tc = _pick_tile(t, 512, c)
    n_steps = t // tc
    table, lvl, n_lev = _gla_tables(c)
    qk_spec = pl.BlockSpec((None, tc, HW), lambda bi, j: (bi, j, 0))
    vr_spec = pl.BlockSpec((None, tc, vw), lambda bi, j: (bi, j, 0))
    st_spec = pl.BlockSpec((None, HW, DV_B), lambda bi, j: (bi, 0, 0))
    const = lambda shape: pl.BlockSpec(shape, lambda bi, j: (0, 0))
    return pl.pallas_call(
        functools.partial(_gla_kernel, c=c, n_lev=n_lev, n_chunks=tc // c, n_steps=n_steps),
        grid=(b, n_steps),
        in_specs=[qk_spec, qk_spec, vr_spec, vr_spec, qk_spec, st_spec,
                  const(table.shape), const(lvl.shape), const((1, vw))],
        out_specs=[vr_spec, st_spec],
        out_shape=[jax.ShapeDtypeStruct((b, t, vw), F32), jax.ShapeDtypeStruct((b, HW, DV_B), F32)],
        scratch_shapes=[pltpu.VMEM((HW, DV_B), F32)],
        compiler_params=_cparams(2),
        name="gated_linear_attn",
    )(q, k, v, r, la, s0, jnp.asarray(table, BF16), jnp.asarray(lvl), g_gla)


_COUNT_ROWS = 32


def _key_to_f32(key):
    key = jnp.maximum(key, _KEY_NEG_INF)
    bits = jnp.where(key < 0, key ^ 0x7FFFFFFF, key)
    return lax.bitcast_convert_type(bits, F32)


def _dsa_kernel(q_ref, qi_ref, wi_ref, kc_ref, vc_ref, ki_ref, eye_ref, tri_ref, o_ref,
                sc_scr, hi_scr, lo_scr, lg_scr, vct_scr, acc_scr, *, tq, tk, past, l_real, n_sel):
    i = pl.program_id(1)
    scale = HEAD_DIM ** -0.5
    eye = eye_ref[...]
    l_pad = sc_scr.shape[0]

    @pl.when(i == 0)
    def _():
        vct_scr[...] = vc_ref[...].astype(BF16)

    q_t = _dot_nt(eye, (q_ref[...] * scale).astype(BF16)).astype(BF16)
    qi_t = _dot_nt(eye, (qi_ref[...] * scale).astype(BF16)).astype(BF16)
    w_t = wi_ref[...].T * (N_HEADS ** -0.5)
    qpos = past + i * tq + lax.broadcasted_iota(jnp.int32, (1, tq), 1)
    chunk_shift = int(math.log2(CHUNK))
    qchunk = lax.shift_right_logical(qpos, chunk_shift)
    k_end = ((past + (i + 1) * tq - 1) // CHUNK + 1) * CHUNK
    n_blk = jnp.minimum((k_end + tk - 1) // tk, l_pad // tk)

    def rows_of(j):
        return pl.ds(pl.multiple_of(j * tk, tk), tk)

    def admissible(j):
        kpos = j * tk + lax.broadcasted_iota(jnp.int32, (tk, 1), 0)
        return (lax.shift_right_logical(kpos, chunk_shift) <= qchunk) & (kpos < l_real)

    def score_blk(j, carry):
        kib = ki_ref[rows_of(j), :].astype(BF16)
        s = jnp.zeros((tk, tq), F32)
        for h in range(N_HEADS):
            sh = _dot(kib, qi_t[h * HEAD_DIM:(h + 1) * HEAD_DIM])
            s = s + w_t[_WI_LANE + h:_WI_LANE + h + 1] * jnp.maximum(sh, 0.0)
        s = jnp.where(admissible(j), jnp.where(s == 0.0, 0.0, s), _NEG_INF)
        sc_scr[rows_of(j), :] = s
        bits = lax.bitcast_convert_type(s, jnp.int32)
        key = jnp.where(bits < 0, bits ^ 0x7FFFFFFF, bits)
        hi_scr[rows_of(j), :] = lax.shift_right_arithmetic(key, 16).astype(jnp.int16)
        lo_scr[rows_of(j), :] = ((key & 0xFFFF) - 32768).astype(jnp.int16)
        return carry

    lax.fori_loop(0, n_blk, score_blk, 0)

    def count(ref, pred, one, zero):
        def blk(j, acc):
            m = jnp.where(pred(ref[rows_of(j), :]), one, zero).reshape(tk // _COUNT_ROWS, _COUNT_ROWS, tq)
            for r in range(tk // _COUNT_ROWS):
                acc = acc + m[r]
            return acc
        acc = lax.fori_loop(0, n_blk, blk, jnp.zeros((_COUNT_ROWS, tq), one.dtype))
        return jnp.sum(acc.astype(jnp.int32), axis=0, keepdims=True)

    one16, zero16 = jnp.int16(1), jnp.int16(0)

    def bisect16(ref, want):
        def bit(it, tau):
            cand = tau + lax.shift_left(jnp.int32(1), 15 - it)
            c16 = cand.astype(jnp.int16)
            return jnp.where(count(ref, lambda a: a >= c16, one16, zero16) >= want, cand, tau)
        return lax.fori_loop(0, 16, bit, jnp.full((1, tq), -32768, jnp.int32))

    hi_star = bisect16(hi_scr, n_sel)
    h16 = hi_star.astype(jnp.int16)
    want_lo = n_sel - count(hi_scr, lambda a: a > h16, one16, zero16)

    def mask_lo(j, carry):
        lo_scr[rows_of(j), :] = jnp.where(hi_scr[rows_of(j), :] == h16, lo_scr[rows_of(j), :],
                                          jnp.int16(-32768))
        return carry

    lax.fori_loop(0, n_blk, mask_lo, 0)
    lo_star = bisect16(lo_scr, want_lo)
    thr = _key_to_f32(lax.shift_left(hi_star, 16) + (lo_star + 32768))
    need = (n_sel - count(sc_scr, lambda a: a > thr, jnp.float32(1.0), jnp.float32(0.0))).astype(F32)

    tri = tri_ref[...]
    fold = lambda a, op: op(a.reshape(tk // SUBLANE, SUBLANE, tq), axis=0)
    parts = lambda v: tuple(jnp.full((SUBLANE, tq), v, F32) for _ in range(N_HEADS))

    def logit_blk(j, carry):
        m_parts, ties_before = carry
        s = sc_scr[rows_of(j), :]
        tie = s == thr
        rank = _dot(tri, jnp.where(tie, 1.0, 0.0).astype(BF16)) + ties_before
        sel = admissible(j) & ((s > thr) | (tie & (rank <= need)))
        bias = jnp.where(sel, 0.0, _NEG_INF)
        kcb = kc_ref[rows_of(j), :].astype(BF16)
        new_parts = []
        for h in range(N_HEADS):
            lg = _dot(kcb, q_t[h * HEAD_DIM:(h + 1) * HEAD_DIM]) + bias
            lg_scr[h, rows_of(j), :] = lg
            new_parts.append(jnp.maximum(m_parts[h], fold(lg, jnp.max)))
        return tuple(new_parts), rank[tk - 1:tk, :]

    m_parts, _ = lax.fori_loop(0, n_blk, logit_blk, (parts(_NEG_INF), jnp.zeros((1, tq), F32)))
    m_use = []
    for h in range(N_HEADS):
        m = jnp.max(m_parts[h], axis=0, keepdims=True)
        m_use.append(jnp.where(m == _NEG_INF, 0.0, m))
    acc_scr[...] = jnp.zeros_like(acc_scr)

    def pv_blk(j, l_parts):
        vct = vct_scr[:, rows_of(j)]
        new_parts = []
        for h in range(N_HEADS):
            hd = slice(h * HEAD_DIM, (h + 1) * HEAD_DIM)
            p = jnp.exp(lg_scr[h, rows_of(j), :] - m_use[h])
            new_parts.append(l_parts[h] + fold(p, jnp.sum))
            acc_scr[hd, :] += _dot(vct, p.astype(BF16))
        return tuple(new_parts)

    l_parts = lax.fori_loop(0, n_blk, pv_blk, parts(0.0))
    out_t = jnp.concatenate(
        [acc_scr[h * HEAD_DIM:(h + 1) * HEAD_DIM, :] / jnp.sum(l_parts[h], axis=0, keepdims=True)
         for h in range(N_HEADS)], axis=0)
    o_ref[...] = out_t.T


def _dsa_call(q, qi, wi, kc, vc_t, ki, layer, past, l_real, tq, tk):
    b, t, _ = q.shape
    l_pad = kc.shape[1]
    n_sel = min(TOPK_MAX, l_real // 4)
    assert tk >= n_sel and l_pad % tk == 0 and tk % _COUNT_ROWS == 0
    row = lambda w: pl.BlockSpec((None, tq, w), lambda bi, i: (bi, i, 0))
    full = pl.BlockSpec((None, l_pad, HEAD_DIM), lambda bi, i: (bi, 0, 0))
    full_t = pl.BlockSpec((None, None, HEAD_DIM, l_pad), lambda bi, i: (layer, bi, 0, 0))
    const = lambda n: pl.BlockSpec((n, n), lambda bi, i: (0, 0))
    eye = np.eye(HW, dtype=np.float32)
    tri = (np.arange(tk)[:, None] >= np.arange(tk)[None, :]).astype(np.float32)
    return pl.pallas_call(
        functools.partial(_dsa_kernel, tq=tq, tk=tk, past=past, l_real=l_real, n_sel=n_sel),
        grid=(b, t // tq),
        in_specs=[row(HW), row(HW), row(LANE), full, full_t, full, const(HW), const(tk)],
        out_specs=row(HW),
        out_shape=jax.ShapeDtypeStruct((b, t, HW), F32),
        scratch_shapes=[pltpu.VMEM((l_pad, tq), F32),
                        pltpu.VMEM((l_pad, tq), jnp.int16), pltpu.VMEM((l_pad, tq), jnp.int16),
                        pltpu.VMEM((N_HEADS, l_pad, tq), F32),
                        pltpu.VMEM((HEAD_DIM, l_pad), BF16), pltpu.VMEM((HW, tq), F32)],
        compiler_params=_cparams(2),
        name="indexer_sparse_attn",
    )(q, qi, wi, kc, vc_t, ki, jnp.asarray(eye, BF16), jnp.asarray(tri, BF16))


def _mix_out_kernel(x_ref, oa_ref, ob_ref, oc_ref, ga_ref, g_ref, w_ref, o_ref):
    wa = HW
    wb = wa + N_HEADS * DV_B
    y = (_dot(oa_ref[...].astype(BF16), w_ref[0:wa, :])
         + _dot(ob_ref[...].astype(BF16), w_ref[wa:wb, :])
         + _dot(oc_ref[...].astype(BF16), w_ref[wb:wb + HW, :]))
    o_ref[...] = x_ref[...] + ga_ref[...] * _rms_rows(y, g_ref[...])


def _mix_out_call(x, oa, ob, oc, mod, gains, w_out, tm):
    rows, d = x.shape
    mixw = w_out.shape[0]
    row = lambda w: pl.BlockSpec((tm, w), lambda i: (i, 0))
    return pl.pallas_call(
        _mix_out_kernel,
        grid=(rows // tm,),
        in_specs=[row(d), row(HW), row(N_HEADS * DV_B), row(HW),
                  mod.spec(5, tm, d), _gain_spec(3, d),
                  pl.BlockSpec((mixw, d), lambda i: (0, 0))],
        out_specs=row(d),
        out_shape=jax.ShapeDtypeStruct((rows, d), F32),
        compiler_params=_cparams(1),
        name="mix_out_proj",
    )(x, oa, ob, oc, mod.arr, gains, w_out)


def _pad_rows(a, n):
    return a if n == a.shape[1] else jnp.pad(a, ((0, 0), (0, n - a.shape[1]), (0, 0)))


def _pad_last(a, n):
    return a if n == a.shape[-1] else jnp.pad(a, [(0, 0)] * (a.ndim - 1) + [(0, n - a.shape[-1])])


def _encoder_layer(x, n_seq, t, mod, past, w, stacked=None):
    rows, d = x.shape
    tm = _pick_tile(t, 512, SUBLANE) if not mod.per_token else rows
    x = _ffn_call(x, mod, 0, w["gains"], 0, w["w_f1_in"], w["w_f1_out"], w["layer"], tm)
    seq = lambda a: a.reshape(n_seq, t, a.shape[-1])
    p_len = 0 if past is None else past["k_sb_t"].shape[2]
    l_real = p_len + t
    tk = 256 if l_real >= 256 else LANE
    l_pad = -(-l_real // tk) * tk
    proj = functools.partial(_proj_call, x, mod, w["gains"], w["w_in"], w["w_gate"], w["b_gate"], tm)
    if stacked is None:
        (sbq, sbk, sbv, gq, gk, gv, gr, la, dq, dqi, dkc, dvc, dki, dwi), _ = proj()
        join_t = lambda old_t, new: _pad_last(
            jnp.concatenate([old_t, jnp.swapaxes(seq(new), 1, 2)], axis=2), l_pad)[None]
        join_r = lambda old, new: _pad_rows(jnp.concatenate([old, seq(new)], axis=1), l_pad)
        k_t, v_t, vc_t = join_t(past["k_sb_t"], sbk), join_t(past["v_sb_t"], sbv), join_t(past["v_dsa_t"], dvc)
        kc_rows, ki_rows = join_r(past["k_dsa"], dkc), join_r(past["k_idx"], dki)
        layer = 0
        new_rows = (seq(sbk).reshape(n_seq, t, N_HEADS, HEAD_DIM), seq(sbv).reshape(n_seq, t, N_HEADS, HEAD_DIM),
                    seq(dkc), seq(dvc), seq(dki))
    else:
        assert past is None and l_pad == t and t % tm == 0 and tm % LANE == 0
        layer, depth, prev = stacked
        (sbq, gq, gk, gv, gr, la, dq, dqi, dkc, dki, dwi), new_rows = proj((layer, depth, n_seq, prev))
        k_t, v_t, _, vc_t, _ = new_rows
        kc_rows, ki_rows = seq(dkc), seq(dki)

    tq = _pick_tile(t, 256, SUBLANE)
    o_a = _sb_call(seq(sbq), k_t, v_t, layer, p_len, tq, tk)

    c = CHUNK if t % CHUNK == 0 else t
    s0 = jnp.zeros((n_seq, HW, DV_B), F32) if past is None else past["gla"]
    o_b, s_new = _gla_call(seq(gq), seq(gk), seq(gv), seq(gr), seq(la), s0, w["g_gla"], c)

    t_c = t if t % LANE == 0 else -(-t // LANE) * LANE
    tq_c = _pick_tile(t_c, 256, LANE)
    qpad = lambda a: _pad_rows(seq(a), t_c)
    o_c = _dsa_call(qpad(dq), qpad(dqi), qpad(dwi), kc_rows, vc_t, ki_rows, layer, p_len, l_real, tq_c, tk)
    o_c = o_c if t_c == t else o_c[:, :t]

    flat = lambda a: a.reshape(rows, a.shape[-1])
    x = _mix_out_call(x, flat(o_a), flat(o_b), flat(o_c), mod, w["gains"], w["w_out"], tm)
    x = _ffn_call(x, mod, 6, w["gains"], 4, w["w_f2_in"], w["w_f2_out"], w["layer"], tm)
    return x, new_rows, s_new.reshape(n_seq, N_HEADS, HEAD_DIM, DV_B)


def kernel(x_prompt, x_sample, cache_k_sb, cache_v_sb, cache_k_dsa, cache_v_dsa, cache_k_idx, state_gla,
           c_prompt, c_sample, w_ada, b_ada, norm_gains, w_ffn1_in, w_ffn1_out, w_ffn2_in, w_ffn2_out,
           w_in, w_gla_gate, b_gla_gate, gla_norm, w_out):
    bp, tp, d = x_prompt.shape
    bs, ts, _ = x_sample.shape
    depth = w_ada.shape[0]
    p_len = cache_k_sb.shape[2]

    mods = _mod_call(jnp.concatenate([c_prompt, c_sample], axis=0), w_ada.astype(BF16), b_ada)
    xp = x_prompt.reshape(bp * tp, d)
    xs = x_sample.reshape(bs * ts, d)
    tm_p = _pick_tile(tp, 512, SUBLANE)
    ffn_w = [a.astype(BF16) for a in (w_ffn1_in, w_ffn1_out, w_ffn2_in, w_ffn2_out)]
    shared_p, gla_p, acc_s = None, [], []
    for l in range(depth):
        w_gate = jnp.pad(w_gla_gate[l], ((0, LANE - GATE_RANK), (0, 0))).astype(BF16)
        w = dict(gains=norm_gains[l].reshape(-1, 1, d),
                 layer=l, w_f1_in=ffn_w[0], w_f1_out=ffn_w[1], w_f2_in=ffn_w[2], w_f2_out=ffn_w[3],
                 w_in=_relayout_w_in(w_in[l]).astype(BF16), w_gate=w_gate,
                 b_gate=b_gla_gate[l].reshape(1, HW), g_gla=gla_norm[l].reshape(1, N_HEADS * DV_B),
                 w_out=w_out[l].astype(BF16))
        mod_p = _Mod(mods[l, :bp].reshape(bp * N_MOD, 1, d), False, tp // tm_p)
        mod_s_rows = jnp.repeat(mods[l, bp:].reshape(bs, N_MOD, d), ts, axis=0)
        mod_s = _Mod(jnp.transpose(mod_s_rows, (1, 0, 2)), True)
        keys_last = lambda c: jnp.transpose(c, (0, 2, 3, 1)).reshape(bs, HW, p_len)
        past = dict(k_sb_t=keys_last(cache_k_sb[l]), v_sb_t=keys_last(cache_v_sb[l]),
                    k_dsa=cache_k_dsa[l], v_dsa_t=jnp.swapaxes(cache_v_dsa[l], 1, 2), k_idx=cache_k_idx[l],
                    gla=state_gla[l].reshape(bs, HW, DV_B))
        xp, shared_p, s_p = _encoder_layer(xp, bp, tp, mod_p, None, w, stacked=(l, depth, shared_p))
        xs, rows_s, s_s = _encoder_layer(xs, bs, ts, mod_s, past, w)
        gla_p.append(s_p)
        acc_s.append(rows_s + (s_s,))
    k_t, v_t, kc_t, vc_t, ki_t = shared_p
    heads_out = lambda a: jnp.transpose(a.reshape(depth, bp, N_HEADS, HEAD_DIM, tp), (0, 1, 4, 2, 3))
    tokens_out = lambda a: jnp.swapaxes(a, 2, 3)
    field = lambda acc, i: jnp.stack([r[i] for r in acc], axis=0)
    return (xp.reshape(bp, tp, d), xs.reshape(bs, ts, d),
            heads_out(k_t), heads_out(v_t), tokens_out(kc_t), tokens_out(vc_t), tokens_out(ki_t),
            jnp.stack(gla_p, axis=0), *(field(acc_s, i) for i in range(6)))
```

```python
import functools
import math

import numpy as np
import jax
import jax.numpy as jnp
from jax import lax
from jax.experimental import pallas as pl
from jax.experimental.pallas import tpu as pltpu

F32 = jnp.float32
BF16 = jnp.bfloat16

CHUNK = 64
N_HEADS = 4
HEAD_DIM = 64
HW = N_HEADS * HEAD_DIM
DV_B = 128
GATE_RANK = 16
GATE_TAU = 16.0
TOPK_MAX = 256
EPS = 1e-6
MACARON_W = 0.5
N_MOD = 9

LANE = 128
SUBLANE = 8
VMEM_LIMIT_BYTES = 56 * 1024 * 1024

_NEG_INF = float("-inf")
_KEY_NEG_INF = -2**31 + 0x7FFFFF
_INT_MIN = -2**31


def _pick_tile(n, target, mult):
    if n <= target:
        return n
    t = (target // mult) * mult
    while t >= mult:
        if n % t == 0:
            return t
        t -= mult
    return n


def _cparams(n_axes):
    return pltpu.CompilerParams(dimension_semantics=("arbitrary",) * n_axes,
                                vmem_limit_bytes=VMEM_LIMIT_BYTES)


def _rms_rows(x, g):
    ms = jnp.mean(x * x, axis=-1, keepdims=True)
    return x * lax.rsqrt(ms + EPS) * g


def _silu(x):
    return x * (1.0 / (1.0 + jnp.exp(-x)))


def _log_sigmoid(x):
    return jnp.minimum(x, 0.0) - jnp.log(1.0 + jnp.exp(-jnp.abs(x)))


_LOG2_E = 1.4426950408889634


def _neg_abs(x):
    bits = lax.bitcast_convert_type(x, jnp.int32) | _INT_MIN
    return lax.bitcast_convert_type(bits, F32)


def _dot(a, b):
    return jnp.dot(a, b, preferred_element_type=F32)


def _dot_nt(a, b):
    return lax.dot_general(a, b, (((1,), (1,)), ((), ())), preferred_element_type=F32)


def _dot_tn(a, b):
    return lax.dot_general(a, b, (((0,), (0,)), ((), ())), preferred_element_type=F32)


def _fori_by_two(lo, hi, body, init):
    n = hi - lo

    def two(p, carry):
        j = lo + 2 * p
        return body(j + 1, body(j, carry))

    carry = lax.fori_loop(0, lax.shift_right_logical(n, 1), two, init)
    return lax.cond((n & 1) == 1, lambda c: body(hi - 1, c), lambda c: c, carry)


def _split_bf16(x):
    hi = x.astype(BF16)
    lo = (x - hi.astype(F32)).astype(BF16)
    return hi, lo


def _head_lane_id(width, per_head):
    return lax.broadcasted_iota(jnp.int32, (1, width), 1) // per_head


def _mod_kernel(c_ref, w_ref, b_ref, o_ref):
    a = _silu(c_ref[...]).astype(BF16)
    o_ref[...] = _dot(a, w_ref[...]) + b_ref[...]


def _mod_call(c, w_ada, b_ada):
    depth, d, nd = w_ada.shape
    n = c.shape[0]
    tn = _pick_tile(nd, 1536, LANE)
    return pl.pallas_call(
        _mod_kernel,
        grid=(depth, nd // tn),
        in_specs=[pl.BlockSpec((n, d), lambda l, j: (0, 0)),
                  pl.BlockSpec((None, d, tn), lambda l, j: (l, 0, j)),
                  pl.BlockSpec((None, 1, tn), lambda l, j: (l, 0, j))],
        out_specs=pl.BlockSpec((None, n, tn), lambda l, j: (l, 0, j)),
        out_shape=jax.ShapeDtypeStruct((depth, n, nd), F32),
        compiler_params=_cparams(2),
        name="adaln_mod",
    )(c, w_ada, b_ada.reshape(depth, 1, nd))


class _Mod:
    def __init__(self, arr, per_token, tiles_per_seq=None):
        self.arr, self.per_token, self.tiles_per_seq = arr, per_token, tiles_per_seq

    def spec(self, k, tm, d):
        if self.per_token:
            return pl.BlockSpec((None, tm, d), lambda i, *_: (k, i, 0))
        tps = self.tiles_per_seq
        return pl.BlockSpec((None, 1, d), lambda i, *_: ((i // tps) * N_MOD + k, 0, 0))


def _gain_spec(k, d):
    return pl.BlockSpec((None, 1, d), lambda i, *_: (k, 0, 0))


def _ffn_kernel(x_ref, sh_ref, sc_ref, ga_ref, gin_ref, gout_ref, wi_ref, wo_ref, o_ref, *, f, tf):
    x = x_ref[...]
    h = (_rms_rows(x, gin_ref[...]) * (1.0 + sc_ref[...]) + sh_ref[...]).astype(BF16)
    y = None
    for c0 in range(0, f, tf):
        gate = _dot(h, wi_ref[:, c0:c0 + tf])
        up = _dot(h, wi_ref[:, f + c0:f + c0 + tf])
        part = _dot((_silu(gate) * up).astype(BF16), wo_ref[c0:c0 + tf, :])
        y = part if y is None else y + part
    o_ref[...] = x + MACARON_W * ga_ref[...] * _rms_rows(y, gout_ref[...])


def _ffn_call(x, mod, mod_k, gains, gain_k, w_in, w_out, layer, tm):
    rows, d = x.shape
    f = w_out.shape[1]
    tf = _pick_tile(f, 1408, LANE)
    row_spec = pl.BlockSpec((tm, d), lambda i: (i, 0))
    resident = lambda shape: pl.BlockSpec((None,) + shape, lambda i: (layer, 0, 0),
                                          pipeline_mode=pl.Buffered(1))
    return pl.pallas_call(
        functools.partial(_ffn_kernel, f=f, tf=tf),
        grid=(rows // tm,),
        in_specs=[row_spec,
                  mod.spec(mod_k, tm, d), mod.spec(mod_k + 1, tm, d), mod.spec(mod_k + 2, tm, d),
                  _gain_spec(gain_k, d), _gain_spec(gain_k + 1, d),
                  resident((d, 2 * f)), resident((f, d))],
        out_specs=row_spec,
        out_shape=jax.ShapeDtypeStruct((rows, d), F32),
        compiler_params=_cparams(1),
        name="macaron_ffn",
    )(x, mod.arr, mod.arr, mod.arr, gains, gains, w_in, w_out)


_PROJ_GROUPS = (("sbq", HW), ("sbk", HW), ("sbv", HW),
                ("gq", HW), ("gk", HW), ("gv", N_HEADS * DV_B), ("gr", N_HEADS * DV_B), ("gb", LANE),
                ("dq", HW), ("dqi", HW), ("dkc", LANE), ("dvc", LANE), ("dkiw", LANE))
_PROJ_OFFSETS = tuple(int(v) for v in np.cumsum([0] + [w for _, w in _PROJ_GROUPS]))
_PROJ_WIDTH = _PROJ_OFFSETS[-1]
_PROJ_OUTPUTS = (("sbq", HW), ("sbk", HW), ("sbv", HW), ("gq", HW), ("gk", HW),
                 ("gv", N_HEADS * DV_B), ("gr", N_HEADS * DV_B), ("gb", HW),
                 ("dq", HW), ("dqi", HW), ("dkc", HEAD_DIM), ("dvc", HEAD_DIM),
                 ("dkiw", HEAD_DIM), ("dkiw", LANE))
_WI_LANE = HEAD_DIM
_PROJ_ROW_OUTPUTS = (("sbq", HW), ("gq", HW), ("gk", HW), ("gv", N_HEADS * DV_B), ("gr", N_HEADS * DV_B),
                     ("gb", HW), ("dq", HW), ("dqi", HW), ("dkc", HEAD_DIM), ("dkiw", HEAD_DIM),
                     ("dkiw", LANE))
_PROJ_T_OUTPUTS = (("sbk", HW), ("sbv", HW), ("dkc", HEAD_DIM), ("dvc", HEAD_DIM), ("dkiw", HEAD_DIM))


def _relayout_w_in(w_in):
    d = w_in.shape[0]
    widths = (HW, HW, HW, HW, HW, N_HEADS * DV_B, N_HEADS * DV_B, GATE_RANK,
              HW, HEAD_DIM, HEAD_DIM, HW, HEAD_DIM, N_HEADS)
    offs = np.cumsum((0,) + widths)
    col = lambda i: w_in[:, offs[i]:offs[i + 1]]
    pad = lambda a, w: jnp.pad(a, ((0, 0), (0, w - a.shape[1])))
    parts = [col(0), col(1), col(2), col(3), col(4), col(5), col(6), pad(col(7), LANE),
             col(8), col(11), pad(col(9), LANE), pad(col(10), LANE),
             pad(jnp.concatenate([col(12), col(13)], axis=1), LANE)]
    out = jnp.concatenate(parts, axis=1)
    assert out.shape == (d, _PROJ_WIDTH)
    return out


def _proj_kernel(x_ref, sh_ref, sc_ref, g_ref, w_ref, wgate_ref, bgate_ref, *refs, row_outs, t_outs, n_alias):
    row_refs = refs[n_alias:n_alias + len(row_outs)]
    t_refs = refs[n_alias + len(row_outs):]
    h = (_rms_rows(x_ref[...], g_ref[...]) * (1.0 + sc_ref[...]) + sh_ref[...]).astype(BF16)
    for gi, (name, _) in enumerate(_PROJ_GROUPS):
        p = _dot(h, w_ref[:, _PROJ_OFFSETS[gi]:_PROJ_OFFSETS[gi + 1]])
        if name == "gb":
            pre = _dot(p.astype(BF16), wgate_ref[...]) + bgate_ref[...]
            p = _log_sigmoid(pre) * (1.0 / GATE_TAU)
        for o_ref, (out_group, width) in zip(row_refs, row_outs):
            if out_group == name:
                o_ref[...] = p[:, :width]
        for o_ref, (out_group, width) in zip(t_refs, t_outs):
            if out_group == name:
                o_ref[...] = p.T[:width, :]


def _proj_call(x, mod, gains, w_in, w_gate, b_gate, tm, stacked=None):
    rows, d = x.shape
    in_specs = [pl.BlockSpec((tm, d), lambda i: (i, 0)),
                mod.spec(3, tm, d), mod.spec(4, tm, d), _gain_spec(2, d),
                pl.BlockSpec((d, _PROJ_WIDTH), lambda i: (0, 0)),
                pl.BlockSpec((LANE, HW), lambda i: (0, 0)),
                pl.BlockSpec((1, HW), lambda i: (0, 0))]
    args = [x, mod.arr, mod.arr, gains, w_in, w_gate, b_gate]
    if stacked is None:
        row_outs, t_outs, prev, aliases = _PROJ_OUTPUTS, (), (), {}
        t_specs, t_shapes = [], []
    else:
        layer, depth, n_seq, prev = stacked
        row_outs, t_outs = _PROJ_ROW_OUTPUTS, _PROJ_T_OUTPUTS
        t = rows // n_seq
        tps = t // tm
        t_specs = [pl.BlockSpec((None, None, w, tm), lambda i: (layer, i // tps, 0, i % tps)) for _, w in t_outs]
        t_shapes = [jax.ShapeDtypeStruct((depth, n_seq, w, t), F32) for _, w in t_outs]
        prev = () if prev is None else tuple(prev)
        in_specs += [pl.BlockSpec(memory_space=pl.ANY)] * len(prev)
        aliases = {len(args) + k: len(row_outs) + k for k in range(len(prev))}
    outs = pl.pallas_call(
        functools.partial(_proj_kernel, row_outs=row_outs, t_outs=t_outs, n_alias=len(prev)),
        grid=(rows // tm,),
        in_specs=in_specs,
        out_specs=[pl.BlockSpec((tm, w), lambda i: (i, 0)) for _, w in row_outs] + t_specs,
        out_shape=[jax.ShapeDtypeStruct((rows, w), F32) for _, w in row_outs] + t_shapes,
        input_output_aliases=aliases,
        compiler_params=_cparams(1),
        name="mix_in_proj",
    )(*args, *prev)
    return outs[:len(row_outs)], outs[len(row_outs):]


def _sb_kernel(q_ref, k_ref, v_ref, u_ref, o_ref, acc_scr, c_scr, *, tq, tk, past):
    i = pl.program_id(1)
    head = _head_lane_id(HW, HEAD_DIM)
    q = q_ref[...] * (HEAD_DIM ** -0.5)
    q_st = jnp.concatenate([jnp.where(head == h, q, 0.0) for h in range(N_HEADS)], axis=0).astype(BF16)
    qpos = past + i * tq + lax.broadcasted_iota(jnp.int32, (tq, 1), 0)
    last_blk = (past + (i + 1) * tq - 2) // tk
    acc_scr[...] = jnp.zeros_like(acc_scr)
    c_scr[...] = jnp.zeros_like(c_scr)
    u = u_ref[...]
    rows = N_HEADS * tq

    heads_per_tile = LANE // HEAD_DIM
    low_half = [(lax.broadcasted_iota(jnp.int32, (1, LANE), 1) // HEAD_DIM) == r
                for r in range(heads_per_tile)]

    def body(n, carry, *, masked):
        j = last_blk - n
        k0 = pl.multiple_of(j * tk, tk)
        kb = k_ref[:, pl.ds(k0, tk)].astype(BF16)
        vb = v_ref[:, pl.ds(k0, tk)].astype(BF16)
        z = _dot(q_st, kb).reshape(N_HEADS, tq, tk) * _LOG2_E
        soft = jnp.log2(1.0 + jnp.exp2(_neg_abs(z)))
        log_beta = jnp.minimum(z, 0.0) - soft
        log_keep = log_beta - z
        if masked:
            kpos = k0 + lax.broadcasted_iota(jnp.int32, (1, tk), 1)
            valid = (kpos < qpos)[None]
            log_keep = jnp.where(valid, log_keep, 0.0)
        hi, lo = _split_bf16(log_keep.reshape(rows, tk))
        both = _dot(jnp.concatenate([hi, lo], axis=0), u)
        newer = (both[:rows] + both[rows:]).reshape(N_HEADS, tq, tk)
        c = c_scr[...]
        w = jnp.exp2(log_beta + newer + c)
        if masked:
            w = jnp.where(valid, w, 0.0)
        pv = _dot_nt(w.astype(BF16).reshape(rows, tk), vb).reshape(N_HEADS, tq, HW)
        tiles = []
        for t0 in range(HW // LANE):
            sl = slice(t0 * LANE, (t0 + 1) * LANE)
            col = pv[t0 * heads_per_tile][:, sl]
            for r in range(1, heads_per_tile):
                col = jnp.where(low_half[r], pv[t0 * heads_per_tile + r][:, sl], col)
            tiles.append(col)
        acc_scr[...] += jnp.concatenate(tiles, axis=1)
        c_scr[...] = c + newer[:, :, 0:1] + log_keep[:, :, 0:1]
        return carry

    n_masked = last_blk + 1 - jnp.minimum((past + i * tq) // tk, last_blk + 1)
    lax.fori_loop(0, n_masked, functools.partial(body, masked=True), 0)
    _fori_by_two(n_masked, last_blk + 1, functools.partial(body, masked=False), 0)
    o_ref[...] = acc_scr[...]


def _sb_call(q, k_t, v_t, layer, past, tq, tk):
    b, t, _ = q.shape
    l_pad = k_t.shape[3]
    u = (np.arange(tk)[:, None] > np.arange(tk)[None, :]).astype(np.float32)
    kv_spec = pl.BlockSpec((None, None, HW, l_pad), lambda bi, i: (layer, bi, 0, 0))
    return pl.pallas_call(
        functools.partial(_sb_kernel, tq=tq, tk=tk, past=past),
        grid=(b, t // tq),
        in_specs=[pl.BlockSpec((None, tq, HW), lambda bi, i: (bi, i, 0)), kv_spec, kv_spec,
                  pl.BlockSpec((tk, tk), lambda bi, i: (0, 0))],
        out_specs=pl.BlockSpec((None, tq, HW), lambda bi, i: (bi, i, 0)),
        out_shape=jax.ShapeDtypeStruct((b, t, HW), F32),
        scratch_shapes=[pltpu.VMEM((tq, HW), F32), pltpu.VMEM((N_HEADS, tq, 1), F32)],
        compiler_params=_cparams(2),
        name="stick_breaking_attn",
    )(q, k_t, v_t, jnp.asarray(u, BF16))


def _gla_tables(c):
    n_lev = int(math.log2(c))
    assert 1 << n_lev == c
    t = np.arange(c)
    rows = []
    for lev in range(1, n_lev + 1):
        blk = c >> (lev - 1)
        ref = (t // blk) * blk + blk // 2 - 1
        lo, hi = np.minimum(t, ref), np.maximum(t, ref)
        rows.append(((t[None, :] > lo[:, None]) & (t[None, :] <= hi[:, None])).astype(np.float32))
    rows.append((t[None, :] <= t[:, None]).astype(np.float32))
    rows.append((t[None, :] > t[:, None]).astype(np.float32))
    table = np.concatenate(rows, axis=0)
    lvl = np.full((c, c), n_lev + 1, np.int32)
    for lev in range(1, n_lev + 1):
        blk = c >> (lev - 1)
        half = blk // 2
        same = (t[:, None] // blk) == (t[None, :] // blk)
        pair = same & ((t[:, None] % blk) >= half) & ((t[None, :] % blk) < half)
        lvl[pair] = lev
    lvl[t, t] = 0
    return table, np.tile(lvl, (N_HEADS, 1)), n_lev


def _gla_kernel(q_seqs, k_seqs, v_seqs, r_seqs, la_seqs, s0_ref, tab_ref, lvl_ref, g_ref,
                o_seqs, s_ref, s_seqs, *, c, n_lev, n_chunks, n_steps):
    j = pl.program_id(1)

    @pl.when(j == 0)
    def _():
        s_seqs[...] = s0_ref[...]

    head = _head_lane_id(HW, HEAD_DIM)
    tab = tab_ref[...]
    lvl = lvl_ref[...]

    def chunk(ci, carry):
        for bi in range(q_seqs.shape[0]):
            seq_chunk(bi, pl.ds(pl.multiple_of(ci * c, c), c))
        return carry

    def seq_chunk(bi, rows):
        q_ref, k_ref, v_ref, r_ref, la_ref, o_ref, s_scr = (
            a.at[bi] for a in (q_seqs, k_seqs, v_seqs, r_seqs, la_seqs, o_seqs, s_seqs))
        q = q_ref[rows, :] * (HEAD_DIM ** -0.5)
        k = k_ref[rows, :]
        la_hi, la_lo = _split_bf16(la_ref[rows, :])
        decays = jnp.exp(_dot(tab, la_hi) + _dot(tab, la_lo))
        att = jnp.zeros((N_HEADS * c, c), F32)
        for lev in range(n_lev + 1):
            if lev == 0:
                ql, kl = q, k
            else:
                e = decays[(lev - 1) * c:lev * c]
                ql, kl = q * e, k * e
            lhs = jnp.concatenate([jnp.where(head == h, ql, 0.0) for h in range(N_HEADS)], axis=0)
            a = _dot_nt(lhs.astype(BF16), kl.astype(BF16))
            att = jnp.where(lvl == lev, a, att)
        q_decay = decays[n_lev * c:(n_lev + 1) * c]
        q_in = q * q_decay
        k_out = (k * decays[(n_lev + 1) * c:(n_lev + 2) * c]).astype(BF16)
        s = s_scr[...]
        s_b = s.astype(BF16)
        v_all = v_ref[rows, :].astype(BF16)
        for h in range(N_HEADS):
            vh = v_all[:, h * DV_B:(h + 1) * DV_B]
            o = (_dot(att[h * c:(h + 1) * c].astype(BF16), vh)
                 + _dot(jnp.where(head == h, q_in, 0.0).astype(BF16), s_b))
            y = _rms_rows(o, g_ref[:, h * DV_B:(h + 1) * DV_B])
            o_ref[rows, h * DV_B:(h + 1) * DV_B] = y * _silu(r_ref[rows, h * DV_B:(h + 1) * DV_B])
        kv = _dot_tn(k_out, v_all)
        ds = jnp.concatenate([kv[h * HEAD_DIM:(h + 1) * HEAD_DIM, h * DV_B:(h + 1) * DV_B]
                              for h in range(N_HEADS)], axis=0)
        chunk_decay = jnp.broadcast_to(q_decay[c - 1:c, :], (DV_B, HW)).T
        s_scr[...] = chunk_decay * s + ds

    lax.fori_loop(0, n_chunks, chunk, 0)

    @pl.when(j == n_steps - 1)
    def _():
        s_ref[...] = s_seqs[...]


def _gla_call(q, k, v, r, la, s0, g_gla, c):
    b, t, _ = q.shape
    vw = N_HEADS * DV_B
    tc = _pick_tile(t, 512, c)
    n_steps = t // tc
    nb = 2 if b % 2 == 0 else 1
    table, lvl, n_lev = _gla_tables(c)
    qk_spec = pl.BlockSpec((nb, tc, HW), lambda bi, j: (bi, j, 0))
    vr_spec = pl.BlockSpec((nb, tc, vw), lambda bi, j: (bi, j, 0))
    st_spec = pl.BlockSpec((nb, HW, DV_B), lambda bi, j: (bi, 0, 0))
    const = lambda shape: pl.BlockSpec(shape, lambda bi, j: (0, 0))
    return pl.pallas_call(
        functools.partial(_gla_kernel, c=c, n_lev=n_lev, n_chunks=tc // c, n_steps=n_steps),
        grid=(b // nb, n_steps),
        in_specs=[qk_spec, qk_spec, vr_spec, vr_spec, qk_spec, st_spec,
                  const(table.shape), const(lvl.shape), const((1, vw))],
        out_specs=[vr_spec, st_spec],
        out_shape=[jax.ShapeDtypeStruct((b, t, vw), F32), jax.ShapeDtypeStruct((b, HW, DV_B), F32)],
        scratch_shapes=[pltpu.VMEM((nb, HW, DV_B), F32)],
        compiler_params=_cparams(2),
        name="gated_linear_attn",
    )(q, k, v, r, la, s0, jnp.asarray(table, BF16), jnp.asarray(lvl), g_gla)


_COUNT_ROWS = 32


def _key_to_f32(key):
    key = jnp.maximum(key, _KEY_NEG_INF)
    bits = jnp.where(key < 0, key ^ 0x7FFFFFFF, key)
    return lax.bitcast_convert_type(bits, F32)


def _dsa_kernel(q_ref, qi_ref, wi_ref, kc_ref, vc_ref, ki_ref, eye_ref, tri_ref, o_ref,
                sc_scr, hi_scr, lo_scr, lg_scr, vct_scr, acc_scr, *, tq, tk, past, l_real, n_sel):
    i = pl.program_id(1)
    scale = HEAD_DIM ** -0.5
    eye = eye_ref[...]
    l_pad = sc_scr.shape[0]

    @pl.when(i == 0)
    def _():
        vct_scr[...] = vc_ref[...].astype(BF16)

    q_t = _dot_nt(eye, (q_ref[...] * scale).astype(BF16)).astype(BF16)
    qi_t = _dot_nt(eye, (qi_ref[...] * scale).astype(BF16)).astype(BF16)
    w_t = wi_ref[...].T * (N_HEADS ** -0.5)
    qpos = past + i * tq + lax.broadcasted_iota(jnp.int32, (1, tq), 1)
    chunk_shift = int(math.log2(CHUNK))
    qchunk = lax.shift_right_logical(qpos, chunk_shift)
    k_end = ((past + (i + 1) * tq - 1) // CHUNK + 1) * CHUNK
    n_blk = jnp.minimum((k_end + tk - 1) // tk, l_pad // tk)

    def rows_of(j):
        return pl.ds(pl.multiple_of(j * tk, tk), tk)

    def admissible(j):
        kpos = j * tk + lax.broadcasted_iota(jnp.int32, (tk, 1), 0)
        return (lax.shift_right_logical(kpos, chunk_shift) <= qchunk) & (kpos < l_real)

    def score_blk(j, carry):
        kib = ki_ref[rows_of(j), :].astype(BF16)
        s = jnp.zeros((tk, tq), F32)
        for h in range(N_HEADS):
            sh = _dot(kib, qi_t[h * HEAD_DIM:(h + 1) * HEAD_DIM])
            s = s + w_t[_WI_LANE + h:_WI_LANE + h + 1] * jnp.maximum(sh, 0.0)
        s = jnp.where(admissible(j), jnp.where(s == 0.0, 0.0, s), _NEG_INF)
        sc_scr[rows_of(j), :] = s
        bits = lax.bitcast_convert_type(s, jnp.int32)
        key = jnp.where(bits < 0, bits ^ 0x7FFFFFFF, bits)
        hi_scr[rows_of(j), :] = lax.shift_right_arithmetic(key, 16).astype(jnp.int16)
        lo_scr[rows_of(j), :] = ((key & 0xFFFF) - 32768).astype(jnp.int16)
        return carry

    _fori_by_two(0, n_blk, score_blk, 0)

    def count(ref, pred, one, zero):
        def blk(j, acc):
            m = jnp.where(pred(ref[rows_of(j), :]), one, zero).reshape(tk // _COUNT_ROWS, _COUNT_ROWS, tq)
            for r in range(tk // _COUNT_ROWS):
                acc = acc + m[r]
            return acc
        acc = _fori_by_two(0, n_blk, blk, jnp.zeros((_COUNT_ROWS, tq), one.dtype))
        return jnp.sum(acc.astype(jnp.int32), axis=0, keepdims=True)

    one16, zero16 = jnp.int16(1), jnp.int16(0)

    def bisect16(ref, want):
        def bit(it, tau):
            cand = tau + lax.shift_left(jnp.int32(1), 15 - it)
            c16 = cand.astype(jnp.int16)
            return jnp.where(count(ref, lambda a: a >= c16, one16, zero16) >= want, cand, tau)
        return lax.fori_loop(0, 16, bit, jnp.full((1, tq), -32768, jnp.int32))

    hi_star = bisect16(hi_scr, n_sel)
    h16 = hi_star.astype(jnp.int16)
    want_lo = n_sel - count(hi_scr, lambda a: a > h16, one16, zero16)

    def mask_lo(j, carry):
        lo_scr[rows_of(j), :] = jnp.where(hi_scr[rows_of(j), :] == h16, lo_scr[rows_of(j), :],
                                          jnp.int16(-32768))
        return carry

    lax.fori_loop(0, n_blk, mask_lo, 0)
    lo_star = bisect16(lo_scr, want_lo)
    thr = _key_to_f32(lax.shift_left(hi_star, 16) + (lo_star + 32768))
    need = (n_sel - count(sc_scr, lambda a: a > thr, jnp.float32(1.0), jnp.float32(0.0))).astype(F32)

    tri = tri_ref[...]
    fold = lambda a, op: op(a.reshape(tk // SUBLANE, SUBLANE, tq), axis=0)
    parts = lambda v: tuple(jnp.full((SUBLANE, tq), v, F32) for _ in range(N_HEADS))

    def logit_blk(j, carry):
        m_parts, ties_before = carry
        s = sc_scr[rows_of(j), :]
        tie = s == thr
        rank = _dot(tri, jnp.where(tie, 1.0, 0.0).astype(BF16)) + ties_before
        sel = admissible(j) & ((s > thr) | (tie & (rank <= need)))
        bias = jnp.where(sel, 0.0, _NEG_INF)
        kcb = kc_ref[rows_of(j), :].astype(BF16)
        new_parts = []
        for h in range(N_HEADS):
            lg = _dot(kcb, q_t[h * HEAD_DIM:(h + 1) * HEAD_DIM]) + bias
            lg_scr[h, rows_of(j), :] = lg
            new_parts.append(jnp.maximum(m_parts[h], fold(lg, jnp.max)))
        return tuple(new_parts), rank[tk - 1:tk, :]

    m_parts, _ = _fori_by_two(0, n_blk, logit_blk, (parts(_NEG_INF), jnp.zeros((1, tq), F32)))
    m_use = []
    for h in range(N_HEADS):
        m = jnp.max(m_parts[h], axis=0, keepdims=True)
        m_use.append(jnp.where(m == _NEG_INF, 0.0, m))
    acc_scr[...] = jnp.zeros_like(acc_scr)

    def pv_blk(j, l_parts):
        vct = vct_scr[:, rows_of(j)]
        new_parts = []
        for h in range(N_HEADS):
            hd = slice(h * HEAD_DIM, (h + 1) * HEAD_DIM)
            p = jnp.exp(lg_scr[h, rows_of(j), :] - m_use[h])
            new_parts.append(l_parts[h] + fold(p, jnp.sum))
            acc_scr[hd, :] += _dot(vct, p.astype(BF16))
        return tuple(new_parts)

    l_parts = _fori_by_two(0, n_blk, pv_blk, parts(0.0))
    out_t = jnp.concatenate(
        [acc_scr[h * HEAD_DIM:(h + 1) * HEAD_DIM, :] / jnp.sum(l_parts[h], axis=0, keepdims=True)
         for h in range(N_HEADS)], axis=0)
    o_ref[...] = out_t.T


def _dsa_call(q, qi, wi, kc, vc_t, ki, layer, past, l_real, tq, tk):
    b, t, _ = q.shape
    l_pad = kc.shape[1]
    n_sel = min(TOPK_MAX, l_real // 4)
    assert tk >= n_sel and l_pad % tk == 0 and tk % _COUNT_ROWS == 0
    row = lambda w: pl.BlockSpec((None, tq, w), lambda bi, i: (bi, i, 0))
    full = pl.BlockSpec((None, l_pad, HEAD_DIM), lambda bi, i: (bi, 0, 0))
    full_t = pl.BlockSpec((None, None, HEAD_DIM, l_pad), lambda bi, i: (layer, bi, 0, 0))
    const = lambda n: pl.BlockSpec((n, n), lambda bi, i: (0, 0))
    eye = np.eye(HW, dtype=np.float32)
    tri = (np.arange(tk)[:, None] >= np.arange(tk)[None, :]).astype(np.float32)
    return pl.pallas_call(
        functools.partial(_dsa_kernel, tq=tq, tk=tk, past=past, l_real=l_real, n_sel=n_sel),
        grid=(b, t // tq),
        in_specs=[row(HW), row(HW), row(LANE), full, full_t, full, const(HW), const(tk)],
        out_specs=row(HW),
        out_shape=jax.ShapeDtypeStruct((b, t, HW), F32),
        scratch_shapes=[pltpu.VMEM((l_pad, tq), F32),
                        pltpu.VMEM((l_pad, tq), jnp.int16), pltpu.VMEM((l_pad, tq), jnp.int16),
                        pltpu.VMEM((N_HEADS, l_pad, tq), F32),
                        pltpu.VMEM((HEAD_DIM, l_pad), BF16), pltpu.VMEM((HW, tq), F32)],
        compiler_params=_cparams(2),
        name="indexer_sparse_attn",
    )(q, qi, wi, kc, vc_t, ki, jnp.asarray(eye, BF16), jnp.asarray(tri, BF16))


def _mix_out_kernel(x_ref, oa_ref, ob_ref, oc_ref, ga_ref, g_ref, w_ref, o_ref):
    wa = HW
    wb = wa + N_HEADS * DV_B
    y = (_dot(oa_ref[...].astype(BF16), w_ref[0:wa, :])
         + _dot(ob_ref[...].astype(BF16), w_ref[wa:wb, :])
         + _dot(oc_ref[...].astype(BF16), w_ref[wb:wb + HW, :]))
    o_ref[...] = x_ref[...] + ga_ref[...] * _rms_rows(y, g_ref[...])


def _mix_out_call(x, oa, ob, oc, mod, gains, w_out, tm):
    rows, d = x.shape
    mixw = w_out.shape[0]
    row = lambda w: pl.BlockSpec((tm, w), lambda i: (i, 0))
    return pl.pallas_call(
        _mix_out_kernel,
        grid=(rows // tm,),
        in_specs=[row(d), row(HW), row(N_HEADS * DV_B), row(HW),
                  mod.spec(5, tm, d), _gain_spec(3, d),
                  pl.BlockSpec((mixw, d), lambda i: (0, 0))],
        out_specs=row(d),
        out_shape=jax.ShapeDtypeStruct((rows, d), F32),
        compiler_params=_cparams(1),
        name="mix_out_proj",
    )(x, oa, ob, oc, mod.arr, gains, w_out)


def _pad_rows(a, n):
    return a if n == a.shape[1] else jnp.pad(a, ((0, 0), (0, n - a.shape[1]), (0, 0)))


def _pad_last(a, n):
    return a if n == a.shape[-1] else jnp.pad(a, [(0, 0)] * (a.ndim - 1) + [(0, n - a.shape[-1])])


def _encoder_layer(x, n_seq, t, mod, past, w, stacked=None):
    rows, d = x.shape
    tm = _pick_tile(t, 512, SUBLANE) if not mod.per_token else rows
    x = _ffn_call(x, mod, 0, w["gains"], 0, w["w_f1_in"], w["w_f1_out"], w["layer"], tm)
    seq = lambda a: a.reshape(n_seq, t, a.shape[-1])
    p_len = 0 if past is None else past["k_sb_t"].shape[2]
    l_real = p_len + t
    tk = 256 if l_real >= 256 else LANE
    l_pad = -(-l_real // tk) * tk
    proj = functools.partial(_proj_call, x, mod, w["gains"], w["w_in"], w["w_gate"], w["b_gate"], tm)
    if stacked is None:
        (sbq, sbk, sbv, gq, gk, gv, gr, la, dq, dqi, dkc, dvc, dki, dwi), _ = proj()
        join_t = lambda old_t, new: _pad_last(
            jnp.concatenate([old_t, jnp.swapaxes(seq(new), 1, 2)], axis=2), l_pad)[None]
        join_r = lambda old, new: _pad_rows(jnp.concatenate([old, seq(new)], axis=1), l_pad)
        k_t, v_t, vc_t = join_t(past["k_sb_t"], sbk), join_t(past["v_sb_t"], sbv), join_t(past["v_dsa_t"], dvc)
        kc_rows, ki_rows = join_r(past["k_dsa"], dkc), join_r(past["k_idx"], dki)
        layer = 0
        new_rows = (seq(sbk).reshape(n_seq, t, N_HEADS, HEAD_DIM), seq(sbv).reshape(n_seq, t, N_HEADS, HEAD_DIM),
                    seq(dkc), seq(dvc), seq(dki))
    else:
        assert past is None and l_pad == t and t % tm == 0 and tm % LANE == 0
        layer, depth, prev = stacked
        (sbq, gq, gk, gv, gr, la, dq, dqi, dkc, dki, dwi), new_rows = proj((layer, depth, n_seq, prev))
        k_t, v_t, _, vc_t, _ = new_rows
        kc_rows, ki_rows = seq(dkc), seq(dki)

    tq = _pick_tile(t, 256, SUBLANE)
    o_a = _sb_call(seq(sbq), k_t, v_t, layer, p_len, tq, tk)

    c = CHUNK if t % CHUNK == 0 else t
    s0 = jnp.zeros((n_seq, HW, DV_B), F32) if past is None else past["gla"]
    o_b, s_new = _gla_call(seq(gq), seq(gk), seq(gv), seq(gr), seq(la), s0, w["g_gla"], c)

    t_c = t if t % LANE == 0 else -(-t // LANE) * LANE
    tq_c = _pick_tile(t_c, 256, LANE)
    qpad = lambda a: _pad_rows(seq(a), t_c)
    o_c = _dsa_call(qpad(dq), qpad(dqi), qpad(dwi), kc_rows, vc_t, ki_rows, layer, p_len, l_real, tq_c, tk)
    o_c = o_c if t_c == t else o_c[:, :t]

    flat = lambda a: a.reshape(rows, a.shape[-1])
    x = _mix_out_call(x, flat(o_a), flat(o_b), flat(o_c), mod, w["gains"], w["w_out"], tm)
    x = _ffn_call(x, mod, 6, w["gains"], 4, w["w_f2_in"], w["w_f2_out"], w["layer"], tm)
    return x, new_rows, s_new.reshape(n_seq, N_HEADS, HEAD_DIM, DV_B)


def kernel(x_prompt, x_sample, cache_k_sb, cache_v_sb, cache_k_dsa, cache_v_dsa, cache_k_idx, state_gla,
           c_prompt, c_sample, w_ada, b_ada, norm_gains, w_ffn1_in, w_ffn1_out, w_ffn2_in, w_ffn2_out,
           w_in, w_gla_gate, b_gla_gate, gla_norm, w_out):
    bp, tp, d = x_prompt.shape
    bs, ts, _ = x_sample.shape
    depth = w_ada.shape[0]
    p_len = cache_k_sb.shape[2]

    mods = _mod_call(jnp.concatenate([c_prompt, c_sample], axis=0), w_ada.astype(BF16), b_ada)
    xp = x_prompt.reshape(bp * tp, d)
    xs = x_sample.reshape(bs * ts, d)
    tm_p = _pick_tile(tp, 512, SUBLANE)
    ffn_w = [a.astype(BF16) for a in (w_ffn1_in, w_ffn1_out, w_ffn2_in, w_ffn2_out)]
    shared_p, gla_p, acc_s = None, [], []
    for l in range(depth):
        w_gate = jnp.pad(w_gla_gate[l], ((0, LANE - GATE_RANK), (0, 0))).astype(BF16)
        w = dict(gains=norm_gains[l].reshape(-1, 1, d),
                 layer=l, w_f1_in=ffn_w[0], w_f1_out=ffn_w[1], w_f2_in=ffn_w[2], w_f2_out=ffn_w[3],
                 w_in=_relayout_w_in(w_in[l]).astype(BF16), w_gate=w_gate,
                 b_gate=b_gla_gate[l].reshape(1, HW), g_gla=gla_norm[l].reshape(1, N_HEADS * DV_B),
                 w_out=w_out[l].astype(BF16))
        mod_p = _Mod(mods[l, :bp].reshape(bp * N_MOD, 1, d), False, tp // tm_p)
        mod_s_rows = jnp.repeat(mods[l, bp:].reshape(bs, N_MOD, d), ts, axis=0)
        mod_s = _Mod(jnp.transpose(mod_s_rows, (1, 0, 2)), True)
        keys_last = lambda c: jnp.transpose(c, (0, 2, 3, 1)).reshape(bs, HW, p_len)
        past = dict(k_sb_t=keys_last(cache_k_sb[l]), v_sb_t=keys_last(cache_v_sb[l]),
                    k_dsa=cache_k_dsa[l], v_dsa_t=jnp.swapaxes(cache_v_dsa[l], 1, 2), k_idx=cache_k_idx[l],
                    gla=state_gla[l].reshape(bs, HW, DV_B))
        xp, shared_p, s_p = _encoder_layer(xp, bp, tp, mod_p, None, w, stacked=(l, depth, shared_p))
        xs, rows_s, s_s = _encoder_layer(xs, bs, ts, mod_s, past, w)
        gla_p.append(s_p)
        acc_s.append(rows_s + (s_s,))
    k_t, v_t, kc_t, vc_t, ki_t = shared_p
    heads_out = lambda a: jnp.transpose(a.reshape(depth, bp, N_HEADS, HEAD_DIM, tp), (0, 1, 4, 2, 3))
    tokens_out = lambda a: jnp.swapaxes(a, 2, 3)
    field = lambda acc, i: jnp.stack([r[i] for r in acc], axis=0)
    return (xp.reshape(bp, tp, d), xs.reshape(bs, ts, d),
            heads_out(k_t), heads_out(v_t), tokens_out(kc_t), tokens_out(vc_t), tokens_out(ki_t),
            jnp.stack(gla_p, axis=0), *(field(acc_s, i) for i in range(6)))
```

```python
import functools
import math

import numpy as np
import jax
import jax.numpy as jnp
from jax import lax
from jax.experimental import pallas as pl
from jax.experimental.pallas import tpu as pltpu

F32 = jnp.float32
BF16 = jnp.bfloat16

CHUNK = 64
N_HEADS = 4
HEAD_DIM = 64
HW = N_HEADS * HEAD_DIM
DV_B = 128
GATE_RANK = 16
GATE_TAU = 16.0
TOPK_MAX = 256
EPS = 1e-6
MACARON_W = 0.5
N_MOD = 9

LANE = 128
SUBLANE = 8
VMEM_LIMIT_BYTES = 56 * 1024 * 1024

_NEG_INF = float("-inf")
_KEY_NEG_INF = -2**31 + 0x7FFFFF
_INT_MIN = -2**31


def _pick_tile(n, target, mult):
    if n <= target:
        return n
    t = (target // mult) * mult
    while t >= mult:
        if n % t == 0:
            return t
        t -= mult
    return n


def _cparams(n_axes):
    return pltpu.CompilerParams(dimension_semantics=("arbitrary",) * n_axes,
                                vmem_limit_bytes=VMEM_LIMIT_BYTES)


def _rms_rows(x, g):
    ms = jnp.mean(x * x, axis=-1, keepdims=True)
    return x * lax.rsqrt(ms + EPS) * g


def _silu(x):
    return x * (1.0 / (1.0 + jnp.exp(-x)))


def _log_sigmoid(x):
    return jnp.minimum(x, 0.0) - jnp.log(1.0 + jnp.exp(-jnp.abs(x)))


_LOG2_E = 1.4426950408889634


def _neg_abs(x):
    bits = lax.bitcast_convert_type(x, jnp.int32) | _INT_MIN
    return lax.bitcast_convert_type(bits, F32)


def _dot(a, b):
    return jnp.dot(a, b, preferred_element_type=F32)


def _dot_nt(a, b):
    return lax.dot_general(a, b, (((1,), (1,)), ((), ())), preferred_element_type=F32)


def _dot_tn(a, b):
    return lax.dot_general(a, b, (((0,), (0,)), ((), ())), preferred_element_type=F32)


def _fori_by_two(lo, hi, body, init):
    n = hi - lo

    def two(p, carry):
        j = lo + 2 * p
        return body(j + 1, body(j, carry))

    carry = lax.fori_loop(0, lax.shift_right_logical(n, 1), two, init)
    return lax.cond((n & 1) == 1, lambda c: body(hi - 1, c), lambda c: c, carry)


def _split_bf16(x):
    hi = x.astype(BF16)
    lo = (x - hi.astype(F32)).astype(BF16)
    return hi, lo


def _head_lane_id(width, per_head):
    return lax.broadcasted_iota(jnp.int32, (1, width), 1) // per_head


def _mod_kernel(c_ref, w_ref, b_ref, o_ref):
    a = _silu(c_ref[...]).astype(BF16)
    o_ref[...] = _dot(a, w_ref[...]) + b_ref[...]


def _mod_call(c, w_ada, b_ada):
    depth, d, nd = w_ada.shape
    n = c.shape[0]
    tn = _pick_tile(nd, 1536, LANE)
    return pl.pallas_call(
        _mod_kernel,
        grid=(depth, nd // tn),
        in_specs=[pl.BlockSpec((n, d), lambda l, j: (0, 0)),
                  pl.BlockSpec((None, d, tn), lambda l, j: (l, 0, j)),
                  pl.BlockSpec((None, 1, tn), lambda l, j: (l, 0, j))],
        out_specs=pl.BlockSpec((None, n, tn), lambda l, j: (l, 0, j)),
        out_shape=jax.ShapeDtypeStruct((depth, n, nd), F32),
        compiler_params=_cparams(2),
        name="adaln_mod",
    )(c, w_ada, b_ada.reshape(depth, 1, nd))


class _Mod:
    def __init__(self, arr, per_token, tiles_per_seq=None):
        self.arr, self.per_token, self.tiles_per_seq = arr, per_token, tiles_per_seq

    def spec(self, k, tm, d):
        if self.per_token:
            return pl.BlockSpec((None, tm, d), lambda i, *_: (k, i, 0))
        tps = self.tiles_per_seq
        return pl.BlockSpec((None, 1, d), lambda i, *_: ((i // tps) * N_MOD + k, 0, 0))


def _gain_spec(k, d):
    return pl.BlockSpec((None, 1, d), lambda i, *_: (k, 0, 0))


def _mix_residual(x, oa_ref, ob_ref, oc_ref, ga_ref, g_ref, w_ref):
    wa = HW
    wb = wa + N_HEADS * DV_B
    y = (_dot(oa_ref[...].astype(BF16), w_ref[0:wa, :])
         + _dot(ob_ref[...].astype(BF16), w_ref[wa:wb, :])
         + _dot(oc_ref[...].astype(BF16), w_ref[wb:wb + HW, :]))
    return x + ga_ref[...] * _rms_rows(y, g_ref[...])


def _ffn_kernel(x_ref, *refs, f, tf, with_mix):
    x = x_ref[...]
    if with_mix:
        x = _mix_residual(x, *refs[:6])
        refs = refs[6:]
    sh_ref, sc_ref, ga_ref, gin_ref, gout_ref, wi_ref, wo_ref, o_ref = refs
    h = (_rms_rows(x, gin_ref[...]) * (1.0 + sc_ref[...]) + sh_ref[...]).astype(BF16)
    y = None
    for c0 in range(0, f, tf):
        gate = _dot(h, wi_ref[:, c0:c0 + tf])
        up = _dot(h, wi_ref[:, f + c0:f + c0 + tf])
        part = _dot((_silu(gate) * up).astype(BF16), wo_ref[c0:c0 + tf, :])
        y = part if y is None else y + part
    o_ref[...] = x + MACARON_W * ga_ref[...] * _rms_rows(y, gout_ref[...])


def _ffn_call(x, mod, mod_k, gains, gain_k, w_in, w_out, layer, tm, mix=None):
    rows, d = x.shape
    f = w_out.shape[1]
    tf = _pick_tile(f, 1408, LANE)
    row = lambda w: pl.BlockSpec((tm, w), lambda i: (i, 0))
    resident = lambda shape: pl.BlockSpec((None,) + shape, lambda i: (layer, 0, 0),
                                          pipeline_mode=pl.Buffered(1))
    in_specs, args = [row(d)], [x]
    if mix is not None:
        oa, ob, oc, w_mix = mix
        in_specs += [row(oa.shape[1]), row(ob.shape[1]), row(oc.shape[1]), mod.spec(5, tm, d), _gain_spec(3, d),
                     pl.BlockSpec(w_mix.shape, lambda i: (0, 0), pipeline_mode=pl.Buffered(1))]
        args += [oa, ob, oc, mod.arr, gains, w_mix]
    in_specs += [mod.spec(mod_k, tm, d), mod.spec(mod_k + 1, tm, d), mod.spec(mod_k + 2, tm, d),
                 _gain_spec(gain_k, d), _gain_spec(gain_k + 1, d),
                 resident((d, 2 * f)), resident((f, d))]
    args += [mod.arr, mod.arr, mod.arr, gains, gains, w_in, w_out]
    return pl.pallas_call(
        functools.partial(_ffn_kernel, f=f, tf=tf, with_mix=mix is not None),
        grid=(rows // tm,),
        in_specs=in_specs,
        out_specs=row(d),
        out_shape=jax.ShapeDtypeStruct((rows, d), F32),
        compiler_params=_cparams(1),
        name="macaron_ffn",
    )(*args)


_PROJ_GROUPS = (("sbq", HW), ("sbk", HW), ("sbv", HW),
                ("gq", HW), ("gk", HW), ("gv", N_HEADS * DV_B), ("gr", N_HEADS * DV_B), ("gb", LANE),
                ("dq", HW), ("dqi", HW), ("dkc", LANE), ("dvc", LANE), ("dkiw", LANE))
_PROJ_OFFSETS = tuple(int(v) for v in np.cumsum([0] + [w for _, w in _PROJ_GROUPS]))
_PROJ_WIDTH = _PROJ_OFFSETS[-1]
_PROJ_OUTPUTS = (("sbq", HW), ("sbk", HW), ("sbv", HW), ("gq", HW), ("gk", HW),
                 ("gv", N_HEADS * DV_B), ("gr", N_HEADS * DV_B), ("gb", HW),
                 ("dq", HW), ("dqi", HW), ("dkc", HEAD_DIM), ("dvc", HEAD_DIM),
                 ("dkiw", HEAD_DIM), ("dkiw", LANE))
_WI_LANE = HEAD_DIM
_PROJ_ROW_OUTPUTS = (("sbq", HW), ("gq", HW), ("gk", HW), ("gv", N_HEADS * DV_B), ("gr", N_HEADS * DV_B),
                     ("gb", HW), ("dq", HW), ("dqi", HW), ("dkc", HEAD_DIM), ("dkiw", HEAD_DIM),
                     ("dkiw", LANE))
_PROJ_T_OUTPUTS = (("sbk", HW), ("sbv", HW), ("dkc", HEAD_DIM), ("dvc", HEAD_DIM), ("dkiw", HEAD_DIM))


def _relayout_w_in(w_in):
    d = w_in.shape[0]
    widths = (HW, HW, HW, HW, HW, N_HEADS * DV_B, N_HEADS * DV_B, GATE_RANK,
              HW, HEAD_DIM, HEAD_DIM, HW, HEAD_DIM, N_HEADS)
    offs = np.cumsum((0,) + widths)
    col = lambda i: w_in[:, offs[i]:offs[i + 1]]
    pad = lambda a, w: jnp.pad(a, ((0, 0), (0, w - a.shape[1])))
    parts = [col(0), col(1), col(2), col(3), col(4), col(5), col(6), pad(col(7), LANE),
             col(8), col(11), pad(col(9), LANE), pad(col(10), LANE),
             pad(jnp.concatenate([col(12), col(13)], axis=1), LANE)]
    out = jnp.concatenate(parts, axis=1)
    assert out.shape == (d, _PROJ_WIDTH)
    return out


def _proj_kernel(x_ref, sh_ref, sc_ref, g_ref, w_ref, wgate_ref, bgate_ref, *refs, row_outs, t_outs, n_prev):
    prev_refs = refs[:n_prev]
    row_refs = refs[n_prev:n_prev + len(row_outs)]
    t_refs = refs[n_prev + len(row_outs):]
    for o_ref, prev_ref in zip(t_refs, prev_refs):
        o_ref[:prev_ref.shape[0]] = prev_ref[...]
    h = (_rms_rows(x_ref[...], g_ref[...]) * (1.0 + sc_ref[...]) + sh_ref[...]).astype(BF16)
    for gi, (name, _) in enumerate(_PROJ_GROUPS):
        p = _dot(h, w_ref[:, _PROJ_OFFSETS[gi]:_PROJ_OFFSETS[gi + 1]])
        if name == "gb":
            pre = _dot(p.astype(BF16), wgate_ref[...]) + bgate_ref[...]
            p = _log_sigmoid(pre) * (1.0 / GATE_TAU)
        for o_ref, (out_group, width) in zip(row_refs, row_outs):
            if out_group == name:
                o_ref[...] = p[:, :width]
        for o_ref, (out_group, width) in zip(t_refs, t_outs):
            if out_group == name:
                o_ref[o_ref.shape[0] - 1] = p.T[:width, :]


def _proj_call(x, mod, gains, w_in, w_gate, b_gate, tm, stacked=None):
    rows, d = x.shape
    in_specs = [pl.BlockSpec((tm, d), lambda i: (i, 0)),
                mod.spec(3, tm, d), mod.spec(4, tm, d), _gain_spec(2, d),
                pl.BlockSpec((d, _PROJ_WIDTH), lambda i: (0, 0)),
                pl.BlockSpec((LANE, HW), lambda i: (0, 0)),
                pl.BlockSpec((1, HW), lambda i: (0, 0))]
    args = [x, mod.arr, mod.arr, gains, w_in, w_gate, b_gate]
    if stacked is None:
        row_outs, t_outs, prev = _PROJ_OUTPUTS, (), ()
        t_specs, t_shapes = [], []
    else:
        layer, n_seq, prev = stacked
        row_outs, t_outs = _PROJ_ROW_OUTPUTS, _PROJ_T_OUTPUTS
        t = rows // n_seq
        tps = t // tm
        tile = lambda n, w: pl.BlockSpec((n, None, w, tm), lambda i: (0, i // tps, 0, i % tps))
        t_specs = [tile(layer + 1, w) for _, w in t_outs]
        t_shapes = [jax.ShapeDtypeStruct((layer + 1, n_seq, w, t), F32) for _, w in t_outs]
        prev = () if prev is None else tuple(prev)
        in_specs += [tile(layer, w) for _, w in t_outs[:len(prev)]]
    outs = pl.pallas_call(
        functools.partial(_proj_kernel, row_outs=row_outs, t_outs=t_outs, n_prev=len(prev)),
        grid=(rows // tm,),
        in_specs=in_specs,
        out_specs=[pl.BlockSpec((tm, w), lambda i: (i, 0)) for _, w in row_outs] + t_specs,
        out_shape=[jax.ShapeDtypeStruct((rows, w), F32) for _, w in row_outs] + t_shapes,
        compiler_params=_cparams(1),
        name="mix_in_proj",
    )(*args, *prev)
    return outs[:len(row_outs)], outs[len(row_outs):]


def _sb_kernel(q_ref, k_ref, v_ref, u_ref, o_ref, acc_scr, c_scr, *, tq, tk, past):
    i = pl.program_id(1)
    head = _head_lane_id(HW, HEAD_DIM)
    q = q_ref[...] * (HEAD_DIM ** -0.5)
    q_st = jnp.concatenate([jnp.where(head == h, q, 0.0) for h in range(N_HEADS)], axis=0).astype(BF16)
    qpos = past + i * tq + lax.broadcasted_iota(jnp.int32, (tq, 1), 0)
    last_blk = (past + (i + 1) * tq - 2) // tk
    acc_scr[...] = jnp.zeros_like(acc_scr)
    c_scr[...] = jnp.zeros_like(c_scr)
    u = u_ref[...]
    rows = N_HEADS * tq

    heads_per_tile = LANE // HEAD_DIM
    low_half = [(lax.broadcasted_iota(jnp.int32, (1, LANE), 1) // HEAD_DIM) == r
                for r in range(heads_per_tile)]

    def body(n, carry, *, masked):
        j = last_blk - n
        k0 = pl.multiple_of(j * tk, tk)
        kb = k_ref[:, pl.ds(k0, tk)].astype(BF16)
        vb = v_ref[:, pl.ds(k0, tk)].astype(BF16)
        z = _dot(q_st, kb).reshape(N_HEADS, tq, tk) * _LOG2_E
        soft = jnp.log2(1.0 + jnp.exp2(_neg_abs(z)))
        log_beta = jnp.minimum(z, 0.0) - soft
        log_keep = log_beta - z
        if masked:
            kpos = k0 + lax.broadcasted_iota(jnp.int32, (1, tk), 1)
            valid = (kpos < qpos)[None]
            log_keep = jnp.where(valid, log_keep, 0.0)
        hi, lo = _split_bf16(log_keep.reshape(rows, tk))
        both = _dot(jnp.concatenate([hi, lo], axis=0), u)
        newer = (both[:rows] + both[rows:]).reshape(N_HEADS, tq, tk)
        c = c_scr[...]
        w = jnp.exp2(log_beta + newer + c)
        if masked:
            w = jnp.where(valid, w, 0.0)
        pv = _dot_nt(w.astype(BF16).reshape(rows, tk), vb).reshape(N_HEADS, tq, HW)
        tiles = []
        for t0 in range(HW // LANE):
            sl = slice(t0 * LANE, (t0 + 1) * LANE)
            col = pv[t0 * heads_per_tile][:, sl]
            for r in range(1, heads_per_tile):
                col = jnp.where(low_half[r], pv[t0 * heads_per_tile + r][:, sl], col)
            tiles.append(col)
        acc_scr[...] += jnp.concatenate(tiles, axis=1)
        c_scr[...] = c + newer[:, :, 0:1] + log_keep[:, :, 0:1]
        return carry

    n_masked = last_blk + 1 - jnp.minimum((past + i * tq) // tk, last_blk + 1)
    lax.fori_loop(0, n_masked, functools.partial(body, masked=True), 0)
    _fori_by_two(n_masked, last_blk + 1, functools.partial(body, masked=False), 0)
    o_ref[...] = acc_scr[...]


def _sb_call(q, k_t, v_t, layer, past, tq, tk):
    b, t, _ = q.shape
    l_pad = k_t.shape[3]
    u = (np.arange(tk)[:, None] > np.arange(tk)[None, :]).astype(np.float32)
    kv_spec = pl.BlockSpec((None, None, HW, l_pad), lambda bi, i: (layer, bi, 0, 0))
    return pl.pallas_call(
        functools.partial(_sb_kernel, tq=tq, tk=tk, past=past),
        grid=(b, t // tq),
        in_specs=[pl.BlockSpec((None, tq, HW), lambda bi, i: (bi, i, 0)), kv_spec, kv_spec,
                  pl.BlockSpec((tk, tk), lambda bi, i: (0, 0))],
        out_specs=pl.BlockSpec((None, tq, HW), lambda bi, i: (bi, i, 0)),
        out_shape=jax.ShapeDtypeStruct((b, t, HW), F32),
        scratch_shapes=[pltpu.VMEM((tq, HW), F32), pltpu.VMEM((N_HEADS, tq, 1), F32)],
        compiler_params=_cparams(2),
        name="stick_breaking_attn",
    )(q, k_t, v_t, jnp.asarray(u, BF16))


def _gla_tables(c):
    n_lev = int(math.log2(c))
    assert 1 << n_lev == c
    t = np.arange(c)
    rows = []
    for lev in range(1, n_lev + 1):
        blk = c >> (lev - 1)
        ref = (t // blk) * blk + blk // 2 - 1
        lo, hi = np.minimum(t, ref), np.maximum(t, ref)
        rows.append(((t[None, :] > lo[:, None]) & (t[None, :] <= hi[:, None])).astype(np.float32))
    rows.append((t[None, :] <= t[:, None]).astype(np.float32))
    rows.append((t[None, :] > t[:, None]).astype(np.float32))
    table = np.concatenate(rows, axis=0)
    lvl = np.full((c, c), n_lev + 1, np.int32)
    for lev in range(1, n_lev + 1):
        blk = c >> (lev - 1)
        half = blk // 2
        same = (t[:, None] // blk) == (t[None, :] // blk)
        pair = same & ((t[:, None] % blk) >= half) & ((t[None, :] % blk) < half)
        lvl[pair] = lev
    lvl[t, t] = 0
    return table, np.tile(lvl, (N_HEADS, 1)), n_lev


def _gla_kernel(q_seqs, k_seqs, v_seqs, r_seqs, la_seqs, s0_ref, tab_ref, lvl_ref, g_ref,
                o_seqs, s_ref, s_seqs, *, c, n_lev, n_chunks, n_steps):
    j = pl.program_id(1)

    @pl.when(j == 0)
    def _():
        s_seqs[...] = s0_ref[...]

    head = _head_lane_id(HW, HEAD_DIM)
    tab = tab_ref[...]
    lvl = lvl_ref[...]

    def chunk(ci, carry):
        for bi in range(q_seqs.shape[0]):
            seq_chunk(bi, pl.ds(pl.multiple_of(ci * c, c), c))
        return carry

    def seq_chunk(bi, rows):
        q_ref, k_ref, v_ref, r_ref, la_ref, o_ref, s_scr = (
            a.at[bi] for a in (q_seqs, k_seqs, v_seqs, r_seqs, la_seqs, o_seqs, s_seqs))
        q = q_ref[rows, :] * (HEAD_DIM ** -0.5)
        k = k_ref[rows, :]
        la_hi, la_lo = _split_bf16(la_ref[rows, :])
        decays = jnp.exp(_dot(tab, la_hi) + _dot(tab, la_lo))
        att = jnp.zeros((N_HEADS * c, c), F32)
        for lev in range(n_lev + 1):
            if lev == 0:
                ql, kl = q, k
            else:
                e = decays[(lev - 1) * c:lev * c]
                ql, kl = q * e, k * e
            lhs = jnp.concatenate([jnp.where(head == h, ql, 0.0) for h in range(N_HEADS)], axis=0)
            a = _dot_nt(lhs.astype(BF16), kl.astype(BF16))
            att = jnp.where(lvl == lev, a, att)
        q_decay = decays[n_lev * c:(n_lev + 1) * c]
        q_in = q * q_decay
        k_out = (k * decays[(n_lev + 1) * c:(n_lev + 2) * c]).astype(BF16)
        s = s_scr[...]
        s_b = s.astype(BF16)
        v_all = v_ref[rows, :].astype(BF16)
        for h in range(N_HEADS):
            vh = v_all[:, h * DV_B:(h + 1) * DV_B]
            o = (_dot(att[h * c:(h + 1) * c].astype(BF16), vh)
                 + _dot(jnp.where(head == h, q_in, 0.0).astype(BF16), s_b))
            y = _rms_rows(o, g_ref[:, h * DV_B:(h + 1) * DV_B])
            o_ref[rows, h * DV_B:(h + 1) * DV_B] = y * _silu(r_ref[rows, h * DV_B:(h + 1) * DV_B])
        kv = _dot_tn(k_out, v_all)
        ds = jnp.concatenate([kv[h * HEAD_DIM:(h + 1) * HEAD_DIM, h * DV_B:(h + 1) * DV_B]
                              for h in range(N_HEADS)], axis=0)
        chunk_decay = jnp.broadcast_to(q_decay[c - 1:c, :], (DV_B, HW)).T
        s_scr[...] = chunk_decay * s + ds

    lax.fori_loop(0, n_chunks, chunk, 0)

    @pl.when(j == n_steps - 1)
    def _():
        s_ref[...] = s_seqs[...]


def _gla_call(q, k, v, r, la, s0, g_gla, c):
    b, t, _ = q.shape
    vw = N_HEADS * DV_B
    tc = _pick_tile(t, 512, c)
    n_steps = t // tc
    nb = 2 if b % 2 == 0 else 1
    table, lvl, n_lev = _gla_tables(c)
    qk_spec = pl.BlockSpec((nb, tc, HW), lambda bi, j: (bi, j, 0))
    vr_spec = pl.BlockSpec((nb, tc, vw), lambda bi, j: (bi, j, 0))
    st_spec = pl.BlockSpec((nb, HW, DV_B), lambda bi, j: (bi, 0, 0))
    const = lambda shape: pl.BlockSpec(shape, lambda bi, j: (0, 0))
    return pl.pallas_call(
        functools.partial(_gla_kernel, c=c, n_lev=n_lev, n_chunks=tc // c, n_steps=n_steps),
        grid=(b // nb, n_steps),
        in_specs=[qk_spec, qk_spec, vr_spec, vr_spec, qk_spec, st_spec,
                  const(table.shape), const(lvl.shape), const((1, vw))],
        out_specs=[vr_spec, st_spec],
        out_shape=[jax.ShapeDtypeStruct((b, t, vw), F32), jax.ShapeDtypeStruct((b, HW, DV_B), F32)],
        scratch_shapes=[pltpu.VMEM((nb, HW, DV_B), F32)],
        compiler_params=_cparams(2),
        name="gated_linear_attn",
    )(q, k, v, r, la, s0, jnp.asarray(table, BF16), jnp.asarray(lvl), g_gla)


_COUNT_ROWS = 32


def _key_to_f32(key):
    key = jnp.maximum(key, _KEY_NEG_INF)
    bits = jnp.where(key < 0, key ^ 0x7FFFFFFF, key)
    return lax.bitcast_convert_type(bits, F32)


def _dsa_kernel(q_ref, qi_ref, wi_ref, kc_ref, vc_ref, ki_ref, eye_ref, tri_ref, o_ref,
                sc_scr, hi_scr, lo_scr, lg_scr, vct_scr, acc_scr, *, tq, tk, past, l_real, n_sel):
    i = pl.program_id(1)
    scale = HEAD_DIM ** -0.5
    eye = eye_ref[...]
    l_pad = sc_scr.shape[0]

    @pl.when(i == 0)
    def _():
        vct_scr[...] = vc_ref[...].astype(BF16)

    q_t = _dot_nt(eye, (q_ref[...] * scale).astype(BF16)).astype(BF16)
    qi_t = _dot_nt(eye, (qi_ref[...] * scale).astype(BF16)).astype(BF16)
    w_t = wi_ref[...].T * (N_HEADS ** -0.5)
    qpos = past + i * tq + lax.broadcasted_iota(jnp.int32, (1, tq), 1)
    chunk_shift = int(math.log2(CHUNK))
    qchunk = lax.shift_right_logical(qpos, chunk_shift)
    k_end = ((past + (i + 1) * tq - 1) // CHUNK + 1) * CHUNK
    n_blk = jnp.minimum((k_end + tk - 1) // tk, l_pad // tk)

    def rows_of(j):
        return pl.ds(pl.multiple_of(j * tk, tk), tk)

    def admissible(j):
        kpos = j * tk + lax.broadcasted_iota(jnp.int32, (tk, 1), 0)
        return (lax.shift_right_logical(kpos, chunk_shift) <= qchunk) & (kpos < l_real)

    def score_blk(j, carry):
        kib = ki_ref[rows_of(j), :].astype(BF16)
        s = jnp.zeros((tk, tq), F32)
        for h in range(N_HEADS):
            sh = _dot(kib, qi_t[h * HEAD_DIM:(h + 1) * HEAD_DIM])
            s = s + w_t[_WI_LANE + h:_WI_LANE + h + 1] * jnp.maximum(sh, 0.0)
        s = jnp.where(admissible(j), jnp.where(s == 0.0, 0.0, s), _NEG_INF)
        sc_scr[rows_of(j), :] = s
        bits = lax.bitcast_convert_type(s, jnp.int32)
        key = jnp.where(bits < 0, bits ^ 0x7FFFFFFF, bits)
        hi_scr[rows_of(j), :] = lax.shift_right_arithmetic(key, 16).astype(jnp.int16)
        lo_scr[rows_of(j), :] = ((key & 0xFFFF) - 32768).astype(jnp.int16)
        return carry

    _fori_by_two(0, n_blk, score_blk, 0)

    def count(ref, pred, one, zero):
        def blk(j, acc):
            m = jnp.where(pred(ref[rows_of(j), :]), one, zero).reshape(tk // _COUNT_ROWS, _COUNT_ROWS, tq)
            for r in range(tk // _COUNT_ROWS):
                acc = acc + m[r]
            return acc
        acc = _fori_by_two(0, n_blk, blk, jnp.zeros((_COUNT_ROWS, tq), one.dtype))
        return jnp.sum(acc.astype(jnp.int32), axis=0, keepdims=True)

    one16, zero16 = jnp.int16(1), jnp.int16(0)

    def bisect16(ref, want):
        def bit(it, tau):
            cand = tau + lax.shift_left(jnp.int32(1), 15 - it)
            c16 = cand.astype(jnp.int16)
            return jnp.where(count(ref, lambda a: a >= c16, one16, zero16) >= want, cand, tau)
        return lax.fori_loop(0, 16, bit, jnp.full((1, tq), -32768, jnp.int32))

    hi_star = bisect16(hi_scr, n_sel)
    h16 = hi_star.astype(jnp.int16)
    want_lo = n_sel - count(hi_scr, lambda a: a > h16, one16, zero16)

    def mask_lo(j, carry):
        lo_scr[rows_of(j), :] = jnp.where(hi_scr[rows_of(j), :] == h16, lo_scr[rows_of(j), :],
                                          jnp.int16(-32768))
        return carry

    lax.fori_loop(0, n_blk, mask_lo, 0)
    lo_star = bisect16(lo_scr, want_lo)
    thr = _key_to_f32(lax.shift_left(hi_star, 16) + (lo_star + 32768))
    need = (n_sel - count(sc_scr, lambda a: a > thr, jnp.float32(1.0), jnp.float32(0.0))).astype(F32)

    tri = tri_ref[...]
    fold = lambda a, op: op(a.reshape(tk // SUBLANE, SUBLANE, tq), axis=0)
    parts = lambda v: tuple(jnp.full((SUBLANE, tq), v, F32) for _ in range(N_HEADS))

    def logit_blk(j, carry):
        m_parts, ties_before = carry
        s = sc_scr[rows_of(j), :]
        tie = s == thr
        rank = _dot(tri, jnp.where(tie, 1.0, 0.0).astype(BF16)) + ties_before
        sel = admissible(j) & ((s > thr) | (tie & (rank <= need)))
        bias = jnp.where(sel, 0.0, _NEG_INF)
        kcb = kc_ref[rows_of(j), :].astype(BF16)
        new_parts = []
        for h in range(N_HEADS):
            lg = _dot(kcb, q_t[h * HEAD_DIM:(h + 1) * HEAD_DIM]) + bias
            lg_scr[h, rows_of(j), :] = lg
            new_parts.append(jnp.maximum(m_parts[h], fold(lg, jnp.max)))
        return tuple(new_parts), rank[tk - 1:tk, :]

    m_parts, _ = _fori_by_two(0, n_blk, logit_blk, (parts(_NEG_INF), jnp.zeros((1, tq), F32)))
    m_use = []
    for h in range(N_HEADS):
        m = jnp.max(m_parts[h], axis=0, keepdims=True)
        m_use.append(jnp.where(m == _NEG_INF, 0.0, m))
    acc_scr[...] = jnp.zeros_like(acc_scr)

    def pv_blk(j, l_parts):
        vct = vct_scr[:, rows_of(j)]
        new_parts = []
        for h in range(N_HEADS):
            hd = slice(h * HEAD_DIM, (h + 1) * HEAD_DIM)
            p = jnp.exp(lg_scr[h, rows_of(j), :] - m_use[h])
            new_parts.append(l_parts[h] + fold(p, jnp.sum))
            acc_scr[hd, :] += _dot(vct, p.astype(BF16))
        return tuple(new_parts)

    l_parts = _fori_by_two(0, n_blk, pv_blk, parts(0.0))
    out_t = jnp.concatenate(
        [acc_scr[h * HEAD_DIM:(h + 1) * HEAD_DIM, :] / jnp.sum(l_parts[h], axis=0, keepdims=True)
         for h in range(N_HEADS)], axis=0)
    o_ref[...] = out_t.T


def _dsa_call(q, qi, wi, kc, vc_t, ki, layer, past, l_real, tq, tk):
    b, t, _ = q.shape
    l_pad = kc.shape[1]
    n_sel = min(TOPK_MAX, l_real // 4)
    assert tk >= n_sel and l_pad % tk == 0 and tk % _COUNT_ROWS == 0
    row = lambda w: pl.BlockSpec((None, tq, w), lambda bi, i: (bi, i, 0))
    full = pl.BlockSpec((None, l_pad, HEAD_DIM), lambda bi, i: (bi, 0, 0))
    full_t = pl.BlockSpec((None, None, HEAD_DIM, l_pad), lambda bi, i: (layer, bi, 0, 0))
    const = lambda n: pl.BlockSpec((n, n), lambda bi, i: (0, 0))
    eye = np.eye(HW, dtype=np.float32)
    tri = (np.arange(tk)[:, None] >= np.arange(tk)[None, :]).astype(np.float32)
    return pl.pallas_call(
        functools.partial(_dsa_kernel, tq=tq, tk=tk, past=past, l_real=l_real, n_sel=n_sel),
        grid=(b, t // tq),
        in_specs=[row(HW), row(HW), row(LANE), full, full_t, full, const(HW), const(tk)],
        out_specs=row(HW),
        out_shape=jax.ShapeDtypeStruct((b, t, HW), F32),
        scratch_shapes=[pltpu.VMEM((l_pad, tq), F32),
                        pltpu.VMEM((l_pad, tq), jnp.int16), pltpu.VMEM((l_pad, tq), jnp.int16),
                        pltpu.VMEM((N_HEADS, l_pad, tq), F32),
                        pltpu.VMEM((HEAD_DIM, l_pad), BF16), pltpu.VMEM((HW, tq), F32)],
        compiler_params=_cparams(2),
        name="indexer_sparse_attn",
    )(q, qi, wi, kc, vc_t, ki, jnp.asarray(eye, BF16), jnp.asarray(tri, BF16))


def _pad_rows(a, n):
    return a if n == a.shape[1] else jnp.pad(a, ((0, 0), (0, n - a.shape[1]), (0, 0)))


def _pad_last(a, n):
    return a if n == a.shape[-1] else jnp.pad(a, [(0, 0)] * (a.ndim - 1) + [(0, n - a.shape[-1])])


def _encoder_layer(x, n_seq, t, mod, past, w, stacked=None):
    rows, d = x.shape
    tm = _pick_tile(t, 512, SUBLANE) if not mod.per_token else rows
    x = _ffn_call(x, mod, 0, w["gains"], 0, w["w_f1_in"], w["w_f1_out"], w["layer"], tm)
    seq = lambda a: a.reshape(n_seq, t, a.shape[-1])
    p_len = 0 if past is None else past["k_sb_t"].shape[2]
    l_real = p_len + t
    tk = 256 if l_real >= 256 else LANE
    l_pad = -(-l_real // tk) * tk
    proj = functools.partial(_proj_call, x, mod, w["gains"], w["w_in"], w["w_gate"], w["b_gate"], tm)
    if stacked is None:
        (sbq, sbk, sbv, gq, gk, gv, gr, la, dq, dqi, dkc, dvc, dki, dwi), _ = proj()
        join_t = lambda old_t, new: _pad_last(
            jnp.concatenate([old_t, jnp.swapaxes(seq(new), 1, 2)], axis=2), l_pad)[None]
        join_r = lambda old, new: _pad_rows(jnp.concatenate([old, seq(new)], axis=1), l_pad)
        k_t, v_t, vc_t = join_t(past["k_sb_t"], sbk), join_t(past["v_sb_t"], sbv), join_t(past["v_dsa_t"], dvc)
        kc_rows, ki_rows = join_r(past["k_dsa"], dkc), join_r(past["k_idx"], dki)
        layer = 0
        new_rows = (seq(sbk).reshape(n_seq, t, N_HEADS, HEAD_DIM), seq(sbv).reshape(n_seq, t, N_HEADS, HEAD_DIM),
                    seq(dkc), seq(dvc), seq(dki))
    else:
        assert past is None and l_pad == t and t % tm == 0 and tm % LANE == 0
        layer, prev = stacked
        (sbq, gq, gk, gv, gr, la, dq, dqi, dkc, dki, dwi), new_rows = proj((layer, n_seq, prev))
        k_t, v_t, _, vc_t, _ = new_rows
        kc_rows, ki_rows = seq(dkc), seq(dki)

    tq = _pick_tile(t, 256, SUBLANE)
    o_a = _sb_call(seq(sbq), k_t, v_t, layer, p_len, tq, tk)

    c = CHUNK if t % CHUNK == 0 else t
    s0 = jnp.zeros((n_seq, HW, DV_B), F32) if past is None else past["gla"]
    o_b, s_new = _gla_call(seq(gq), seq(gk), seq(gv), seq(gr), seq(la), s0, w["g_gla"], c)

    t_c = t if t % LANE == 0 else -(-t // LANE) * LANE
    tq_c = _pick_tile(t_c, 256, LANE)
    qpad = lambda a: _pad_rows(seq(a), t_c)
    o_c = _dsa_call(qpad(dq), qpad(dqi), qpad(dwi), kc_rows, vc_t, ki_rows, layer, p_len, l_real, tq_c, tk)
    o_c = o_c if t_c == t else o_c[:, :t]

    flat = lambda a: a.reshape(rows, a.shape[-1])
    x = _ffn_call(x, mod, 6, w["gains"], 4, w["w_f2_in"], w["w_f2_out"], w["layer"], tm,
                  mix=(flat(o_a), flat(o_b), flat(o_c), w["w_out"]))
    return x, new_rows, s_new.reshape(n_seq, N_HEADS, HEAD_DIM, DV_B)


def kernel(x_prompt, x_sample, cache_k_sb, cache_v_sb, cache_k_dsa, cache_v_dsa, cache_k_idx, state_gla,
           c_prompt, c_sample, w_ada, b_ada, norm_gains, w_ffn1_in, w_ffn1_out, w_ffn2_in, w_ffn2_out,
           w_in, w_gla_gate, b_gla_gate, gla_norm, w_out):
    bp, tp, d = x_prompt.shape
    bs, ts, _ = x_sample.shape
    depth = w_ada.shape[0]
    p_len = cache_k_sb.shape[2]

    mods = _mod_call(jnp.concatenate([c_prompt, c_sample], axis=0), w_ada.astype(BF16), b_ada)
    xp = x_prompt.reshape(bp * tp, d)
    xs = x_sample.reshape(bs * ts, d)
    tm_p = _pick_tile(tp, 512, SUBLANE)
    ffn_w = [a.astype(BF16) for a in (w_ffn1_in, w_ffn1_out, w_ffn2_in, w_ffn2_out)]
    shared_p, gla_p, acc_s = None, [], []
    for l in range(depth):
        w_gate = jnp.pad(w_gla_gate[l], ((0, LANE - GATE_RANK), (0, 0))).astype(BF16)
        w = dict(gains=norm_gains[l].reshape(-1, 1, d),
                 layer=l, w_f1_in=ffn_w[0], w_f1_out=ffn_w[1], w_f2_in=ffn_w[2], w_f2_out=ffn_w[3],
                 w_in=_relayout_w_in(w_in[l]).astype(BF16), w_gate=w_gate,
                 b_gate=b_gla_gate[l].reshape(1, HW), g_gla=gla_norm[l].reshape(1, N_HEADS * DV_B),
                 w_out=w_out[l].astype(BF16))
        mod_p = _Mod(mods[l, :bp].reshape(bp * N_MOD, 1, d), False, tp // tm_p)
        mod_s_rows = jnp.repeat(mods[l, bp:].reshape(bs, N_MOD, d), ts, axis=0)
        mod_s = _Mod(jnp.transpose(mod_s_rows, (1, 0, 2)), True)
        keys_last = lambda c: jnp.transpose(c, (0, 2, 3, 1)).reshape(bs, HW, p_len)
        past = dict(k_sb_t=keys_last(cache_k_sb[l]), v_sb_t=keys_last(cache_v_sb[l]),
                    k_dsa=cache_k_dsa[l], v_dsa_t=jnp.swapaxes(cache_v_dsa[l], 1, 2), k_idx=cache_k_idx[l],
                    gla=state_gla[l].reshape(bs, HW, DV_B))
        xp, shared_p, s_p = _encoder_layer(xp, bp, tp, mod_p, None, w, stacked=(l, shared_p))
        xs, rows_s, s_s = _encoder_layer(xs, bs, ts, mod_s, past, w)
        gla_p.append(s_p)
        acc_s.append(rows_s + (s_s,))
    k_t, v_t, kc_t, vc_t, ki_t = shared_p
    heads_out = lambda a: jnp.transpose(a.reshape(depth, bp, N_HEADS, HEAD_DIM, tp), (0, 1, 4, 2, 3))
    tokens_out = lambda a: jnp.swapaxes(a, 2, 3)
    field = lambda acc, i: jnp.stack([r[i] for r in acc], axis=0)
    return (xp.reshape(bp, tp, d), xs.reshape(bs, ts, d),
            heads_out(k_t), heads_out(v_t), tokens_out(kc_t), tokens_out(vc_t), tokens_out(ki_t),
            jnp.stack(gla_p, axis=0), *(field(acc_s, i) for i in range(6)))
```

```python
import functools
import math

import numpy as np
import jax
import jax.numpy as jnp
from jax import lax
from jax.experimental import pallas as pl
from jax.experimental.pallas import tpu as pltpu

F32 = jnp.float32
BF16 = jnp.bfloat16

CHUNK = 64
N_HEADS = 4
HEAD_DIM = 64
HW = N_HEADS * HEAD_DIM
DV_B = 128
GATE_RANK = 16
GATE_TAU = 16.0
TOPK_MAX = 256
EPS = 1e-6
MACARON_W = 0.5
N_MOD = 9

LANE = 128
SUBLANE = 8
VMEM_LIMIT_BYTES = 56 * 1024 * 1024

_NEG_INF = float("-inf")
_KEY_NEG_INF = -2**31 + 0x7FFFFF
_INT_MIN = -2**31


def _pick_tile(n, target, mult):
    if n <= target:
        return n
    t = (target // mult) * mult
    while t >= mult:
        if n % t == 0:
            return t
        t -= mult
    return n


def _cparams(n_axes):
    return pltpu.CompilerParams(dimension_semantics=("arbitrary",) * n_axes,
                                vmem_limit_bytes=VMEM_LIMIT_BYTES)


def _rms_rows(x, g):
    ms = jnp.mean(x * x, axis=-1, keepdims=True)
    return x * lax.rsqrt(ms + EPS) * g


def _silu(x):
    return x * (1.0 / (1.0 + jnp.exp(-x)))


def _log_sigmoid(x):
    return jnp.minimum(x, 0.0) - jnp.log(1.0 + jnp.exp(-jnp.abs(x)))


_LOG2_E = 1.4426950408889634


def _neg_abs(x):
    bits = lax.bitcast_convert_type(x, jnp.int32) | _INT_MIN
    return lax.bitcast_convert_type(bits, F32)


def _dot(a, b):
    return jnp.dot(a, b, preferred_element_type=F32)


def _dot_nt(a, b):
    return lax.dot_general(a, b, (((1,), (1,)), ((), ())), preferred_element_type=F32)


def _dot_tn(a, b):
    return lax.dot_general(a, b, (((0,), (0,)), ((), ())), preferred_element_type=F32)


def _fori_by_two(lo, hi, body, init):
    n = hi - lo

    def two(p, carry):
        j = lo + 2 * p
        return body(j + 1, body(j, carry))

    carry = lax.fori_loop(0, lax.shift_right_logical(n, 1), two, init)
    return lax.cond((n & 1) == 1, lambda c: body(hi - 1, c), lambda c: c, carry)


def _split_bf16(x):
    hi = x.astype(BF16)
    lo = (x - hi.astype(F32)).astype(BF16)
    return hi, lo


def _head_lane_id(width, per_head):
    return lax.broadcasted_iota(jnp.int32, (1, width), 1) // per_head


def _mod_kernel(c_ref, w_ref, b_ref, o_ref):
    a = _silu(c_ref[...]).astype(BF16)
    o_ref[...] = _dot(a, w_ref[...]) + b_ref[...]


def _mod_call(c, w_ada, b_ada):
    depth, d, nd = w_ada.shape
    n = c.shape[0]
    tn = _pick_tile(nd, 1536, LANE)
    return pl.pallas_call(
        _mod_kernel,
        grid=(depth, nd // tn),
        in_specs=[pl.BlockSpec((n, d), lambda l, j: (0, 0)),
                  pl.BlockSpec((None, d, tn), lambda l, j: (l, 0, j)),
                  pl.BlockSpec((None, 1, tn), lambda l, j: (l, 0, j))],
        out_specs=pl.BlockSpec((None, n, tn), lambda l, j: (l, 0, j)),
        out_shape=jax.ShapeDtypeStruct((depth, n, nd), F32),
        compiler_params=_cparams(2),
        name="adaln_mod",
    )(c, w_ada, b_ada.reshape(depth, 1, nd))


class _Mod:
    def __init__(self, arr, per_token, tiles_per_seq=None):
        self.arr, self.per_token, self.tiles_per_seq = arr, per_token, tiles_per_seq

    def spec(self, k, tm, d):
        if self.per_token:
            return pl.BlockSpec((None, tm, d), lambda i, *_: (k, i, 0))
        tps = self.tiles_per_seq
        return pl.BlockSpec((None, 1, d), lambda i, *_: ((i // tps) * N_MOD + k, 0, 0))


def _gain_spec(k, d):
    return pl.BlockSpec((None, 1, d), lambda i, *_: (k, 0, 0))


def _mix_residual(x, oa_ref, ob_ref, oc_ref, ga_ref, g_ref, w_ref):
    wa = HW
    wb = wa + N_HEADS * DV_B
    y = (_dot(oa_ref[...].astype(BF16), w_ref[0:wa, :])
         + _dot(ob_ref[...].astype(BF16), w_ref[wa:wb, :])
         + _dot(oc_ref[...].astype(BF16), w_ref[wb:wb + HW, :]))
    return x + ga_ref[...] * _rms_rows(y, g_ref[...])


def _ffn_kernel(x_ref, *refs, f, tf, with_mix):
    x = x_ref[...]
    if with_mix:
        x = _mix_residual(x, *refs[:6])
        refs = refs[6:]
    sh_ref, sc_ref, ga_ref, gin_ref, gout_ref, wi_ref, wo_ref, o_ref = refs
    h = (_rms_rows(x, gin_ref[...]) * (1.0 + sc_ref[...]) + sh_ref[...]).astype(BF16)
    y = None
    for c0 in range(0, f, tf):
        gate = _dot(h, wi_ref[:, c0:c0 + tf])
        up = _dot(h, wi_ref[:, f + c0:f + c0 + tf])
        part = _dot((_silu(gate) * up).astype(BF16), wo_ref[c0:c0 + tf, :])
        y = part if y is None else y + part
    o_ref[...] = x + MACARON_W * ga_ref[...] * _rms_rows(y, gout_ref[...])


def _ffn_call(x, mod, mod_k, gains, gain_k, w_in, w_out, layer, tm, mix=None):
    rows, d = x.shape
    f = w_out.shape[1]
    tf = _pick_tile(f, 1408, LANE)
    row = lambda w: pl.BlockSpec((tm, w), lambda i: (i, 0))
    resident = lambda shape: pl.BlockSpec((None,) + shape, lambda i: (layer, 0, 0),
                                          pipeline_mode=pl.Buffered(1))
    in_specs, args = [row(d)], [x]
    if mix is not None:
        oa, ob, oc, w_mix = mix
        in_specs += [row(oa.shape[1]), row(ob.shape[1]), row(oc.shape[1]), mod.spec(5, tm, d), _gain_spec(3, d),
                     pl.BlockSpec(w_mix.shape, lambda i: (0, 0), pipeline_mode=pl.Buffered(1))]
        args += [oa, ob, oc, mod.arr, gains, w_mix]
    in_specs += [mod.spec(mod_k, tm, d), mod.spec(mod_k + 1, tm, d), mod.spec(mod_k + 2, tm, d),
                 _gain_spec(gain_k, d), _gain_spec(gain_k + 1, d),
                 resident((d, 2 * f)), resident((f, d))]
    args += [mod.arr, mod.arr, mod.arr, gains, gains, w_in, w_out]
    return pl.pallas_call(
        functools.partial(_ffn_kernel, f=f, tf=tf, with_mix=mix is not None),
        grid=(rows // tm,),
        in_specs=in_specs,
        out_specs=row(d),
        out_shape=jax.ShapeDtypeStruct((rows, d), F32),
        compiler_params=_cparams(1),
        name="macaron_ffn",
    )(*args)


_PROJ_GROUPS = (("sbq", HW), ("sbk", HW), ("sbv", HW),
                ("gq", HW), ("gk", HW), ("gv", N_HEADS * DV_B), ("gr", N_HEADS * DV_B), ("gb", LANE),
                ("dq", HW), ("dqi", HW), ("dkc", LANE), ("dvc", LANE), ("dkiw", LANE))
_PROJ_OFFSETS = tuple(int(v) for v in np.cumsum([0] + [w for _, w in _PROJ_GROUPS]))
_PROJ_WIDTH = _PROJ_OFFSETS[-1]
_PROJ_OUTPUTS = (("sbq", HW), ("sbk", HW), ("sbv", HW), ("gq", HW), ("gk", HW),
                 ("gv", N_HEADS * DV_B), ("gr", N_HEADS * DV_B), ("gb", HW),
                 ("dq", HW), ("dqi", HW), ("dkc", HEAD_DIM), ("dvc", HEAD_DIM),
                 ("dkiw", HEAD_DIM), ("dkiw", LANE))
_WI_LANE = HEAD_DIM
_PROJ_ROW_OUTPUTS = (("sbq", HW), ("gq", HW), ("gk", HW), ("gv", N_HEADS * DV_B), ("gr", N_HEADS * DV_B),
                     ("gb", HW), ("dq", HW), ("dqi", HW), ("dkc", HEAD_DIM), ("dkiw", HEAD_DIM),
                     ("dkiw", LANE))
_PROJ_T_OUTPUTS = (("sbk", HW), ("sbv", HW), ("dkc", HEAD_DIM), ("dvc", HEAD_DIM), ("dkiw", HEAD_DIM))


def _relayout_w_in(w_in):
    d = w_in.shape[0]
    widths = (HW, HW, HW, HW, HW, N_HEADS * DV_B, N_HEADS * DV_B, GATE_RANK,
              HW, HEAD_DIM, HEAD_DIM, HW, HEAD_DIM, N_HEADS)
    offs = np.cumsum((0,) + widths)
    col = lambda i: w_in[:, offs[i]:offs[i + 1]]
    pad = lambda a, w: jnp.pad(a, ((0, 0), (0, w - a.shape[1])))
    parts = [col(0), col(1), col(2), col(3), col(4), col(5), col(6), pad(col(7), LANE),
             col(8), col(11), pad(col(9), LANE), pad(col(10), LANE),
             pad(jnp.concatenate([col(12), col(13)], axis=1), LANE)]
    out = jnp.concatenate(parts, axis=1)
    assert out.shape == (d, _PROJ_WIDTH)
    return out


def _proj_kernel(x_ref, sh_ref, sc_ref, g_ref, w_ref, wgate_ref, bgate_ref, *refs, row_outs, t_outs, n_prev):
    prev_refs = refs[:n_prev]
    row_refs = refs[n_prev:n_prev + len(row_outs)]
    t_refs = refs[n_prev + len(row_outs):]
    for o_ref, prev_ref in zip(t_refs, prev_refs):
        o_ref[:prev_ref.shape[0]] = prev_ref[...]
    h = (_rms_rows(x_ref[...], g_ref[...]) * (1.0 + sc_ref[...]) + sh_ref[...]).astype(BF16)
    for gi, (name, _) in enumerate(_PROJ_GROUPS):
        p = _dot(h, w_ref[:, _PROJ_OFFSETS[gi]:_PROJ_OFFSETS[gi + 1]])
        if name == "gb":
            pre = _dot(p.astype(BF16), wgate_ref[...]) + bgate_ref[...]
            p = _log_sigmoid(pre) * (1.0 / GATE_TAU)
        for o_ref, (out_group, width) in zip(row_refs, row_outs):
            if out_group == name:
                o_ref[...] = p[:, :width]
        for o_ref, (out_group, width) in zip(t_refs, t_outs):
            if out_group == name:
                o_ref[o_ref.shape[0] - 1] = p.T[:width, :]


def _proj_call(x, mod, gains, w_in, w_gate, b_gate, tm, stacked=None):
    rows, d = x.shape
    in_specs = [pl.BlockSpec((tm, d), lambda i: (i, 0)),
                mod.spec(3, tm, d), mod.spec(4, tm, d), _gain_spec(2, d),
                pl.BlockSpec((d, _PROJ_WIDTH), lambda i: (0, 0)),
                pl.BlockSpec((LANE, HW), lambda i: (0, 0)),
                pl.BlockSpec((1, HW), lambda i: (0, 0))]
    args = [x, mod.arr, mod.arr, gains, w_in, w_gate, b_gate]
    if stacked is None:
        row_outs, t_outs, prev = _PROJ_OUTPUTS, (), ()
        t_specs, t_shapes = [], []
    else:
        layer, n_seq, prev = stacked
        row_outs, t_outs = _PROJ_ROW_OUTPUTS, _PROJ_T_OUTPUTS
        t = rows // n_seq
        tps = t // tm
        tile = lambda n, w: pl.BlockSpec((n, None, w, tm), lambda i: (0, i // tps, 0, i % tps))
        t_specs = [tile(layer + 1, w) for _, w in t_outs]
        t_shapes = [jax.ShapeDtypeStruct((layer + 1, n_seq, w, t), F32) for _, w in t_outs]
        prev = () if prev is None else tuple(prev)
        in_specs += [tile(layer, w) for _, w in t_outs[:len(prev)]]
    outs = pl.pallas_call(
        functools.partial(_proj_kernel, row_outs=row_outs, t_outs=t_outs, n_prev=len(prev)),
        grid=(rows // tm,),
        in_specs=in_specs,
        out_specs=[pl.BlockSpec((tm, w), lambda i: (i, 0)) for _, w in row_outs] + t_specs,
        out_shape=[jax.ShapeDtypeStruct((rows, w), F32) for _, w in row_outs] + t_shapes,
        compiler_params=_cparams(1),
        name="mix_in_proj",
    )(*args, *prev)
    return outs[:len(row_outs)], outs[len(row_outs):]


def _sb_kernel(q_ref, k_ref, v_ref, *rest, tq, tk, past, new_apart):
    if new_apart:
        kn_ref, vn_ref, u_ref, o_ref, acc_scr, c_scr = rest
    else:
        u_ref, o_ref, acc_scr, c_scr = rest
    i = pl.program_id(1)
    head = _head_lane_id(HW, HEAD_DIM)
    q = q_ref[...] * (HEAD_DIM ** -0.5)
    q_st = jnp.concatenate([jnp.where(head == h, q, 0.0) for h in range(N_HEADS)], axis=0).astype(BF16)
    qpos = past + i * tq + lax.broadcasted_iota(jnp.int32, (tq, 1), 0)
    last_blk = (past + (i + 1) * tq - 2) // tk
    acc_scr[...] = jnp.zeros_like(acc_scr)
    c_scr[...] = jnp.zeros_like(c_scr)
    u = u_ref[...]
    rows = N_HEADS * tq

    heads_per_tile = LANE // HEAD_DIM
    low_half = [(lax.broadcasted_iota(jnp.int32, (1, LANE), 1) // HEAD_DIM) == r
                for r in range(heads_per_tile)]

    def body(n, carry, *, masked):
        j = last_blk - n
        k0 = pl.multiple_of(j * tk, tk)
        if new_apart and masked:
            cols = pl.ds(pl.multiple_of(k0 - past, tk), tk)
            kb, vb = kn_ref[:, cols].astype(BF16), vn_ref[:, cols].astype(BF16)
        else:
            kb = k_ref[:, pl.ds(k0, tk)].astype(BF16)
            vb = v_ref[:, pl.ds(k0, tk)].astype(BF16)
        z = _dot(q_st, kb).reshape(N_HEADS, tq, tk) * _LOG2_E
        soft = jnp.log2(1.0 + jnp.exp2(_neg_abs(z)))
        log_beta = jnp.minimum(z, 0.0) - soft
        log_keep = log_beta - z
        if masked:
            kpos = k0 + lax.broadcasted_iota(jnp.int32, (1, tk), 1)
            valid = (kpos < qpos)[None]
            log_keep = jnp.where(valid, log_keep, 0.0)
        hi, lo = _split_bf16(log_keep.reshape(rows, tk))
        both = _dot(jnp.concatenate([hi, lo], axis=0), u)
        newer = (both[:rows] + both[rows:]).reshape(N_HEADS, tq, tk)
        c = c_scr[...]
        w = jnp.exp2(log_beta + newer + c)
        if masked:
            w = jnp.where(valid, w, 0.0)
        pv = _dot_nt(w.astype(BF16).reshape(rows, tk), vb).reshape(N_HEADS, tq, HW)
        tiles = []
        for t0 in range(HW // LANE):
            sl = slice(t0 * LANE, (t0 + 1) * LANE)
            col = pv[t0 * heads_per_tile][:, sl]
            for r in range(1, heads_per_tile):
                col = jnp.where(low_half[r], pv[t0 * heads_per_tile + r][:, sl], col)
            tiles.append(col)
        acc_scr[...] += jnp.concatenate(tiles, axis=1)
        c_scr[...] = c + newer[:, :, 0:1] + log_keep[:, :, 0:1]
        return carry

    n_masked = last_blk + 1 - jnp.minimum((past + i * tq) // tk, last_blk + 1)
    lax.fori_loop(0, n_masked, functools.partial(body, masked=True), 0)
    _fori_by_two(n_masked, last_blk + 1, functools.partial(body, masked=False), 0)
    o_ref[...] = acc_scr[...]


def _sb_call(q, k_t, v_t, layer, past, tq, tk, new_kv=None):
    b, t, _ = q.shape
    u = (np.arange(tk)[:, None] > np.arange(tk)[None, :]).astype(np.float32)
    kv_spec = pl.BlockSpec((None, None, HW, k_t.shape[3]), lambda bi, i: (layer, bi, 0, 0))
    in_specs = [pl.BlockSpec((None, tq, HW), lambda bi, i: (bi, i, 0)), kv_spec, kv_spec]
    args = [q, k_t, v_t]
    if new_kv is not None:
        assert past % tk == 0 and k_t.shape[3] == past and t <= tk
        in_specs += [pl.BlockSpec((None, HW, new_kv[0].shape[2]), lambda bi, i: (bi, 0, 0))] * 2
        args += list(new_kv)
    return pl.pallas_call(
        functools.partial(_sb_kernel, tq=tq, tk=tk, past=past, new_apart=new_kv is not None),
        grid=(b, t // tq),
        in_specs=in_specs + [pl.BlockSpec((tk, tk), lambda bi, i: (0, 0))],
        out_specs=pl.BlockSpec((None, tq, HW), lambda bi, i: (bi, i, 0)),
        out_shape=jax.ShapeDtypeStruct((b, t, HW), F32),
        scratch_shapes=[pltpu.VMEM((tq, HW), F32), pltpu.VMEM((N_HEADS, tq, 1), F32)],
        compiler_params=_cparams(2),
        name="stick_breaking_attn",
    )(*args, jnp.asarray(u, BF16))


def _gla_tables(c):
    n_lev = int(math.log2(c))
    assert 1 << n_lev == c
    t = np.arange(c)
    rows = []
    for lev in range(1, n_lev + 1):
        blk = c >> (lev - 1)
        ref = (t // blk) * blk + blk // 2 - 1
        lo, hi = np.minimum(t, ref), np.maximum(t, ref)
        rows.append(((t[None, :] > lo[:, None]) & (t[None, :] <= hi[:, None])).astype(np.float32))
    rows.append((t[None, :] <= t[:, None]).astype(np.float32))
    rows.append((t[None, :] > t[:, None]).astype(np.float32))
    table = np.concatenate(rows, axis=0)
    lvl = np.full((c, c), n_lev + 1, np.int32)
    for lev in range(1, n_lev + 1):
        blk = c >> (lev - 1)
        half = blk // 2
        same = (t[:, None] // blk) == (t[None, :] // blk)
        pair = same & ((t[:, None] % blk) >= half) & ((t[None, :] % blk) < half)
        lvl[pair] = lev
    lvl[t, t] = 0
    return np.concatenate([table, table], axis=1), np.tile(lvl.T, (1, N_HEADS)), n_lev


def _gla_kernel(q_seqs, k_seqs, v_seqs, r_seqs, la_seqs, s0_ref, tab_ref, lvl_ref, g_ref,
                o_seqs, s_ref, s_seqs, *, c, n_lev, n_chunks, n_steps):
    j = pl.program_id(1)

    @pl.when(j == 0)
    def _():
        s_seqs[...] = s0_ref[...]

    head = _head_lane_id(HW, HEAD_DIM)
    tab = tab_ref[...]
    lvl = lvl_ref[...]

    def chunk(ci, carry):
        for bi in range(q_seqs.shape[0]):
            seq_chunk(bi, pl.ds(pl.multiple_of(ci * c, c), c))
        return carry

    def seq_chunk(bi, rows):
        q_ref, k_ref, v_ref, r_ref, la_ref, o_ref, s_scr = (
            a.at[bi] for a in (q_seqs, k_seqs, v_seqs, r_seqs, la_seqs, o_seqs, s_seqs))
        q = q_ref[rows, :] * (HEAD_DIM ** -0.5)
        k = k_ref[rows, :]
        decays = jnp.exp(_dot(tab, jnp.concatenate(_split_bf16(la_ref[rows, :]), axis=0)))
        by_head = lambda a: jnp.concatenate([jnp.where(head == h, a, 0.0) for h in range(N_HEADS)],
                                            axis=0).astype(BF16)
        att_t = jnp.zeros((c, N_HEADS * c), F32)
        for lev in range(n_lev + 1):
            if lev == 0:
                ql, kl = q, k
            else:
                e = decays[(lev - 1) * c:lev * c]
                ql, kl = q * e, k * e
            a_t = _dot_nt(kl.astype(BF16), by_head(ql))
            att_t = jnp.where(lvl == lev, a_t, att_t)
        q_decay = decays[n_lev * c:(n_lev + 1) * c]
        k_out = (k * decays[(n_lev + 1) * c:(n_lev + 2) * c]).astype(BF16)
        s = s_scr[...]
        v_all = v_ref[rows, :].astype(BF16)
        intra = _dot_tn(att_t.astype(BF16), v_all)
        inter = _dot(by_head(q * q_decay), s.astype(BF16))
        for h in range(N_HEADS):
            o = intra[h * c:(h + 1) * c, h * DV_B:(h + 1) * DV_B] + inter[h * c:(h + 1) * c]
            y = _rms_rows(o, g_ref[:, h * DV_B:(h + 1) * DV_B])
            o_ref[rows, h * DV_B:(h + 1) * DV_B] = y * _silu(r_ref[rows, h * DV_B:(h + 1) * DV_B])
        kv = _dot_tn(k_out, v_all)
        ds = jnp.concatenate([kv[h * HEAD_DIM:(h + 1) * HEAD_DIM, h * DV_B:(h + 1) * DV_B]
                              for h in range(N_HEADS)], axis=0)
        chunk_decay = jnp.broadcast_to(q_decay[c - 1:c, :], (DV_B, HW)).T
        s_scr[...] = chunk_decay * s + ds

    lax.fori_loop(0, n_chunks, chunk, 0)

    @pl.when(j == n_steps - 1)
    def _():
        s_ref[...] = s_seqs[...]


def _gla_call(q, k, v, r, la, s0, g_gla, c):
    b, t, _ = q.shape
    vw = N_HEADS * DV_B
    tc = _pick_tile(t, 512, c)
    n_steps = t // tc
    nb = max(n for n in (4, 2, 1) if b % n == 0)
    table, lvl, n_lev = _gla_tables(c)
    qk_spec = pl.BlockSpec((nb, tc, HW), lambda bi, j: (bi, j, 0))
    vr_spec = pl.BlockSpec((nb, tc, vw), lambda bi, j: (bi, j, 0))
    st_spec = pl.BlockSpec((nb, HW, DV_B), lambda bi, j: (bi, 0, 0))
    const = lambda shape: pl.BlockSpec(shape, lambda bi, j: (0, 0))
    return pl.pallas_call(
        functools.partial(_gla_kernel, c=c, n_lev=n_lev, n_chunks=tc // c, n_steps=n_steps),
        grid=(b // nb, n_steps),
        in_specs=[qk_spec, qk_spec, vr_spec, vr_spec, qk_spec, st_spec,
                  const(table.shape), const(lvl.shape), const((1, vw))],
        out_specs=[vr_spec, st_spec],
        out_shape=[jax.ShapeDtypeStruct((b, t, vw), F32), jax.ShapeDtypeStruct((b, HW, DV_B), F32)],
        scratch_shapes=[pltpu.VMEM((nb, HW, DV_B), F32)],
        compiler_params=_cparams(2),
        name="gated_linear_attn",
    )(q, k, v, r, la, s0, jnp.asarray(table, BF16), jnp.asarray(lvl), g_gla)


_COUNT_ROWS = 32


def _key_to_f32(key):
    key = jnp.maximum(key, _KEY_NEG_INF)
    bits = jnp.where(key < 0, key ^ 0x7FFFFFFF, key)
    return lax.bitcast_convert_type(bits, F32)


def _dsa_kernel(q_ref, qi_ref, wi_ref, kc_ref, vc_ref, ki_ref, eye_ref, tri_ref, o_ref,
                sc_scr, hi_scr, lo_scr, lg_scr, vct_scr, acc_scr, *, tq, tk, past, l_real, n_sel):
    i = pl.program_id(1)
    scale = HEAD_DIM ** -0.5
    eye = eye_ref[...]
    l_pad = sc_scr.shape[0]

    @pl.when(i == 0)
    def _():
        vct_scr[...] = vc_ref[...].astype(BF16)

    q_t = _dot_nt(eye, (q_ref[...] * scale).astype(BF16)).astype(BF16)
    qi_t = _dot_nt(eye, (qi_ref[...] * scale).astype(BF16)).astype(BF16)
    w_t = wi_ref[...].T * (N_HEADS ** -0.5)
    qpos = past + i * tq + lax.broadcasted_iota(jnp.int32, (1, tq), 1)
    chunk_shift = int(math.log2(CHUNK))
    qchunk = lax.shift_right_logical(qpos, chunk_shift)
    k_end = ((past + (i + 1) * tq - 1) // CHUNK + 1) * CHUNK
    n_blk = jnp.minimum((k_end + tk - 1) // tk, l_pad // tk)

    def rows_of(j):
        return pl.ds(pl.multiple_of(j * tk, tk), tk)

    def admissible(j):
        kpos = j * tk + lax.broadcasted_iota(jnp.int32, (tk, 1), 0)
        return (lax.shift_right_logical(kpos, chunk_shift) <= qchunk) & (kpos < l_real)

    def score_blk(j, carry):
        kib = ki_ref[rows_of(j), :].astype(BF16)
        s = jnp.zeros((tk, tq), F32)
        for h in range(N_HEADS):
            sh = _dot(kib, qi_t[h * HEAD_DIM:(h + 1) * HEAD_DIM])
            s = s + w_t[_WI_LANE + h:_WI_LANE + h + 1] * jnp.maximum(sh, 0.0)
        s = jnp.where(admissible(j), jnp.where(s == 0.0, 0.0, s), _NEG_INF)
        sc_scr[rows_of(j), :] = s
        bits = lax.bitcast_convert_type(s, jnp.int32)
        key = jnp.where(bits < 0, bits ^ 0x7FFFFFFF, bits)
        hi_scr[rows_of(j), :] = lax.shift_right_arithmetic(key, 16).astype(jnp.int16)
        lo_scr[rows_of(j), :] = ((key & 0xFFFF) - 32768).astype(jnp.int16)
        return carry

    _fori_by_two(0, n_blk, score_blk, 0)

    def count(ref, pred, one, zero):
        def blk(j, acc):
            m = jnp.where(pred(ref[rows_of(j), :]), one, zero).reshape(tk // _COUNT_ROWS, _COUNT_ROWS, tq)
            for r in range(tk // _COUNT_ROWS):
                acc = acc + m[r]
            return acc
        acc = _fori_by_two(0, n_blk, blk, jnp.zeros((_COUNT_ROWS, tq), one.dtype))
        return jnp.sum(acc.astype(jnp.int32), axis=0, keepdims=True)

    one16, zero16 = jnp.int16(1), jnp.int16(0)

    def bisect16(ref, want):
        def bit(it, tau):
            cand = tau + lax.shift_left(jnp.int32(1), 15 - it)
            c16 = cand.astype(jnp.int16)
            return jnp.where(count(ref, lambda a: a >= c16, one16, zero16) >= want, cand, tau)
        return lax.fori_loop(0, 16, bit, jnp.full((1, tq), -32768, jnp.int32))

    hi_star = bisect16(hi_scr, n_sel)
    h16 = hi_star.astype(jnp.int16)
    want_lo = n_sel - count(hi_scr, lambda a: a > h16, one16, zero16)

    def mask_lo(j, carry):
        lo_scr[rows_of(j), :] = jnp.where(hi_scr[rows_of(j), :] == h16, lo_scr[rows_of(j), :],
                                          jnp.int16(-32768))
        return carry

    lax.fori_loop(0, n_blk, mask_lo, 0)
    lo_star = bisect16(lo_scr, want_lo)
    thr = _key_to_f32(lax.shift_left(hi_star, 16) + (lo_star + 32768))
    need = (n_sel - count(sc_scr, lambda a: a > thr, jnp.float32(1.0), jnp.float32(0.0))).astype(F32)

    tri = tri_ref[...]
    fold = lambda a, op: op(a.reshape(tk // SUBLANE, SUBLANE, tq), axis=0)
    parts = lambda v: tuple(jnp.full((SUBLANE, tq), v, F32) for _ in range(N_HEADS))

    def logit_blk(j, carry):
        m_parts, ties_before = carry
        s = sc_scr[rows_of(j), :]
        tie = s == thr
        rank = _dot(tri, jnp.where(tie, 1.0, 0.0).astype(BF16)) + ties_before
        sel = admissible(j) & ((s > thr) | (tie & (rank <= need)))
        bias = jnp.where(sel, 0.0, _NEG_INF)
        kcb = kc_ref[rows_of(j), :].astype(BF16)
        new_parts = []
        for h in range(N_HEADS):
            lg = _dot(kcb, q_t[h * HEAD_DIM:(h + 1) * HEAD_DIM]) + bias
            lg_scr[h, rows_of(j), :] = lg
            new_parts.append(jnp.maximum(m_parts[h], fold(lg, jnp.max)))
        return tuple(new_parts), rank[tk - 1:tk, :]

    m_parts, _ = _fori_by_two(0, n_blk, logit_blk, (parts(_NEG_INF), jnp.zeros((1, tq), F32)))
    m_use = []
    for h in range(N_HEADS):
        m = jnp.max(m_parts[h], axis=0, keepdims=True)
        m_use.append(jnp.where(m == _NEG_INF, 0.0, m))
    acc_scr[...] = jnp.zeros_like(acc_scr)

    def pv_blk(j, l_parts):
        vct = vct_scr[:, rows_of(j)]
        new_parts = []
        for h in range(N_HEADS):
            hd = slice(h * HEAD_DIM, (h + 1) * HEAD_DIM)
            p = jnp.exp(lg_scr[h, rows_of(j), :] - m_use[h])
            new_parts.append(l_parts[h] + fold(p, jnp.sum))
            acc_scr[hd, :] += _dot(vct, p.astype(BF16))
        return tuple(new_parts)

    l_parts = _fori_by_two(0, n_blk, pv_blk, parts(0.0))
    out_t = jnp.concatenate(
        [acc_scr[h * HEAD_DIM:(h + 1) * HEAD_DIM, :] / jnp.sum(l_parts[h], axis=0, keepdims=True)
         for h in range(N_HEADS)], axis=0)
    o_ref[...] = out_t.T


def _dsa_call(q, qi, wi, kc, vc_t, ki, layer, past, l_real, tq, tk):
    b, t, _ = q.shape
    l_pad = kc.shape[1]
    n_sel = min(TOPK_MAX, l_real // 4)
    assert tk >= n_sel and l_pad % tk == 0 and tk % _COUNT_ROWS == 0
    row = lambda w: pl.BlockSpec((None, tq, w), lambda bi, i: (bi, i, 0))
    full = pl.BlockSpec((None, l_pad, HEAD_DIM), lambda bi, i: (bi, 0, 0))
    full_t = pl.BlockSpec((None, None, HEAD_DIM, l_pad), lambda bi, i: (layer, bi, 0, 0))
    const = lambda n: pl.BlockSpec((n, n), lambda bi, i: (0, 0))
    eye = np.eye(HW, dtype=np.float32)
    tri = (np.arange(tk)[:, None] >= np.arange(tk)[None, :]).astype(np.float32)
    return pl.pallas_call(
        functools.partial(_dsa_kernel, tq=tq, tk=tk, past=past, l_real=l_real, n_sel=n_sel),
        grid=(b, t // tq),
        in_specs=[row(HW), row(HW), row(LANE), full, full_t, full, const(HW), const(tk)],
        out_specs=row(HW),
        out_shape=jax.ShapeDtypeStruct((b, t, HW), F32),
        scratch_shapes=[pltpu.VMEM((l_pad, tq), F32),
                        pltpu.VMEM((l_pad, tq), jnp.int16), pltpu.VMEM((l_pad, tq), jnp.int16),
                        pltpu.VMEM((N_HEADS, l_pad, tq), F32),
                        pltpu.VMEM((HEAD_DIM, l_pad), BF16), pltpu.VMEM((HW, tq), F32)],
        compiler_params=_cparams(2),
        name="indexer_sparse_attn",
    )(q, qi, wi, kc, vc_t, ki, jnp.asarray(eye, BF16), jnp.asarray(tri, BF16))


def _pad_rows(a, n):
    return a if n == a.shape[1] else jnp.pad(a, ((0, 0), (0, n - a.shape[1]), (0, 0)))


def _pad_last(a, n):
    return a if n == a.shape[-1] else jnp.pad(a, [(0, 0)] * (a.ndim - 1) + [(0, n - a.shape[-1])])


def _encoder_layer(x, n_seq, t, mod, past, w, stacked=None):
    rows, d = x.shape
    tm = _pick_tile(t, 512, SUBLANE) if not mod.per_token else rows
    x = _ffn_call(x, mod, 0, w["gains"], 0, w["w_f1_in"], w["w_f1_out"], w["layer"], tm)
    seq = lambda a: a.reshape(n_seq, t, a.shape[-1])
    p_len = 0 if past is None else past["k_sb_t"].shape[3]
    l_real = p_len + t
    tk = 256 if l_real >= 256 else LANE
    l_pad = -(-l_real // tk) * tk
    tq = _pick_tile(t, 256, SUBLANE)
    proj = functools.partial(_proj_call, x, mod, w["gains"], w["w_in"], w["w_gate"], w["b_gate"], tm)
    if stacked is None:
        (sbq, sbk, sbv, gq, gk, gv, gr, la, dq, dqi, dkc, dvc, dki, dwi), _ = proj()
        lyr = past["layer"]
        keys_last = lambda new: jnp.swapaxes(seq(new), 1, 2)
        join_t = lambda old_t, new: _pad_last(jnp.concatenate([old_t, keys_last(new)], axis=2), l_pad)[None]
        join_r = lambda old, new: _pad_rows(jnp.concatenate([old, seq(new)], axis=1), l_pad)
        if p_len % tk == 0 and t <= tk:
            new_kv = tuple(_pad_last(keys_last(a), tk) for a in (sbk, sbv))
            o_a = _sb_call(seq(sbq), past["k_sb_t"], past["v_sb_t"], lyr, p_len, tq, tk, new_kv=new_kv)
        else:
            o_a = _sb_call(seq(sbq), join_t(past["k_sb_t"][lyr], sbk), join_t(past["v_sb_t"][lyr], sbv),
                           0, p_len, tq, tk)
        vc_t = join_t(past["v_dsa_t"], dvc)
        kc_rows, ki_rows = join_r(past["k_dsa"], dkc), join_r(past["k_idx"], dki)
        layer = 0
        new_rows = (seq(sbk).reshape(n_seq, t, N_HEADS, HEAD_DIM), seq(sbv).reshape(n_seq, t, N_HEADS, HEAD_DIM),
                    seq(dkc), seq(dvc), seq(dki))
    else:
        assert past is None and l_pad == t and t % tm == 0 and tm % LANE == 0
        layer, prev = stacked
        (sbq, gq, gk, gv, gr, la, dq, dqi, dkc, dki, dwi), new_rows = proj((layer, n_seq, prev))
        k_t, v_t, _, vc_t, _ = new_rows
        kc_rows, ki_rows = seq(dkc), seq(dki)
        o_a = _sb_call(seq(sbq), k_t, v_t, layer, p_len, tq, tk)

    c = CHUNK if t % CHUNK == 0 else t
    s0 = jnp.zeros((n_seq, HW, DV_B), F32) if past is None else past["gla"]
    o_b, s_new = _gla_call(seq(gq), seq(gk), seq(gv), seq(gr), seq(la), s0, w["g_gla"], c)

    t_c = t if t % LANE == 0 else -(-t // LANE) * LANE
    tq_c = _pick_tile(t_c, 256, LANE)
    qpad = lambda a: _pad_rows(seq(a), t_c)
    o_c = _dsa_call(qpad(dq), qpad(dqi), qpad(dwi), kc_rows, vc_t, ki_rows, layer, p_len, l_real, tq_c, tk)
    o_c = o_c if t_c == t else o_c[:, :t]

    flat = lambda a: a.reshape(rows, a.shape[-1])
    x = _ffn_call(x, mod, 6, w["gains"], 4, w["w_f2_in"], w["w_f2_out"], w["layer"], tm,
                  mix=(flat(o_a), flat(o_b), flat(o_c), w["w_out"]))
    return x, new_rows, s_new.reshape(n_seq, N_HEADS, HEAD_DIM, DV_B)


def kernel(x_prompt, x_sample, cache_k_sb, cache_v_sb, cache_k_dsa, cache_v_dsa, cache_k_idx, state_gla,
           c_prompt, c_sample, w_ada, b_ada, norm_gains, w_ffn1_in, w_ffn1_out, w_ffn2_in, w_ffn2_out,
           w_in, w_gla_gate, b_gla_gate, gla_norm, w_out):
    bp, tp, d = x_prompt.shape
    bs, ts, _ = x_sample.shape
    depth = w_ada.shape[0]
    p_len = cache_k_sb.shape[2]

    mods = _mod_call(jnp.concatenate([c_prompt, c_sample], axis=0), w_ada.astype(BF16), b_ada)
    xp = x_prompt.reshape(bp * tp, d)
    xs = x_sample.reshape(bs * ts, d)
    tm_p = _pick_tile(tp, 512, SUBLANE)
    ffn_w = [a.astype(BF16) for a in (w_ffn1_in, w_ffn1_out, w_ffn2_in, w_ffn2_out)]
    sb_cache_t = [jnp.transpose(c, (0, 1, 3, 4, 2)).reshape(depth, bs, HW, p_len) for c in (cache_k_sb, cache_v_sb)]
    shared_p, gla_p, acc_s = None, [], []
    for l in range(depth):
        w_gate = jnp.pad(w_gla_gate[l], ((0, LANE - GATE_RANK), (0, 0))).astype(BF16)
        w = dict(gains=norm_gains[l].reshape(-1, 1, d),
                 layer=l, w_f1_in=ffn_w[0], w_f1_out=ffn_w[1], w_f2_in=ffn_w[2], w_f2_out=ffn_w[3],
                 w_in=_relayout_w_in(w_in[l]).astype(BF16), w_gate=w_gate,
                 b_gate=b_gla_gate[l].reshape(1, HW), g_gla=gla_norm[l].reshape(1, N_HEADS * DV_B),
                 w_out=w_out[l].astype(BF16))
        mod_p = _Mod(mods[l, :bp].reshape(bp * N_MOD, 1, d), False, tp // tm_p)
        mod_s_rows = jnp.repeat(mods[l, bp:].reshape(bs, N_MOD, d), ts, axis=0)
        mod_s = _Mod(jnp.transpose(mod_s_rows, (1, 0, 2)), True)
        past = dict(layer=l, k_sb_t=sb_cache_t[0], v_sb_t=sb_cache_t[1],
                    k_dsa=cache_k_dsa[l], v_dsa_t=jnp.swapaxes(cache_v_dsa[l], 1, 2), k_idx=cache_k_idx[l],
                    gla=state_gla[l].reshape(bs, HW, DV_B))
        xp, shared_p, s_p = _encoder_layer(xp, bp, tp, mod_p, None, w, stacked=(l, shared_p))
        xs, rows_s, s_s = _encoder_layer(xs, bs, ts, mod_s, past, w)
        gla_p.append(s_p)
        acc_s.append(rows_s + (s_s,))
    k_t, v_t, kc_t, vc_t, ki_t = shared_p
    heads_out = lambda a: jnp.transpose(a.reshape(depth, bp, N_HEADS, HEAD_DIM, tp), (0, 1, 4, 2, 3))
    tokens_out = lambda a: jnp.swapaxes(a, 2, 3)
    field = lambda acc, i: jnp.stack([r[i] for r in acc], axis=0)
    return (xp.reshape(bp, tp, d), xs.reshape(bs, ts, d),
            heads_out(k_t), heads_out(v_t), tokens_out(kc_t), tokens_out(vc_t), tokens_out(ki_t),
            jnp.stack(gla_p, axis=0), *(field(acc_s, i) for i in range(6)))
```

```python
import functools
import math

import numpy as np
import jax
import jax.numpy as jnp
from jax import lax
from jax.experimental import pallas as pl
from jax.experimental.pallas import tpu as pltpu

F32 = jnp.float32
BF16 = jnp.bfloat16

CHUNK = 64
N_HEADS = 4
HEAD_DIM = 64
HW = N_HEADS * HEAD_DIM
DV_B = 128
GATE_RANK = 16
GATE_TAU = 16.0
TOPK_MAX = 256
EPS = 1e-6
MACARON_W = 0.5
N_MOD = 9

LANE = 128
SUBLANE = 8
VMEM_LIMIT_BYTES = 56 * 1024 * 1024

_NEG_INF = float("-inf")
_KEY_NEG_INF = -2**31 + 0x7FFFFF
_INT_MIN = -2**31


def _pick_tile(n, target, mult):
    if n <= target:
        return n
    t = (target // mult) * mult
    while t >= mult:
        if n % t == 0:
            return t
        t -= mult
    return n


def _cparams(n_axes):
    return pltpu.CompilerParams(dimension_semantics=("arbitrary",) * n_axes,
                                vmem_limit_bytes=VMEM_LIMIT_BYTES)


def _rms_rows(x, g):
    ms = jnp.mean(x * x, axis=-1, keepdims=True)
    return x * lax.rsqrt(ms + EPS) * g


def _silu(x):
    return x * (1.0 / (1.0 + jnp.exp(-x)))


def _log_sigmoid(x):
    return jnp.minimum(x, 0.0) - jnp.log(1.0 + jnp.exp(-jnp.abs(x)))


_LOG2_E = 1.4426950408889634


def _neg_abs(x):
    bits = lax.bitcast_convert_type(x, jnp.int32) | _INT_MIN
    return lax.bitcast_convert_type(bits, F32)


def _dot(a, b):
    return jnp.dot(a, b, preferred_element_type=F32)


def _dot_nt(a, b):
    return lax.dot_general(a, b, (((1,), (1,)), ((), ())), preferred_element_type=F32)


def _dot_tn(a, b):
    return lax.dot_general(a, b, (((0,), (0,)), ((), ())), preferred_element_type=F32)


def _fori_by_two(lo, hi, body, init):
    n = hi - lo

    def two(p, carry):
        j = lo + 2 * p
        return body(j + 1, body(j, carry))

    carry = lax.fori_loop(0, lax.shift_right_logical(n, 1), two, init)
    return lax.cond((n & 1) == 1, lambda c: body(hi - 1, c), lambda c: c, carry)


def _split_bf16(x):
    hi = x.astype(BF16)
    lo = (x - hi.astype(F32)).astype(BF16)
    return hi, lo


def _head_lane_id(width, per_head):
    return lax.broadcasted_iota(jnp.int32, (1, width), 1) // per_head


def _mod_kernel(c_ref, w_ref, b_ref, o_ref):
    a = _silu(c_ref[...]).astype(BF16)
    o_ref[...] = _dot(a, w_ref[...]) + b_ref[...]


def _mod_call(c, w_ada, b_ada):
    depth, d, nd = w_ada.shape
    n = c.shape[0]
    tn = _pick_tile(nd, 1536, LANE)
    return pl.pallas_call(
        _mod_kernel,
        grid=(depth, nd // tn),
        in_specs=[pl.BlockSpec((n, d), lambda l, j: (0, 0)),
                  pl.BlockSpec((None, d, tn), lambda l, j: (l, 0, j)),
                  pl.BlockSpec((None, 1, tn), lambda l, j: (l, 0, j))],
        out_specs=pl.BlockSpec((None, n, tn), lambda l, j: (l, 0, j)),
        out_shape=jax.ShapeDtypeStruct((depth, n, nd), F32),
        compiler_params=_cparams(2),
        name="adaln_mod",
    )(c, w_ada, b_ada.reshape(depth, 1, nd))


class _Mod:
    def __init__(self, arr, per_token, tiles_per_seq=None):
        self.arr, self.per_token, self.tiles_per_seq = arr, per_token, tiles_per_seq

    def spec(self, k, tm, d):
        if self.per_token:
            return pl.BlockSpec((None, tm, d), lambda i, *_: (k, i, 0))
        tps = self.tiles_per_seq
        return pl.BlockSpec((None, 1, d), lambda i, *_: ((i // tps) * N_MOD + k, 0, 0))


def _gain_spec(k, d):
    return pl.BlockSpec((None, 1, d), lambda i, *_: (k, 0, 0))


def _mix_residual(x, oa_ref, ob_ref, oc_ref, ga_ref, g_ref, w_ref):
    wa = HW
    wb = wa + N_HEADS * DV_B
    y = (_dot(oa_ref[...].astype(BF16), w_ref[0:wa, :])
         + _dot(ob_ref[...].astype(BF16), w_ref[wa:wb, :])
         + _dot(oc_ref[...].astype(BF16), w_ref[wb:wb + HW, :]))
    return x + ga_ref[...] * _rms_rows(y, g_ref[...])


def _ffn_kernel(x_ref, *refs, f, tf, with_mix):
    x = x_ref[...]
    if with_mix:
        x = _mix_residual(x, *refs[:6])
        refs = refs[6:]
    sh_ref, sc_ref, ga_ref, gin_ref, gout_ref, wi_ref, wo_ref, o_ref = refs
    h = (_rms_rows(x, gin_ref[...]) * (1.0 + sc_ref[...]) + sh_ref[...]).astype(BF16)
    y = None
    for c0 in range(0, f, tf):
        gate = _dot(h, wi_ref[:, c0:c0 + tf])
        up = _dot(h, wi_ref[:, f + c0:f + c0 + tf])
        part = _dot((_silu(gate) * up).astype(BF16), wo_ref[c0:c0 + tf, :])
        y = part if y is None else y + part
    o_ref[...] = x + MACARON_W * ga_ref[...] * _rms_rows(y, gout_ref[...])


def _ffn_call(x, mod, mod_k, gains, gain_k, w_in, w_out, layer, tm, mix=None):
    rows, d = x.shape
    f = w_out.shape[1]
    tf = _pick_tile(f, 1408, LANE)
    row = lambda w: pl.BlockSpec((tm, w), lambda i: (i, 0))
    resident = lambda shape: pl.BlockSpec((None,) + shape, lambda i: (layer, 0, 0),
                                          pipeline_mode=pl.Buffered(1))
    in_specs, args = [row(d)], [x]
    if mix is not None:
        oa, ob, oc, w_mix = mix
        in_specs += [row(oa.shape[1]), row(ob.shape[1]), row(oc.shape[1]), mod.spec(5, tm, d), _gain_spec(3, d),
                     pl.BlockSpec(w_mix.shape, lambda i: (0, 0), pipeline_mode=pl.Buffered(1))]
        args += [oa, ob, oc, mod.arr, gains, w_mix]
    in_specs += [mod.spec(mod_k, tm, d), mod.spec(mod_k + 1, tm, d), mod.spec(mod_k + 2, tm, d),
                 _gain_spec(gain_k, d), _gain_spec(gain_k + 1, d),
                 resident((d, 2 * f)), resident((f, d))]
    args += [mod.arr, mod.arr, mod.arr, gains, gains, w_in, w_out]
    return pl.pallas_call(
        functools.partial(_ffn_kernel, f=f, tf=tf, with_mix=mix is not None),
        grid=(rows // tm,),
        in_specs=in_specs,
        out_specs=row(d),
        out_shape=jax.ShapeDtypeStruct((rows, d), F32),
        compiler_params=_cparams(1),
        name="macaron_ffn",
    )(*args)


_PROJ_GROUPS = (("sbq", HW), ("sbk", HW), ("sbv", HW),
                ("gq", HW), ("gk", HW), ("gv", N_HEADS * DV_B), ("gr", N_HEADS * DV_B), ("gb", LANE),
                ("dq", HW), ("dqi", HW), ("dkc", LANE), ("dvc", LANE), ("dkiw", LANE))
_PROJ_OFFSETS = tuple(int(v) for v in np.cumsum([0] + [w for _, w in _PROJ_GROUPS]))
_PROJ_WIDTH = _PROJ_OFFSETS[-1]
_PROJ_OUTPUTS = (("sbq", HW), ("sbk", HW), ("sbv", HW), ("gq", HW), ("gk", HW),
                 ("gv", N_HEADS * DV_B), ("gr", N_HEADS * DV_B), ("gb", HW),
                 ("dq", HW), ("dqi", HW), ("dkc", HEAD_DIM), ("dvc", HEAD_DIM),
                 ("dkiw", HEAD_DIM), ("dkiw", LANE))
_WI_LANE = HEAD_DIM
_PROJ_ROW_OUTPUTS = (("sbq", HW), ("gq", HW), ("gk", HW), ("gv", N_HEADS * DV_B), ("gr", N_HEADS * DV_B),
                     ("gb", HW), ("dq", HW), ("dqi", HW), ("dkc", HEAD_DIM), ("dkiw", HEAD_DIM),
                     ("dkiw", LANE))
_PROJ_T_OUTPUTS = (("sbk", HW), ("sbv", HW), ("dkc", HEAD_DIM), ("dvc", HEAD_DIM), ("dkiw", HEAD_DIM))


def _relayout_w_in(w_in):
    d = w_in.shape[0]
    widths = (HW, HW, HW, HW, HW, N_HEADS * DV_B, N_HEADS * DV_B, GATE_RANK,
              HW, HEAD_DIM, HEAD_DIM, HW, HEAD_DIM, N_HEADS)
    offs = np.cumsum((0,) + widths)
    col = lambda i: w_in[:, offs[i]:offs[i + 1]]
    pad = lambda a, w: jnp.pad(a, ((0, 0), (0, w - a.shape[1])))
    parts = [col(0), col(1), col(2), col(3), col(4), col(5), col(6), pad(col(7), LANE),
             col(8), col(11), pad(col(9), LANE), pad(col(10), LANE),
             pad(jnp.concatenate([col(12), col(13)], axis=1), LANE)]
    out = jnp.concatenate(parts, axis=1)
    assert out.shape == (d, _PROJ_WIDTH)
    return out


def _proj_kernel(x_ref, sh_ref, sc_ref, g_ref, w_ref, wgate_ref, bgate_ref, *refs, row_outs, t_outs, n_prev):
    prev_refs = refs[:n_prev]
    row_refs = refs[n_prev:n_prev + len(row_outs)]
    t_refs = refs[n_prev + len(row_outs):]
    for o_ref, prev_ref in zip(t_refs, prev_refs):
        o_ref[:prev_ref.shape[0]] = prev_ref[...]
    h = (_rms_rows(x_ref[...], g_ref[...]) * (1.0 + sc_ref[...]) + sh_ref[...]).astype(BF16)
    for gi, (name, _) in enumerate(_PROJ_GROUPS):
        p = _dot(h, w_ref[:, _PROJ_OFFSETS[gi]:_PROJ_OFFSETS[gi + 1]])
        if name == "gb":
            pre = _dot(p.astype(BF16), wgate_ref[...]) + bgate_ref[...]
            p = _log_sigmoid(pre) * (1.0 / GATE_TAU)
        for o_ref, (out_group, width) in zip(row_refs, row_outs):
            if out_group == name:
                o_ref[...] = p[:, :width]
        for o_ref, (out_group, width) in zip(t_refs, t_outs):
            if out_group == name:
                o_ref[o_ref.shape[0] - 1] = p.T[:width, :]


def _proj_call(x, mod, gains, w_in, w_gate, b_gate, tm, stacked=None):
    rows, d = x.shape
    in_specs = [pl.BlockSpec((tm, d), lambda i: (i, 0)),
                mod.spec(3, tm, d), mod.spec(4, tm, d), _gain_spec(2, d),
                pl.BlockSpec((d, _PROJ_WIDTH), lambda i: (0, 0)),
                pl.BlockSpec((LANE, HW), lambda i: (0, 0)),
                pl.BlockSpec((1, HW), lambda i: (0, 0))]
    args = [x, mod.arr, mod.arr, gains, w_in, w_gate, b_gate]
    if stacked is None:
        row_outs, t_outs, prev = _PROJ_OUTPUTS, (), ()
        t_specs, t_shapes = [], []
    else:
        layer, n_seq, prev = stacked
        row_outs, t_outs = _PROJ_ROW_OUTPUTS, _PROJ_T_OUTPUTS
        t = rows // n_seq
        tps = t // tm
        tile = lambda n, w: pl.BlockSpec((n, None, w, tm), lambda i: (0, i // tps, 0, i % tps))
        t_specs = [tile(layer + 1, w) for _, w in t_outs]
        t_shapes = [jax.ShapeDtypeStruct((layer + 1, n_seq, w, t), F32) for _, w in t_outs]
        prev = () if prev is None else tuple(prev)
        in_specs += [tile(layer, w) for _, w in t_outs[:len(prev)]]
    outs = pl.pallas_call(
        functools.partial(_proj_kernel, row_outs=row_outs, t_outs=t_outs, n_prev=len(prev)),
        grid=(rows // tm,),
        in_specs=in_specs,
        out_specs=[pl.BlockSpec((tm, w), lambda i: (i, 0)) for _, w in row_outs] + t_specs,
        out_shape=[jax.ShapeDtypeStruct((rows, w), F32) for _, w in row_outs] + t_shapes,
        compiler_params=_cparams(1),
        name="mix_in_proj",
    )(*args, *prev)
    return outs[:len(row_outs)], outs[len(row_outs):]


def _sb_kernel(q_ref, k_ref, v_ref, *rest, tq, tk, past, new_apart):
    if new_apart:
        kn_ref, vn_ref, u_ref, o_ref, acc_scr, c_scr = rest
    else:
        u_ref, o_ref, acc_scr, c_scr = rest
    i = pl.program_id(1)
    head = _head_lane_id(HW, HEAD_DIM)
    q = q_ref[...] * (HEAD_DIM ** -0.5)
    q_st = jnp.concatenate([jnp.where(head == h, q, 0.0) for h in range(N_HEADS)], axis=0).astype(BF16)
    qpos = past + i * tq + lax.broadcasted_iota(jnp.int32, (tq, 1), 0)
    last_blk = (past + (i + 1) * tq - 2) // tk
    acc_scr[...] = jnp.zeros_like(acc_scr)
    c_scr[...] = jnp.zeros_like(c_scr)
    u = u_ref[...]
    rows = N_HEADS * tq

    heads_per_tile = LANE // HEAD_DIM
    low_half = [(lax.broadcasted_iota(jnp.int32, (1, LANE), 1) // HEAD_DIM) == r
                for r in range(heads_per_tile)]

    def body(n, carry, *, masked):
        j = last_blk - n
        k0 = pl.multiple_of(j * tk, tk)
        if new_apart and masked:
            cols = pl.ds(pl.multiple_of(k0 - past, tk), tk)
            kb, vb = kn_ref[:, cols].astype(BF16), vn_ref[:, cols].astype(BF16)
        else:
            kb = k_ref[:, pl.ds(k0, tk)].astype(BF16)
            vb = v_ref[:, pl.ds(k0, tk)].astype(BF16)
        z = _dot(q_st, kb).reshape(N_HEADS, tq, tk) * _LOG2_E
        soft = jnp.log2(1.0 + jnp.exp2(_neg_abs(z)))
        log_beta = jnp.minimum(z, 0.0) - soft
        log_keep = log_beta - z
        if masked:
            kpos = k0 + lax.broadcasted_iota(jnp.int32, (1, tk), 1)
            valid = (kpos < qpos)[None]
            log_keep = jnp.where(valid, log_keep, 0.0)
        newer = _dot(log_keep.astype(BF16).reshape(rows, tk), u).reshape(N_HEADS, tq, tk)
        c = c_scr[...]
        w = jnp.exp2(log_beta + newer + c)
        if masked:
            w = jnp.where(valid, w, 0.0)
        pv = _dot_nt(w.astype(BF16).reshape(rows, tk), vb).reshape(N_HEADS, tq, HW)
        tiles = []
        for t0 in range(HW // LANE):
            sl = slice(t0 * LANE, (t0 + 1) * LANE)
            col = pv[t0 * heads_per_tile][:, sl]
            for r in range(1, heads_per_tile):
                col = jnp.where(low_half[r], pv[t0 * heads_per_tile + r][:, sl], col)
            tiles.append(col)
        acc_scr[...] += jnp.concatenate(tiles, axis=1)
        c_scr[...] = c + newer[:, :, 0:1] + log_keep[:, :, 0:1]
        return carry

    n_masked = last_blk + 1 - jnp.minimum((past + i * tq) // tk, last_blk + 1)
    lax.fori_loop(0, n_masked, functools.partial(body, masked=True), 0)
    _fori_by_two(n_masked, last_blk + 1, functools.partial(body, masked=False), 0)
    o_ref[...] = acc_scr[...]


def _sb_call(q, k_t, v_t, layer, past, tq, tk, new_kv=None):
    b, t, _ = q.shape
    u = (np.arange(tk)[:, None] > np.arange(tk)[None, :]).astype(np.float32)
    kv_spec = pl.BlockSpec((None, None, HW, k_t.shape[3]), lambda bi, i: (layer, bi, 0, 0))
    in_specs = [pl.BlockSpec((None, tq, HW), lambda bi, i: (bi, i, 0)), kv_spec, kv_spec]
    args = [q, k_t, v_t]
    if new_kv is not None:
        assert past % tk == 0 and k_t.shape[3] == past and t <= tk
        in_specs += [pl.BlockSpec((None, HW, new_kv[0].shape[2]), lambda bi, i: (bi, 0, 0))] * 2
        args += list(new_kv)
    return pl.pallas_call(
        functools.partial(_sb_kernel, tq=tq, tk=tk, past=past, new_apart=new_kv is not None),
        grid=(b, t // tq),
        in_specs=in_specs + [pl.BlockSpec((tk, tk), lambda bi, i: (0, 0))],
        out_specs=pl.BlockSpec((None, tq, HW), lambda bi, i: (bi, i, 0)),
        out_shape=jax.ShapeDtypeStruct((b, t, HW), F32),
        scratch_shapes=[pltpu.VMEM((tq, HW), F32), pltpu.VMEM((N_HEADS, tq, 1), F32)],
        compiler_params=_cparams(2),
        name="stick_breaking_attn",
    )(*args, jnp.asarray(u, BF16))


def _gla_tables(c):
    n_lev = int(math.log2(c))
    assert 1 << n_lev == c
    t = np.arange(c)
    rows = []
    for lev in range(1, n_lev + 1):
        blk = c >> (lev - 1)
        ref = (t // blk) * blk + blk // 2 - 1
        lo, hi = np.minimum(t, ref), np.maximum(t, ref)
        rows.append(((t[None, :] > lo[:, None]) & (t[None, :] <= hi[:, None])).astype(np.float32))
    rows.append((t[None, :] <= t[:, None]).astype(np.float32))
    rows.append((t[None, :] > t[:, None]).astype(np.float32))
    table = np.concatenate(rows, axis=0)
    lvl = np.full((c, c), n_lev + 1, np.int32)
    for lev in range(1, n_lev + 1):
        blk = c >> (lev - 1)
        half = blk // 2
        same = (t[:, None] // blk) == (t[None, :] // blk)
        pair = same & ((t[:, None] % blk) >= half) & ((t[None, :] % blk) < half)
        lvl[pair] = lev
    lvl[t, t] = 0
    return np.concatenate([table, table], axis=1), np.tile(lvl.T, (1, N_HEADS)), n_lev


def _gla_kernel(q_seqs, k_seqs, v_seqs, r_seqs, la_seqs, s0_ref, tab_ref, lvl_ref, g_ref,
                o_seqs, s_ref, s_seqs, *, c, n_lev, n_chunks, n_steps):
    j = pl.program_id(1)

    @pl.when(j == 0)
    def _():
        s_seqs[...] = s0_ref[...]

    head = _head_lane_id(HW, HEAD_DIM)
    tab = tab_ref[...]
    lvl = lvl_ref[...]

    def chunk(ci, carry):
        for bi in range(q_seqs.shape[0]):
            seq_chunk(bi, pl.ds(pl.multiple_of(ci * c, c), c))
        return carry

    def seq_chunk(bi, rows):
        q_ref, k_ref, v_ref, r_ref, la_ref, o_ref, s_scr = (
            a.at[bi] for a in (q_seqs, k_seqs, v_seqs, r_seqs, la_seqs, o_seqs, s_seqs))
        q = q_ref[rows, :] * (HEAD_DIM ** -0.5)
        k = k_ref[rows, :]
        decays = jnp.exp(_dot(tab, jnp.concatenate(_split_bf16(la_ref[rows, :]), axis=0)))
        by_head = lambda a: jnp.concatenate([jnp.where(head == h, a, 0.0) for h in range(N_HEADS)],
                                            axis=0).astype(BF16)
        att_t = jnp.zeros((c, N_HEADS * c), F32)
        for lev in range(n_lev + 1):
            if lev == 0:
                ql, kl = q, k
            else:
                e = decays[(lev - 1) * c:lev * c]
                ql, kl = q * e, k * e
            a_t = _dot_nt(kl.astype(BF16), by_head(ql))
            att_t = jnp.where(lvl == lev, a_t, att_t)
        q_decay = decays[n_lev * c:(n_lev + 1) * c]
        k_out = (k * decays[(n_lev + 1) * c:(n_lev + 2) * c]).astype(BF16)
        s = s_scr[...]
        v_all = v_ref[rows, :].astype(BF16)
        intra = _dot_tn(att_t.astype(BF16), v_all)
        inter = _dot(by_head(q * q_decay), s.astype(BF16))
        for h in range(N_HEADS):
            o = intra[h * c:(h + 1) * c, h * DV_B:(h + 1) * DV_B] + inter[h * c:(h + 1) * c]
            y = _rms_rows(o, g_ref[:, h * DV_B:(h + 1) * DV_B])
            o_ref[rows, h * DV_B:(h + 1) * DV_B] = y * _silu(r_ref[rows, h * DV_B:(h + 1) * DV_B])
        kv = _dot_tn(k_out, v_all)
        ds = jnp.concatenate([kv[h * HEAD_DIM:(h + 1) * HEAD_DIM, h * DV_B:(h + 1) * DV_B]
                              for h in range(N_HEADS)], axis=0)
        chunk_decay = jnp.broadcast_to(q_decay[c - 1:c, :], (DV_B, HW)).T
        s_scr[...] = chunk_decay * s + ds

    lax.fori_loop(0, n_chunks, chunk, 0)

    @pl.when(j == n_steps - 1)
    def _():
        s_ref[...] = s_seqs[...]


def _gla_call(q, k, v, r, la, s0, g_gla, c):
    b, t, _ = q.shape
    vw = N_HEADS * DV_B
    tc = _pick_tile(t, 512, c)
    n_steps = t // tc
    nb = max(n for n in (4, 2, 1) if b % n == 0)
    table, lvl, n_lev = _gla_tables(c)
    qk_spec = pl.BlockSpec((nb, tc, HW), lambda bi, j: (bi, j, 0))
    vr_spec = pl.BlockSpec((nb, tc, vw), lambda bi, j: (bi, j, 0))
    st_spec = pl.BlockSpec((nb, HW, DV_B), lambda bi, j: (bi, 0, 0))
    const = lambda shape: pl.BlockSpec(shape, lambda bi, j: (0, 0))
    return pl.pallas_call(
        functools.partial(_gla_kernel, c=c, n_lev=n_lev, n_chunks=tc // c, n_steps=n_steps),
        grid=(b // nb, n_steps),
        in_specs=[qk_spec, qk_spec, vr_spec, vr_spec, qk_spec, st_spec,
                  const(table.shape), const(lvl.shape), const((1, vw))],
        out_specs=[vr_spec, st_spec],
        out_shape=[jax.ShapeDtypeStruct((b, t, vw), F32), jax.ShapeDtypeStruct((b, HW, DV_B), F32)],
        scratch_shapes=[pltpu.VMEM((nb, HW, DV_B), F32)],
        compiler_params=_cparams(2),
        name="gated_linear_attn",
    )(q, k, v, r, la, s0, jnp.asarray(table, BF16), jnp.asarray(lvl), g_gla)


_COUNT_ROWS = 32


def _key_to_f32(key):
    key = jnp.maximum(key, _KEY_NEG_INF)
    bits = jnp.where(key < 0, key ^ 0x7FFFFFFF, key)
    return lax.bitcast_convert_type(bits, F32)


def _dsa_kernel(q_ref, qi_ref, wi_ref, kc_ref, vc_ref, ki_ref, eye_ref, tri_ref, o_ref,
                sc_scr, hi_scr, lo_scr, lg_scr, vct_scr, acc_scr, *, tq, tk, past, l_real, n_sel):
    i = pl.program_id(1)
    scale = HEAD_DIM ** -0.5
    eye = eye_ref[...]
    l_pad = sc_scr.shape[0]

    @pl.when(i == 0)
    def _():
        vct_scr[...] = vc_ref[...].astype(BF16)

    q_t = _dot_nt(eye, (q_ref[...] * scale).astype(BF16)).astype(BF16)
    qi_t = _dot_nt(eye, (qi_ref[...] * scale).astype(BF16)).astype(BF16)
    w_t = wi_ref[...].T * (N_HEADS ** -0.5)
    qpos = past + i * tq + lax.broadcasted_iota(jnp.int32, (1, tq), 1)
    chunk_shift = int(math.log2(CHUNK))
    qchunk = lax.shift_right_logical(qpos, chunk_shift)
    k_end = ((past + (i + 1) * tq - 1) // CHUNK + 1) * CHUNK
    n_blk = jnp.minimum((k_end + tk - 1) // tk, l_pad // tk)

    def rows_of(j):
        return pl.ds(pl.multiple_of(j * tk, tk), tk)

    def admissible(j):
        kpos = j * tk + lax.broadcasted_iota(jnp.int32, (tk, 1), 0)
        return (lax.shift_right_logical(kpos, chunk_shift) <= qchunk) & (kpos < l_real)

    def score_blk(j, carry):
        kib = ki_ref[rows_of(j), :].astype(BF16)
        s = jnp.zeros((tk, tq), F32)
        for h in range(N_HEADS):
            sh = _dot(kib, qi_t[h * HEAD_DIM:(h + 1) * HEAD_DIM])
            s = s + w_t[_WI_LANE + h:_WI_LANE + h + 1] * jnp.maximum(sh, 0.0)
        s = jnp.where(admissible(j), jnp.where(s == 0.0, 0.0, s), _NEG_INF)
        sc_scr[rows_of(j), :] = s
        bits = lax.bitcast_convert_type(s, jnp.int32)
        key = jnp.where(bits < 0, bits ^ 0x7FFFFFFF, bits)
        hi_scr[rows_of(j), :] = lax.shift_right_arithmetic(key, 16).astype(jnp.int16)
        lo_scr[rows_of(j), :] = ((key & 0xFFFF) - 32768).astype(jnp.int16)
        return carry

    _fori_by_two(0, n_blk, score_blk, 0)

    def count(ref, pred, one, zero):
        def blk(j, acc):
            m = jnp.where(pred(ref[rows_of(j), :]), one, zero).reshape(tk // _COUNT_ROWS, _COUNT_ROWS, tq)
            for r in range(tk // _COUNT_ROWS):
                acc = acc + m[r]
            return acc
        acc = _fori_by_two(0, n_blk, blk, jnp.zeros((_COUNT_ROWS, tq), one.dtype))
        return jnp.sum(acc.astype(jnp.int32), axis=0, keepdims=True)

    one16, zero16 = jnp.int16(1), jnp.int16(0)

    def bisect16(ref, want):
        def bit(it, tau):
            cand = tau + lax.shift_left(jnp.int32(1), 15 - it)
            c16 = cand.astype(jnp.int16)
            return jnp.where(count(ref, lambda a: a >= c16, one16, zero16) >= want, cand, tau)
        return lax.fori_loop(0, 16, bit, jnp.full((1, tq), -32768, jnp.int32))

    hi_star = bisect16(hi_scr, n_sel)
    h16 = hi_star.astype(jnp.int16)
    want_lo = n_sel - count(hi_scr, lambda a: a > h16, one16, zero16)

    def mask_lo(j, carry):
        lo_scr[rows_of(j), :] = jnp.where(hi_scr[rows_of(j), :] == h16, lo_scr[rows_of(j), :],
                                          jnp.int16(-32768))
        return carry

    lax.fori_loop(0, n_blk, mask_lo, 0)
    lo_star = bisect16(lo_scr, want_lo)
    thr = _key_to_f32(lax.shift_left(hi_star, 16) + (lo_star + 32768))
    need = (n_sel - count(sc_scr, lambda a: a > thr, jnp.float32(1.0), jnp.float32(0.0))).astype(F32)

    tri = tri_ref[...]
    fold = lambda a, op: op(a.reshape(tk // SUBLANE, SUBLANE, tq), axis=0)
    parts = lambda v: tuple(jnp.full((SUBLANE, tq), v, F32) for _ in range(N_HEADS))

    def logit_blk(j, carry):
        m_parts, ties_before = carry
        s = sc_scr[rows_of(j), :]
        tie = s == thr
        rank = _dot(tri, jnp.where(tie, 1.0, 0.0).astype(BF16)) + ties_before
        sel = admissible(j) & ((s > thr) | (tie & (rank <= need)))
        bias = jnp.where(sel, 0.0, _NEG_INF)
        kcb = kc_ref[rows_of(j), :].astype(BF16)
        new_parts = []
        for h in range(N_HEADS):
            lg = _dot(kcb, q_t[h * HEAD_DIM:(h + 1) * HEAD_DIM]) + bias
            lg_scr[h, rows_of(j), :] = lg
            new_parts.append(jnp.maximum(m_parts[h], fold(lg, jnp.max)))
        return tuple(new_parts), rank[tk - 1:tk, :]

    m_parts, _ = _fori_by_two(0, n_blk, logit_blk, (parts(_NEG_INF), jnp.zeros((1, tq), F32)))
    m_use = []
    for h in range(N_HEADS):
        m = jnp.max(m_parts[h], axis=0, keepdims=True)
        m_use.append(jnp.where(m == _NEG_INF, 0.0, m))
    acc_scr[...] = jnp.zeros_like(acc_scr)

    def pv_blk(j, l_parts):
        vct = vct_scr[:, rows_of(j)]
        new_parts = []
        for h in range(N_HEADS):
            hd = slice(h * HEAD_DIM, (h + 1) * HEAD_DIM)
            p = jnp.exp(lg_scr[h, rows_of(j), :] - m_use[h])
            new_parts.append(l_parts[h] + fold(p, jnp.sum))
            acc_scr[hd, :] += _dot(vct, p.astype(BF16))
        return tuple(new_parts)

    l_parts = _fori_by_two(0, n_blk, pv_blk, parts(0.0))
    out_t = jnp.concatenate(
        [acc_scr[h * HEAD_DIM:(h + 1) * HEAD_DIM, :] / jnp.sum(l_parts[h], axis=0, keepdims=True)
         for h in range(N_HEADS)], axis=0)
    o_ref[...] = out_t.T


def _dsa_call(q, qi, wi, kc, vc_t, ki, layer, past, l_real, tq, tk):
    b, t, _ = q.shape
    l_pad = kc.shape[1]
    n_sel = min(TOPK_MAX, l_real // 4)
    assert tk >= n_sel and l_pad % tk == 0 and tk % _COUNT_ROWS == 0
    row = lambda w: pl.BlockSpec((None, tq, w), lambda bi, i: (bi, i, 0))
    full = pl.BlockSpec((None, l_pad, HEAD_DIM), lambda bi, i: (bi, 0, 0))
    full_t = pl.BlockSpec((None, None, HEAD_DIM, l_pad), lambda bi, i: (layer, bi, 0, 0))
    const = lambda n: pl.BlockSpec((n, n), lambda bi, i: (0, 0))
    eye = np.eye(HW, dtype=np.float32)
    tri = (np.arange(tk)[:, None] >= np.arange(tk)[None, :]).astype(np.float32)
    return pl.pallas_call(
        functools.partial(_dsa_kernel, tq=tq, tk=tk, past=past, l_real=l_real, n_sel=n_sel),
        grid=(b, t // tq),
        in_specs=[row(HW), row(HW), row(LANE), full, full_t, full, const(HW), const(tk)],
        out_specs=row(HW),
        out_shape=jax.ShapeDtypeStruct((b, t, HW), F32),
        scratch_shapes=[pltpu.VMEM((l_pad, tq), F32),
                        pltpu.VMEM((l_pad, tq), jnp.int16), pltpu.VMEM((l_pad, tq), jnp.int16),
                        pltpu.VMEM((N_HEADS, l_pad, tq), F32),
                        pltpu.VMEM((HEAD_DIM, l_pad), BF16), pltpu.VMEM((HW, tq), F32)],
        compiler_params=_cparams(2),
        name="indexer_sparse_attn",
    )(q, qi, wi, kc, vc_t, ki, jnp.asarray(eye, BF16), jnp.asarray(tri, BF16))


def _pad_rows(a, n):
    return a if n == a.shape[1] else jnp.pad(a, ((0, 0), (0, n - a.shape[1]), (0, 0)))


def _pad_last(a, n):
    return a if n == a.shape[-1] else jnp.pad(a, [(0, 0)] * (a.ndim - 1) + [(0, n - a.shape[-1])])


def _encoder_layer(x, n_seq, t, mod, past, w, stacked=None):
    rows, d = x.shape
    tm = _pick_tile(t, 512, SUBLANE) if not mod.per_token else rows
    x = _ffn_call(x, mod, 0, w["gains"], 0, w["w_f1_in"], w["w_f1_out"], w["layer"], tm)
    seq = lambda a: a.reshape(n_seq, t, a.shape[-1])
    p_len = 0 if past is None else past["k_sb_t"].shape[3]
    l_real = p_len + t
    tk = 256 if l_real >= 256 else LANE
    l_pad = -(-l_real // tk) * tk
    tq = _pick_tile(t, 256, SUBLANE)
    proj = functools.partial(_proj_call, x, mod, w["gains"], w["w_in"], w["w_gate"], w["b_gate"], tm)
    if stacked is None:
        (sbq, sbk, sbv, gq, gk, gv, gr, la, dq, dqi, dkc, dvc, dki, dwi), _ = proj()
        lyr = past["layer"]
        keys_last = lambda new: jnp.swapaxes(seq(new), 1, 2)
        join_t = lambda old_t, new: _pad_last(jnp.concatenate([old_t, keys_last(new)], axis=2), l_pad)[None]
        join_r = lambda old, new: _pad_rows(jnp.concatenate([old, seq(new)], axis=1), l_pad)
        if p_len % tk == 0 and t <= tk:
            new_kv = tuple(_pad_last(keys_last(a), tk) for a in (sbk, sbv))
            o_a = _sb_call(seq(sbq), past["k_sb_t"], past["v_sb_t"], lyr, p_len, tq, tk, new_kv=new_kv)
        else:
            o_a = _sb_call(seq(sbq), join_t(past["k_sb_t"][lyr], sbk), join_t(past["v_sb_t"][lyr], sbv),
                           0, p_len, tq, tk)
        vc_t = join_t(past["v_dsa_t"], dvc)
        kc_rows, ki_rows = join_r(past["k_dsa"], dkc), join_r(past["k_idx"], dki)
        layer = 0
        new_rows = (seq(sbk).reshape(n_seq, t, N_HEADS, HEAD_DIM), seq(sbv).reshape(n_seq, t, N_HEADS, HEAD_DIM),
                    seq(dkc), seq(dvc), seq(dki))
    else:
        assert past is None and l_pad == t and t % tm == 0 and tm % LANE == 0
        layer, prev = stacked
        (sbq, gq, gk, gv, gr, la, dq, dqi, dkc, dki, dwi), new_rows = proj((layer, n_seq, prev))
        k_t, v_t, _, vc_t, _ = new_rows
        kc_rows, ki_rows = seq(dkc), seq(dki)
        o_a = _sb_call(seq(sbq), k_t, v_t, layer, p_len, tq, tk)

    c = CHUNK if t % CHUNK == 0 else t
    s0 = jnp.zeros((n_seq, HW, DV_B), F32) if past is None else past["gla"]
    o_b, s_new = _gla_call(seq(gq), seq(gk), seq(gv), seq(gr), seq(la), s0, w["g_gla"], c)

    t_c = t if t % LANE == 0 else -(-t // LANE) * LANE
    tq_c = _pick_tile(t_c, 256, LANE)
    qpad = lambda a: _pad_rows(seq(a), t_c)
    o_c = _dsa_call(qpad(dq), qpad(dqi), qpad(dwi), kc_rows, vc_t, ki_rows, layer, p_len, l_real, tq_c, tk)
    o_c = o_c if t_c == t else o_c[:, :t]

    flat = lambda a: a.reshape(rows, a.shape[-1])
    x = _ffn_call(x, mod, 6, w["gains"], 4, w["w_f2_in"], w["w_f2_out"], w["layer"], tm,
                  mix=(flat(o_a), flat(o_b), flat(o_c), w["w_out"]))
    return x, new_rows, s_new.reshape(n_seq, N_HEADS, HEAD_DIM, DV_B)


def kernel(x_prompt, x_sample, cache_k_sb, cache_v_sb, cache_k_dsa, cache_v_dsa, cache_k_idx, state_gla,
           c_prompt, c_sample, w_ada, b_ada, norm_gains, w_ffn1_in, w_ffn1_out, w_ffn2_in, w_ffn2_out,
           w_in, w_gla_gate, b_gla_gate, gla_norm, w_out):
    bp, tp, d = x_prompt.shape
    bs, ts, _ = x_sample.shape
    depth = w_ada.shape[0]
    p_len = cache_k_sb.shape[2]

    mods = _mod_call(jnp.concatenate([c_prompt, c_sample], axis=0), w_ada.astype(BF16), b_ada)
    xp = x_prompt.reshape(bp * tp, d)
    xs = x_sample.reshape(bs * ts, d)
    tm_p = _pick_tile(tp, 512, SUBLANE)
    ffn_w = [a.astype(BF16) for a in (w_ffn1_in, w_ffn1_out, w_ffn2_in, w_ffn2_out)]
    sb_cache_t = [jnp.transpose(c, (0, 1, 3, 4, 2)).reshape(depth, bs, HW, p_len) for c in (cache_k_sb, cache_v_sb)]
    shared_p, gla_p, acc_s = None, [], []
    for l in range(depth):
        w_gate = jnp.pad(w_gla_gate[l], ((0, LANE - GATE_RANK), (0, 0))).astype(BF16)
        w = dict(gains=norm_gains[l].reshape(-1, 1, d),
                 layer=l, w_f1_in=ffn_w[0], w_f1_out=ffn_w[1], w_f2_in=ffn_w[2], w_f2_out=ffn_w[3],
                 w_in=_relayout_w_in(w_in[l]).astype(BF16), w_gate=w_gate,
                 b_gate=b_gla_gate[l].reshape(1, HW), g_gla=gla_norm[l].reshape(1, N_HEADS * DV_B),
                 w_out=w_out[l].astype(BF16))
        mod_p = _Mod(mods[l, :bp].reshape(bp * N_MOD, 1, d), False, tp // tm_p)
        mod_s_rows = jnp.repeat(mods[l, bp:].reshape(bs, N_MOD, d), ts, axis=0)
        mod_s = _Mod(jnp.transpose(mod_s_rows, (1, 0, 2)), True)
        past = dict(layer=l, k_sb_t=sb_cache_t[0], v_sb_t=sb_cache_t[1],
                    k_dsa=cache_k_dsa[l], v_dsa_t=jnp.swapaxes(cache_v_dsa[l], 1, 2), k_idx=cache_k_idx[l],
                    gla=state_gla[l].reshape(bs, HW, DV_B))
        xp, shared_p, s_p = _encoder_layer(xp, bp, tp, mod_p, None, w, stacked=(l, shared_p))
        xs, rows_s, s_s = _encoder_layer(xs, bs, ts, mod_s, past, w)
        gla_p.append(s_p)
        acc_s.append(rows_s + (s_s,))
    k_t, v_t, kc_t, vc_t, ki_t = shared_p
    heads_out = lambda a: jnp.transpose(a.reshape(depth, bp, N_HEADS, HEAD_DIM, tp), (0, 1, 4, 2, 3))
    tokens_out = lambda a: jnp.swapaxes(a, 2, 3)
    field = lambda acc, i: jnp.stack([r[i] for r in acc], axis=0)
    return (xp.reshape(bp, tp, d), xs.reshape(bs, ts, d),
            heads_out(k_t), heads_out(v_t), tokens_out(kc_t), tokens_out(vc_t), tokens_out(ki_t),
            jnp.stack(gla_p, axis=0), *(field(acc_s, i) for i in range(6)))
```

```python
import functools
import math

import numpy as np
import jax
import jax.numpy as jnp
from jax import lax
from jax.experimental import pallas as pl
from jax.experimental.pallas import tpu as pltpu

F32 = jnp.float32
BF16 = jnp.bfloat16

CHUNK = 64
N_HEADS = 4
HEAD_DIM = 64
HW = N_HEADS * HEAD_DIM
DV_B = 128
GATE_RANK = 16
GATE_TAU = 16.0
TOPK_MAX = 256
EPS = 1e-6
MACARON_W = 0.5
N_MOD = 9

LANE = 128
SUBLANE = 8
VMEM_LIMIT_BYTES = 56 * 1024 * 1024

_NEG_INF = float("-inf")
_KEY_NEG_INF = -2**31 + 0x7FFFFF
_INT_MIN = -2**31


def _pick_tile(n, target, mult):
    if n <= target:
        return n
    t = (target // mult) * mult
    while t >= mult:
        if n % t == 0:
            return t
        t -= mult
    return n


def _cparams(n_axes):
    return pltpu.CompilerParams(dimension_semantics=("arbitrary",) * n_axes,
                                vmem_limit_bytes=VMEM_LIMIT_BYTES)


def _rms_rows(x, g):
    ms = jnp.mean(x * x, axis=-1, keepdims=True)
    return x * lax.rsqrt(ms + EPS) * g


def _silu(x):
    return x * (1.0 / (1.0 + jnp.exp(-x)))


def _log_sigmoid(x):
    return jnp.minimum(x, 0.0) - jnp.log(1.0 + jnp.exp(-jnp.abs(x)))


_LOG2_E = 1.4426950408889634


def _neg_abs(x):
    bits = lax.bitcast_convert_type(x, jnp.int32) | _INT_MIN
    return lax.bitcast_convert_type(bits, F32)


def _dot(a, b):
    return jnp.dot(a, b, preferred_element_type=F32)


def _dot_nt(a, b):
    return lax.dot_general(a, b, (((1,), (1,)), ((), ())), preferred_element_type=F32)


def _dot_tn(a, b):
    return lax.dot_general(a, b, (((0,), (0,)), ((), ())), preferred_element_type=F32)


def _fori_by_two(lo, hi, body, init):
    n = hi - lo

    def two(p, carry):
        j = lo + 2 * p
        return body(j + 1, body(j, carry))

    carry = lax.fori_loop(0, lax.shift_right_logical(n, 1), two, init)
    return lax.cond((n & 1) == 1, lambda c: body(hi - 1, c), lambda c: c, carry)


def _split_bf16(x):
    hi = x.astype(BF16)
    lo = (x - hi.astype(F32)).astype(BF16)
    return hi, lo


def _head_lane_id(width, per_head):
    return lax.broadcasted_iota(jnp.int32, (1, width), 1) // per_head


def _mod_kernel(c_ref, w_ref, b_ref, o_ref):
    a = _silu(c_ref[...]).astype(BF16)
    o_ref[...] = _dot(a, w_ref[...]) + b_ref[...]


def _mod_call(c, w_ada, b_ada):
    depth, d, nd = w_ada.shape
    n = c.shape[0]
    tn = _pick_tile(nd, 1536, LANE)
    return pl.pallas_call(
        _mod_kernel,
        grid=(depth, nd // tn),
        in_specs=[pl.BlockSpec((n, d), lambda l, j: (0, 0)),
                  pl.BlockSpec((None, d, tn), lambda l, j: (l, 0, j)),
                  pl.BlockSpec((None, 1, tn), lambda l, j: (l, 0, j))],
        out_specs=pl.BlockSpec((None, n, tn), lambda l, j: (l, 0, j)),
        out_shape=jax.ShapeDtypeStruct((depth, n, nd), F32),
        compiler_params=_cparams(2),
        name="adaln_mod",
    )(c, w_ada, b_ada.reshape(depth, 1, nd))


class _Mod:
    def __init__(self, arr, per_token, seq_len=None):
        self.arr, self.per_token, self.seq_len = arr, per_token, seq_len

    def spec(self, k, tm, d):
        if self.per_token:
            return pl.BlockSpec((None, tm, d), lambda i, *_: (k, i, 0))
        assert self.seq_len % tm == 0
        tps = self.seq_len // tm
        return pl.BlockSpec((None, 1, d), lambda i, *_: ((i // tps) * N_MOD + k, 0, 0))


def _gain_spec(k, d):
    return pl.BlockSpec((None, 1, d), lambda i, *_: (k, 0, 0))


def _mix_residual(x, oa_ref, ob_ref, oc_ref, ga_ref, g_ref, w_ref):
    wa = HW
    wb = wa + N_HEADS * DV_B
    y = (_dot(oa_ref[...].astype(BF16), w_ref[0:wa, :])
         + _dot(ob_ref[...].astype(BF16), w_ref[wa:wb, :])
         + _dot(oc_ref[...].astype(BF16), w_ref[wb:wb + HW, :]))
    return x + ga_ref[...] * _rms_rows(y, g_ref[...])


class _Rows:
    def __init__(self, ref, r0, n, full):
        self.ref, self.rows = ref, (slice(r0, r0 + n) if ref.shape[0] == full else slice(None))

    def __getitem__(self, _):
        return self.ref[self.rows, :]


def _ffn_kernel(x_ref, *refs, f, tf, with_mix, n_split):
    tm = x_ref.shape[0]
    o_ref = refs[-1]
    for r0 in range(0, tm, tm // n_split):
        view = lambda ref: _Rows(ref, r0, tm // n_split, tm)
        x = x_ref[r0:r0 + tm // n_split, :]
        rest = refs[:-1]
        if with_mix:
            oa, ob, oc, ga2, g3, w_mix = rest[:6]
            x = _mix_residual(x, view(oa), view(ob), view(oc), view(ga2), g3, w_mix)
            rest = rest[6:]
        sh_ref, sc_ref, ga_ref, gin_ref, gout_ref, wi_ref, wo_ref = rest
        h = (_rms_rows(x, gin_ref[...]) * (1.0 + view(sc_ref)[...]) + view(sh_ref)[...]).astype(BF16)
        y = None
        for c0 in range(0, f, tf):
            gate = _dot(h, wi_ref[:, c0:c0 + tf])
            up = _dot(h, wi_ref[:, f + c0:f + c0 + tf])
            part = _dot((_silu(gate) * up).astype(BF16), wo_ref[c0:c0 + tf, :])
            y = part if y is None else y + part
        o_ref[r0:r0 + tm // n_split, :] = x + MACARON_W * view(ga_ref)[...] * _rms_rows(y, gout_ref[...])


def _ffn_call(x, mod, mod_k, gains, gain_k, w_in, w_out, layer, tm, mix=None):
    rows, d = x.shape
    f = w_out.shape[1]
    tf = _pick_tile(f, 1408, LANE)
    row = lambda w: pl.BlockSpec((tm, w), lambda i: (i, 0))
    resident = lambda shape: pl.BlockSpec((None,) + shape, lambda i: (layer, 0, 0),
                                          pipeline_mode=pl.Buffered(1))
    in_specs, args = [row(d)], [x]
    if mix is not None:
        oa, ob, oc, w_mix = mix
        in_specs += [row(oa.shape[1]), row(ob.shape[1]), row(oc.shape[1]), mod.spec(5, tm, d), _gain_spec(3, d),
                     pl.BlockSpec(w_mix.shape, lambda i: (0, 0), pipeline_mode=pl.Buffered(1))]
        args += [oa, ob, oc, mod.arr, gains, w_mix]
    in_specs += [mod.spec(mod_k, tm, d), mod.spec(mod_k + 1, tm, d), mod.spec(mod_k + 2, tm, d),
                 _gain_spec(gain_k, d), _gain_spec(gain_k + 1, d),
                 resident((d, 2 * f)), resident((f, d))]
    args += [mod.arr, mod.arr, mod.arr, gains, gains, w_in, w_out]
    return pl.pallas_call(
        functools.partial(_ffn_kernel, f=f, tf=tf, with_mix=mix is not None,
                          n_split=2 if tm % (2 * SUBLANE) == 0 else 1),
        grid=(rows // tm,),
        in_specs=in_specs,
        out_specs=row(d),
        out_shape=jax.ShapeDtypeStruct((rows, d), F32),
        compiler_params=_cparams(1),
        name="macaron_ffn",
    )(*args)


_PROJ_GROUPS = (("sbq", HW), ("sbk", HW), ("sbv", HW),
                ("gq", HW), ("gk", HW), ("gv", N_HEADS * DV_B), ("gr", N_HEADS * DV_B),
                ("dq", HW), ("dqi", HW), ("dkv", LANE), ("dkiwg", LANE))
_PROJ_OFFSETS = tuple(int(v) for v in np.cumsum([0] + [w for _, w in _PROJ_GROUPS]))
_PROJ_WIDTH = _PROJ_OFFSETS[-1]
_WI_LANE = HEAD_DIM
_GB_LANE = _WI_LANE + N_HEADS
_GATE = "gate"
_PROJ_OUTPUTS = (("sbq", 0, HW), ("sbk", 0, HW), ("sbv", 0, HW), ("gq", 0, HW), ("gk", 0, HW),
                 ("gv", 0, N_HEADS * DV_B), ("gr", 0, N_HEADS * DV_B), (_GATE, 0, HW),
                 ("dq", 0, HW), ("dqi", 0, HW), ("dkv", 0, HEAD_DIM), ("dkv", HEAD_DIM, HEAD_DIM),
                 ("dkiwg", 0, HEAD_DIM), ("dkiwg", 0, LANE))
_PROJ_ROW_OUTPUTS = (("sbq", 0, HW), ("gq", 0, HW), ("gk", 0, HW), ("gv", 0, N_HEADS * DV_B),
                     ("gr", 0, N_HEADS * DV_B), (_GATE, 0, HW), ("dq", 0, HW), ("dqi", 0, HW),
                     ("dkv", 0, HEAD_DIM), ("dkiwg", 0, HEAD_DIM), ("dkiwg", 0, LANE))
_PROJ_T_OUTPUTS = (("sbk", 0, HW), ("sbv", 0, HW), ("dkv", 0, HEAD_DIM), ("dkv", HEAD_DIM, HEAD_DIM),
                   ("dkiwg", 0, HEAD_DIM))


def _relayout_w_in(w_in):
    d = w_in.shape[0]
    widths = (HW, HW, HW, HW, HW, N_HEADS * DV_B, N_HEADS * DV_B, GATE_RANK,
              HW, HEAD_DIM, HEAD_DIM, HW, HEAD_DIM, N_HEADS)
    offs = np.cumsum((0,) + widths)
    col = lambda i: w_in[:, offs[i]:offs[i + 1]]
    pad = lambda a, w: jnp.pad(a, ((0, 0), (0, w - a.shape[1])))
    parts = [col(0), col(1), col(2), col(3), col(4), col(5), col(6),
             col(8), col(11), jnp.concatenate([col(9), col(10)], axis=1),
             pad(jnp.concatenate([col(12), col(13), col(7)], axis=1), LANE)]
    out = jnp.concatenate(parts, axis=1)
    assert out.shape == (d, _PROJ_WIDTH)
    return out


def _proj_kernel(x_ref, sh_ref, sc_ref, g_ref, w_ref, wgate_ref, bgate_ref, *refs, row_outs, t_outs, n_prev):
    prev_refs = refs[:n_prev]
    row_refs = refs[n_prev:n_prev + len(row_outs)]
    t_refs = refs[n_prev + len(row_outs):]
    for o_ref, prev_ref in zip(t_refs, prev_refs):
        o_ref[:prev_ref.shape[0]] = prev_ref[...]
    h = (_rms_rows(x_ref[...], g_ref[...]) * (1.0 + sc_ref[...]) + sh_ref[...]).astype(BF16)
    for gi, (name, _) in enumerate(_PROJ_GROUPS):
        p = _dot(h, w_ref[:, _PROJ_OFFSETS[gi]:_PROJ_OFFSETS[gi + 1]])
        groups = {name: p}
        if name == "dkiwg":
            pre = _dot(p.astype(BF16), wgate_ref[...]) + bgate_ref[...]
            groups[_GATE] = _log_sigmoid(pre) * (1.0 / GATE_TAU)
        for o_ref, (out_group, start, width) in zip(row_refs, row_outs):
            if out_group in groups:
                o_ref[...] = groups[out_group][:, start:start + width]
        for o_ref, (out_group, start, width) in zip(t_refs, t_outs):
            if out_group in groups:
                o_ref[o_ref.shape[0] - 1] = groups[out_group].T[start:start + width, :]


def _proj_call(x, mod, gains, w_in, w_gate, b_gate, tm, stacked=None):
    rows, d = x.shape
    in_specs = [pl.BlockSpec((tm, d), lambda i: (i, 0)),
                mod.spec(3, tm, d), mod.spec(4, tm, d), _gain_spec(2, d),
                pl.BlockSpec((d, _PROJ_WIDTH), lambda i: (0, 0)),
                pl.BlockSpec((LANE, HW), lambda i: (0, 0)),
                pl.BlockSpec((1, HW), lambda i: (0, 0))]
    args = [x, mod.arr, mod.arr, gains, w_in, w_gate, b_gate]
    if stacked is None:
        row_outs, t_outs, prev = _PROJ_OUTPUTS, (), ()
        t_specs, t_shapes = [], []
    else:
        layer, n_seq, prev = stacked
        row_outs, t_outs = _PROJ_ROW_OUTPUTS, _PROJ_T_OUTPUTS
        t = rows // n_seq
        tps = t // tm
        tile = lambda n, w: pl.BlockSpec((n, None, w, tm), lambda i: (0, i // tps, 0, i % tps))
        t_specs = [tile(layer + 1, w) for _, _, w in t_outs]
        t_shapes = [jax.ShapeDtypeStruct((layer + 1, n_seq, w, t), F32) for _, _, w in t_outs]
        prev = () if prev is None else tuple(prev)
        in_specs += [tile(layer, w) for _, _, w in t_outs[:len(prev)]]
    outs = pl.pallas_call(
        functools.partial(_proj_kernel, row_outs=row_outs, t_outs=t_outs, n_prev=len(prev)),
        grid=(rows // tm,),
        in_specs=in_specs,
        out_specs=[pl.BlockSpec((tm, w), lambda i: (i, 0)) for _, _, w in row_outs] + t_specs,
        out_shape=[jax.ShapeDtypeStruct((rows, w), F32) for _, _, w in row_outs] + t_shapes,
        compiler_params=_cparams(1),
        name="mix_in_proj",
    )(*args, *prev)
    return outs[:len(row_outs)], outs[len(row_outs):]


def _sb_kernel(q_ref, k_ref, v_ref, *rest, tq, tk, past, new_apart):
    if new_apart:
        kn_ref, vn_ref, u_ref, o_ref, acc_scr, c_scr = rest
    else:
        u_ref, o_ref, acc_scr, c_scr = rest
    i = pl.program_id(1)
    head = _head_lane_id(HW, HEAD_DIM)
    q = q_ref[...] * (HEAD_DIM ** -0.5)
    q_st = jnp.concatenate([jnp.where(head == h, q, 0.0) for h in range(N_HEADS)], axis=0).astype(BF16)
    qpos = past + i * tq + lax.broadcasted_iota(jnp.int32, (tq, 1), 0)
    last_blk = (past + (i + 1) * tq - 2) // tk
    acc_scr[...] = jnp.zeros_like(acc_scr)
    c_scr[...] = jnp.zeros_like(c_scr)
    u = u_ref[...]
    rows = N_HEADS * tq

    heads_per_tile = LANE // HEAD_DIM
    low_half = [(lax.broadcasted_iota(jnp.int32, (1, LANE), 1) // HEAD_DIM) == r
                for r in range(heads_per_tile)]

    def body(n, carry, *, masked):
        j = last_blk - n
        k0 = pl.multiple_of(j * tk, tk)
        if new_apart and masked:
            cols = pl.ds(pl.multiple_of(k0 - past, tk), tk)
            kb, vb = kn_ref[:, cols].astype(BF16), vn_ref[:, cols].astype(BF16)
        else:
            kb = k_ref[:, pl.ds(k0, tk)].astype(BF16)
            vb = v_ref[:, pl.ds(k0, tk)].astype(BF16)
        z = _dot(q_st, kb).reshape(N_HEADS, tq, tk) * _LOG2_E
        soft = jnp.log2(1.0 + jnp.exp2(_neg_abs(z)))
        log_beta = jnp.minimum(z, 0.0) - soft
        log_keep = log_beta - z
        if masked:
            kpos = k0 + lax.broadcasted_iota(jnp.int32, (1, tk), 1)
            valid = (kpos < qpos)[None]
            log_keep = jnp.where(valid, log_keep, 0.0)
        newer = _dot(log_keep.astype(BF16).reshape(rows, tk), u).reshape(N_HEADS, tq, tk)
        c = c_scr[...]
        w = jnp.exp2(log_beta + newer + c)
        if masked:
            w = jnp.where(valid, w, 0.0)
        pv = _dot_nt(w.astype(BF16).reshape(rows, tk), vb).reshape(N_HEADS, tq, HW)
        tiles = []
        for t0 in range(HW // LANE):
            sl = slice(t0 * LANE, (t0 + 1) * LANE)
            col = pv[t0 * heads_per_tile][:, sl]
            for r in range(1, heads_per_tile):
                col = jnp.where(low_half[r], pv[t0 * heads_per_tile + r][:, sl], col)
            tiles.append(col)
        acc_scr[...] += jnp.concatenate(tiles, axis=1)
        c_scr[...] = c + newer[:, :, 0:1] + log_keep[:, :, 0:1]
        return carry

    n_masked = last_blk + 1 - jnp.minimum((past + i * tq) // tk, last_blk + 1)
    lax.fori_loop(0, n_masked, functools.partial(body, masked=True), 0)
    _fori_by_two(n_masked, last_blk + 1, functools.partial(body, masked=False), 0)
    o_ref[...] = acc_scr[...]


def _sb_call(q, k_t, v_t, layer, past, tq, tk, new_kv=None):
    b, t, _ = q.shape
    u = (np.arange(tk)[:, None] > np.arange(tk)[None, :]).astype(np.float32)
    kv_spec = pl.BlockSpec((None, None, HW, k_t.shape[3]), lambda bi, i: (layer, bi, 0, 0))
    in_specs = [pl.BlockSpec((None, tq, HW), lambda bi, i: (bi, i, 0)), kv_spec, kv_spec]
    args = [q, k_t, v_t]
    if new_kv is not None:
        assert past % tk == 0 and k_t.shape[3] == past and t <= tk
        in_specs += [pl.BlockSpec((None, HW, new_kv[0].shape[2]), lambda bi, i: (bi, 0, 0))] * 2
        args += list(new_kv)
    return pl.pallas_call(
        functools.partial(_sb_kernel, tq=tq, tk=tk, past=past, new_apart=new_kv is not None),
        grid=(b, t // tq),
        in_specs=in_specs + [pl.BlockSpec((tk, tk), lambda bi, i: (0, 0))],
        out_specs=pl.BlockSpec((None, tq, HW), lambda bi, i: (bi, i, 0)),
        out_shape=jax.ShapeDtypeStruct((b, t, HW), F32),
        scratch_shapes=[pltpu.VMEM((tq, HW), F32), pltpu.VMEM((N_HEADS, tq, 1), F32)],
        compiler_params=_cparams(2),
        name="stick_breaking_attn",
    )(*args, jnp.asarray(u, BF16))


def _gla_tables(c):
    n_lev = int(math.log2(c))
    assert 1 << n_lev == c
    t = np.arange(c)
    rows = []
    for lev in range(1, n_lev + 1):
        blk = c >> (lev - 1)
        ref = (t // blk) * blk + blk // 2 - 1
        lo, hi = np.minimum(t, ref), np.maximum(t, ref)
        rows.append(((t[None, :] > lo[:, None]) & (t[None, :] <= hi[:, None])).astype(np.float32))
    rows.append((t[None, :] <= t[:, None]).astype(np.float32))
    rows.append((t[None, :] > t[:, None]).astype(np.float32))
    table = np.concatenate(rows, axis=0)
    lvl = np.full((c, c), n_lev + 1, np.int32)
    for lev in range(1, n_lev + 1):
        blk = c >> (lev - 1)
        half = blk // 2
        same = (t[:, None] // blk) == (t[None, :] // blk)
        pair = same & ((t[:, None] % blk) >= half) & ((t[None, :] % blk) < half)
        lvl[pair] = lev
    lvl[t, t] = 0
    return np.concatenate([table, table], axis=1), np.tile(lvl.T, (1, N_HEADS)), n_lev


def _gla_kernel(q_seqs, k_seqs, v_seqs, r_seqs, la_seqs, s0_ref, tab_ref, lvl_ref, g_ref,
                o_seqs, s_ref, s_seqs, *, c, n_lev, n_chunks, n_steps):
    j = pl.program_id(1)

    @pl.when(j == 0)
    def _():
        s_seqs[...] = s0_ref[...]

    head = _head_lane_id(HW, HEAD_DIM)
    tab = tab_ref[...]
    lvl = lvl_ref[...]

    def chunk(ci, carry):
        for bi in range(q_seqs.shape[0]):
            seq_chunk(bi, pl.ds(pl.multiple_of(ci * c, c), c))
        return carry

    def seq_chunk(bi, rows):
        q_ref, k_ref, v_ref, r_ref, la_ref, o_ref, s_scr = (
            a.at[bi] for a in (q_seqs, k_seqs, v_seqs, r_seqs, la_seqs, o_seqs, s_seqs))
        q = q_ref[rows, :] * (HEAD_DIM ** -0.5)
        k = k_ref[rows, :]
        decays = jnp.exp(_dot(tab, jnp.concatenate(_split_bf16(la_ref[rows, :]), axis=0)))
        by_head = lambda a: jnp.concatenate([jnp.where(head == h, a, 0.0) for h in range(N_HEADS)],
                                            axis=0).astype(BF16)
        att_t = jnp.zeros((c, N_HEADS * c), F32)
        for lev in range(n_lev + 1):
            if lev == 0:
                ql, kl = q, k
            else:
                e = decays[(lev - 1) * c:lev * c]
                ql, kl = q * e, k * e
            a_t = _dot_nt(kl.astype(BF16), by_head(ql))
            att_t = jnp.where(lvl == lev, a_t, att_t)
        q_decay = decays[n_lev * c:(n_lev + 1) * c]
        k_out = (k * decays[(n_lev + 1) * c:(n_lev + 2) * c]).astype(BF16)
        s = s_scr[...]
        v_all = v_ref[rows, :].astype(BF16)
        intra = _dot_tn(att_t.astype(BF16), v_all)
        inter = _dot(by_head(q * q_decay), s.astype(BF16))
        for h in range(N_HEADS):
            o = intra[h * c:(h + 1) * c, h * DV_B:(h + 1) * DV_B] + inter[h * c:(h + 1) * c]
            y = _rms_rows(o, g_ref[:, h * DV_B:(h + 1) * DV_B])
            o_ref[rows, h * DV_B:(h + 1) * DV_B] = y * _silu(r_ref[rows, h * DV_B:(h + 1) * DV_B])
        kv = _dot_tn(k_out, v_all)
        ds = jnp.concatenate([kv[h * HEAD_DIM:(h + 1) * HEAD_DIM, h * DV_B:(h + 1) * DV_B]
                              for h in range(N_HEADS)], axis=0)
        chunk_decay = jnp.broadcast_to(q_decay[c - 1:c, :], (DV_B, HW)).T
        s_scr[...] = chunk_decay * s + ds

    lax.fori_loop(0, n_chunks, chunk, 0)

    @pl.when(j == n_steps - 1)
    def _():
        s_ref[...] = s_seqs[...]


def _gla_call(q, k, v, r, la, s0, g_gla, c):
    b, t, _ = q.shape
    vw = N_HEADS * DV_B
    tc = _pick_tile(t, 512, c)
    n_steps = t // tc
    nb = max(n for n in (4, 2, 1) if b % n == 0)
    table, lvl, n_lev = _gla_tables(c)
    qk_spec = pl.BlockSpec((nb, tc, HW), lambda bi, j: (bi, j, 0))
    vr_spec = pl.BlockSpec((nb, tc, vw), lambda bi, j: (bi, j, 0))
    st_spec = pl.BlockSpec((nb, HW, DV_B), lambda bi, j: (bi, 0, 0))
    const = lambda shape: pl.BlockSpec(shape, lambda bi, j: (0, 0))
    return pl.pallas_call(
        functools.partial(_gla_kernel, c=c, n_lev=n_lev, n_chunks=tc // c, n_steps=n_steps),
        grid=(b // nb, n_steps),
        in_specs=[qk_spec, qk_spec, vr_spec, vr_spec, qk_spec, st_spec,
                  const(table.shape), const(lvl.shape), const((1, vw))],
        out_specs=[vr_spec, st_spec],
        out_shape=[jax.ShapeDtypeStruct((b, t, vw), F32), jax.ShapeDtypeStruct((b, HW, DV_B), F32)],
        scratch_shapes=[pltpu.VMEM((nb, HW, DV_B), F32)],
        compiler_params=_cparams(2),
        name="gated_linear_attn",
    )(q, k, v, r, la, s0, jnp.asarray(table, BF16), jnp.asarray(lvl), g_gla)


_COUNT_ROWS = 32


def _key_to_f32(key):
    key = jnp.maximum(key, _KEY_NEG_INF)
    bits = jnp.where(key < 0, key ^ 0x7FFFFFFF, key)
    return lax.bitcast_convert_type(bits, F32)


def _dsa_kernel(q_ref, qi_ref, wi_ref, kc_ref, vc_ref, ki_ref, eye_ref, tri_ref, o_ref,
                sc_scr, hi_scr, lo_scr, lg_scr, vct_scr, acc_scr, *, tq, tk, past, l_real, n_sel):
    i = pl.program_id(1)
    scale = HEAD_DIM ** -0.5
    eye = eye_ref[...]
    l_pad = sc_scr.shape[0]

    @pl.when(i == 0)
    def _():
        vct_scr[...] = vc_ref[...].astype(BF16)

    q_t = _dot_nt(eye, (q_ref[...] * scale).astype(BF16)).astype(BF16)
    qi_t = _dot_nt(eye, (qi_ref[...] * scale).astype(BF16)).astype(BF16)
    w_t = wi_ref[...].T * (N_HEADS ** -0.5)
    qpos = past + i * tq + lax.broadcasted_iota(jnp.int32, (1, tq), 1)
    chunk_shift = int(math.log2(CHUNK))
    qchunk = lax.shift_right_logical(qpos, chunk_shift)
    k_end = ((past + (i + 1) * tq - 1) // CHUNK + 1) * CHUNK
    n_blk = jnp.minimum((k_end + tk - 1) // tk, l_pad // tk)

    def rows_of(j):
        return pl.ds(pl.multiple_of(j * tk, tk), tk)

    def admissible(j):
        kpos = j * tk + lax.broadcasted_iota(jnp.int32, (tk, 1), 0)
        return (lax.shift_right_logical(kpos, chunk_shift) <= qchunk) & (kpos < l_real)

    def score_blk(j, carry):
        kib = ki_ref[rows_of(j), :].astype(BF16)
        s = jnp.zeros((tk, tq), F32)
        for h in range(N_HEADS):
            sh = _dot(kib, qi_t[h * HEAD_DIM:(h + 1) * HEAD_DIM])
            s = s + w_t[_WI_LANE + h:_WI_LANE + h + 1] * jnp.maximum(sh, 0.0)
        s = jnp.where(admissible(j), jnp.where(s == 0.0, 0.0, s), _NEG_INF)
        sc_scr[rows_of(j), :] = s
        bits = lax.bitcast_convert_type(s, jnp.int32)
        key = jnp.where(bits < 0, bits ^ 0x7FFFFFFF, bits)
        hi_scr[rows_of(j), :] = lax.shift_right_arithmetic(key, 16).astype(jnp.int16)
        lo_scr[rows_of(j), :] = ((key & 0xFFFF) - 32768).astype(jnp.int16)
        return carry

    _fori_by_two(0, n_blk, score_blk, 0)

    def count(ref, pred, one, zero):
        def blk(j, acc):
            m = jnp.where(pred(ref[rows_of(j), :]), one, zero).reshape(tk // _COUNT_ROWS, _COUNT_ROWS, tq)
            for r in range(tk // _COUNT_ROWS):
                acc = acc + m[r]
            return acc
        acc = _fori_by_two(0, n_blk, blk, jnp.zeros((_COUNT_ROWS, tq), one.dtype))
        return jnp.sum(acc.astype(jnp.int32), axis=0, keepdims=True)

    one16, zero16 = jnp.int16(1), jnp.int16(0)

    def bisect16(ref, want):
        def bit(it, tau):
            cand = tau + lax.shift_left(jnp.int32(1), 15 - it)
            c16 = cand.astype(jnp.int16)
            return jnp.where(count(ref, lambda a: a >= c16, one16, zero16) >= want, cand, tau)
        return lax.fori_loop(0, 16, bit, jnp.full((1, tq), -32768, jnp.int32))

    hi_star = bisect16(hi_scr, n_sel)
    h16 = hi_star.astype(jnp.int16)
    want_lo = n_sel - count(hi_scr, lambda a: a > h16, one16, zero16)

    def mask_lo(j, carry):
        lo_scr[rows_of(j), :] = jnp.where(hi_scr[rows_of(j), :] == h16, lo_scr[rows_of(j), :],
                                          jnp.int16(-32768))
        return carry

    lax.fori_loop(0, n_blk, mask_lo, 0)
    lo_star = bisect16(lo_scr, want_lo)
    thr = _key_to_f32(lax.shift_left(hi_star, 16) + (lo_star + 32768))
    need = (n_sel - count(sc_scr, lambda a: a > thr, jnp.float32(1.0), jnp.float32(0.0))).astype(F32)

    tri = tri_ref[...]
    fold = lambda a, op: op(a.reshape(tk // SUBLANE, SUBLANE, tq), axis=0)
    parts = lambda v: tuple(jnp.full((SUBLANE, tq), v, F32) for _ in range(N_HEADS))

    def logit_blk(j, carry):
        m_parts, ties_before = carry
        s = sc_scr[rows_of(j), :]
        tie = s == thr
        rank = _dot(tri, jnp.where(tie, 1.0, 0.0).astype(BF16)) + ties_before
        sel = admissible(j) & ((s > thr) | (tie & (rank <= need)))
        bias = jnp.where(sel, 0.0, _NEG_INF)
        kcb = kc_ref[rows_of(j), :].astype(BF16)
        new_parts = []
        for h in range(N_HEADS):
            lg = _dot(kcb, q_t[h * HEAD_DIM:(h + 1) * HEAD_DIM]) + bias
            lg_scr[h, rows_of(j), :] = lg
            new_parts.append(jnp.maximum(m_parts[h], fold(lg, jnp.max)))
        return tuple(new_parts), rank[tk - 1:tk, :]

    m_parts, _ = _fori_by_two(0, n_blk, logit_blk, (parts(_NEG_INF), jnp.zeros((1, tq), F32)))
    m_use = []
    for h in range(N_HEADS):
        m = jnp.max(m_parts[h], axis=0, keepdims=True)
        m_use.append(jnp.where(m == _NEG_INF, 0.0, m))
    acc_scr[...] = jnp.zeros_like(acc_scr)

    def pv_blk(j, l_parts):
        vct = vct_scr[:, rows_of(j)]
        new_parts = []
        for h in range(N_HEADS):
            hd = slice(h * HEAD_DIM, (h + 1) * HEAD_DIM)
            p = jnp.exp(lg_scr[h, rows_of(j), :] - m_use[h])
            new_parts.append(l_parts[h] + fold(p, jnp.sum))
            acc_scr[hd, :] += _dot(vct, p.astype(BF16))
        return tuple(new_parts)

    l_parts = _fori_by_two(0, n_blk, pv_blk, parts(0.0))
    out_t = jnp.concatenate(
        [acc_scr[h * HEAD_DIM:(h + 1) * HEAD_DIM, :] / jnp.sum(l_parts[h], axis=0, keepdims=True)
         for h in range(N_HEADS)], axis=0)
    o_ref[...] = out_t.T


def _dsa_call(q, qi, wi, kc, vc_t, ki, layer, past, l_real, tq, tk):
    b, t, _ = q.shape
    l_pad = kc.shape[1]
    n_sel = min(TOPK_MAX, l_real // 4)
    assert tk >= n_sel and l_pad % tk == 0 and tk % _COUNT_ROWS == 0
    row = lambda w: pl.BlockSpec((None, tq, w), lambda bi, i: (bi, i, 0))
    full = pl.BlockSpec((None, l_pad, HEAD_DIM), lambda bi, i: (bi, 0, 0))
    full_t = pl.BlockSpec((None, None, HEAD_DIM, l_pad), lambda bi, i: (layer, bi, 0, 0))
    const = lambda n: pl.BlockSpec((n, n), lambda bi, i: (0, 0))
    eye = np.eye(HW, dtype=np.float32)
    tri = (np.arange(tk)[:, None] >= np.arange(tk)[None, :]).astype(np.float32)
    return pl.pallas_call(
        functools.partial(_dsa_kernel, tq=tq, tk=tk, past=past, l_real=l_real, n_sel=n_sel),
        grid=(b, t // tq),
        in_specs=[row(HW), row(HW), row(LANE), full, full_t, full, const(HW), const(tk)],
        out_specs=row(HW),
        out_shape=jax.ShapeDtypeStruct((b, t, HW), F32),
        scratch_shapes=[pltpu.VMEM((l_pad, tq), F32),
                        pltpu.VMEM((l_pad, tq), jnp.int16), pltpu.VMEM((l_pad, tq), jnp.int16),
                        pltpu.VMEM((N_HEADS, l_pad, tq), F32),
                        pltpu.VMEM((HEAD_DIM, l_pad), BF16), pltpu.VMEM((HW, tq), F32)],
        compiler_params=_cparams(2),
        name="indexer_sparse_attn",
    )(q, qi, wi, kc, vc_t, ki, jnp.asarray(eye, BF16), jnp.asarray(tri, BF16))


def _pad_rows(a, n):
    return a if n == a.shape[1] else jnp.pad(a, ((0, 0), (0, n - a.shape[1]), (0, 0)))


def _pad_last(a, n):
    return a if n == a.shape[-1] else jnp.pad(a, [(0, 0)] * (a.ndim - 1) + [(0, n - a.shape[-1])])


def _encoder_layer(x, n_seq, t, mod, past, w, stacked=None):
    rows, d = x.shape
    tm = _pick_tile(t, 512, SUBLANE) if not mod.per_token else rows
    tm_ffn = _pick_tile(t, 1024, SUBLANE) if not mod.per_token else rows
    x = _ffn_call(x, mod, 0, w["gains"], 0, w["w_f1_in"], w["w_f1_out"], w["layer"], tm_ffn)
    seq = lambda a: a.reshape(n_seq, t, a.shape[-1])
    p_len = 0 if past is None else past["k_sb_t"].shape[3]
    l_real = p_len + t
    tk = 256 if l_real >= 256 else LANE
    l_pad = -(-l_real // tk) * tk
    tq = _pick_tile(t, 256, SUBLANE)
    proj = functools.partial(_proj_call, x, mod, w["gains"], w["w_in"], w["w_gate"], w["b_gate"], tm)
    if stacked is None:
        (sbq, sbk, sbv, gq, gk, gv, gr, la, dq, dqi, dkc, dvc, dki, dwi), _ = proj()
        lyr = past["layer"]
        keys_last = lambda new: jnp.swapaxes(seq(new), 1, 2)
        join_t = lambda old_t, new: _pad_last(jnp.concatenate([old_t, keys_last(new)], axis=2), l_pad)[None]
        join_r = lambda old, new: _pad_rows(jnp.concatenate([old, seq(new)], axis=1), l_pad)
        if p_len % tk == 0 and t <= tk:
            new_kv = tuple(_pad_last(keys_last(a), tk) for a in (sbk, sbv))
            o_a = _sb_call(seq(sbq), past["k_sb_t"], past["v_sb_t"], lyr, p_len, tq, tk, new_kv=new_kv)
        else:
            o_a = _sb_call(seq(sbq), join_t(past["k_sb_t"][lyr], sbk), join_t(past["v_sb_t"][lyr], sbv),
                           0, p_len, tq, tk)
        vc_t = join_t(past["v_dsa_t"], dvc)
        kc_rows, ki_rows = join_r(past["k_dsa"], dkc), join_r(past["k_idx"], dki)
        layer = 0
        new_rows = (seq(sbk).reshape(n_seq, t, N_HEADS, HEAD_DIM), seq(sbv).reshape(n_seq, t, N_HEADS, HEAD_DIM),
                    seq(dkc), seq(dvc), seq(dki))
    else:
        assert past is None and l_pad == t and t % tm == 0 and tm % LANE == 0
        layer, prev = stacked
        (sbq, gq, gk, gv, gr, la, dq, dqi, dkc, dki, dwi), new_rows = proj((layer, n_seq, prev))
        k_t, v_t, _, vc_t, _ = new_rows
        kc_rows, ki_rows = seq(dkc), seq(dki)
        o_a = _sb_call(seq(sbq), k_t, v_t, layer, p_len, tq, tk)

    c = CHUNK if t % CHUNK == 0 else t
    s0 = jnp.zeros((n_seq, HW, DV_B), F32) if past is None else past["gla"]
    o_b, s_new = _gla_call(seq(gq), seq(gk), seq(gv), seq(gr), seq(la), s0, w["g_gla"], c)

    t_c = t if t % LANE == 0 else -(-t // LANE) * LANE
    tq_c = _pick_tile(t_c, 256, LANE)
    qpad = lambda a: _pad_rows(seq(a), t_c)
    o_c = _dsa_call(qpad(dq), qpad(dqi), qpad(dwi), kc_rows, vc_t, ki_rows, layer, p_len, l_real, tq_c, tk)
    o_c = o_c if t_c == t else o_c[:, :t]

    flat = lambda a: a.reshape(rows, a.shape[-1])
    x = _ffn_call(x, mod, 6, w["gains"], 4, w["w_f2_in"], w["w_f2_out"], w["layer"], tm_ffn,
                  mix=(flat(o_a), flat(o_b), flat(o_c), w["w_out"]))
    return x, new_rows, s_new.reshape(n_seq, N_HEADS, HEAD_DIM, DV_B)


def kernel(x_prompt, x_sample, cache_k_sb, cache_v_sb, cache_k_dsa, cache_v_dsa, cache_k_idx, state_gla,
           c_prompt, c_sample, w_ada, b_ada, norm_gains, w_ffn1_in, w_ffn1_out, w_ffn2_in, w_ffn2_out,
           w_in, w_gla_gate, b_gla_gate, gla_norm, w_out):
    bp, tp, d = x_prompt.shape
    bs, ts, _ = x_sample.shape
    depth = w_ada.shape[0]
    p_len = cache_k_sb.shape[2]

    mods = _mod_call(jnp.concatenate([c_prompt, c_sample], axis=0), w_ada.astype(BF16), b_ada)
    xp = x_prompt.reshape(bp * tp, d)
    xs = x_sample.reshape(bs * ts, d)
    ffn_w = [a.astype(BF16) for a in (w_ffn1_in, w_ffn1_out, w_ffn2_in, w_ffn2_out)]
    sb_cache_t = [jnp.transpose(c, (0, 1, 3, 4, 2)).reshape(depth, bs, HW, p_len) for c in (cache_k_sb, cache_v_sb)]
    shared_p, gla_p, acc_s = None, [], []
    for l in range(depth):
        w_gate = jnp.pad(w_gla_gate[l], ((_GB_LANE, LANE - GATE_RANK - _GB_LANE), (0, 0))).astype(BF16)
        w = dict(gains=norm_gains[l].reshape(-1, 1, d),
                 layer=l, w_f1_in=ffn_w[0], w_f1_out=ffn_w[1], w_f2_in=ffn_w[2], w_f2_out=ffn_w[3],
                 w_in=_relayout_w_in(w_in[l]).astype(BF16), w_gate=w_gate,
                 b_gate=b_gla_gate[l].reshape(1, HW), g_gla=gla_norm[l].reshape(1, N_HEADS * DV_B),
                 w_out=w_out[l].astype(BF16))
        mod_p = _Mod(mods[l, :bp].reshape(bp * N_MOD, 1, d), False, tp)
        mod_s_rows = jnp.repeat(mods[l, bp:].reshape(bs, N_MOD, d), ts, axis=0)
        mod_s = _Mod(jnp.transpose(mod_s_rows, (1, 0, 2)), True)
        past = dict(layer=l, k_sb_t=sb_cache_t[0], v_sb_t=sb_cache_t[1],
                    k_dsa=cache_k_dsa[l], v_dsa_t=jnp.swapaxes(cache_v_dsa[l], 1, 2), k_idx=cache_k_idx[l],
                    gla=state_gla[l].reshape(bs, HW, DV_B))
        xp, shared_p, s_p = _encoder_layer(xp, bp, tp, mod_p, None, w, stacked=(l, shared_p))
        xs, rows_s, s_s = _encoder_layer(xs, bs, ts, mod_s, past, w)
        gla_p.append(s_p)
        acc_s.append(rows_s + (s_s,))
    k_t, v_t, kc_t, vc_t, ki_t = shared_p
    heads_out = lambda a: jnp.transpose(a.reshape(depth, bp, N_HEADS, HEAD_DIM, tp), (0, 1, 4, 2, 3))
    tokens_out = lambda a: jnp.swapaxes(a, 2, 3)
    field = lambda acc, i: jnp.stack([r[i] for r in acc], axis=0)
    return (xp.reshape(bp, tp, d), xs.reshape(bs, ts, d),
            heads_out(k_t), heads_out(v_t), tokens_out(kc_t), tokens_out(vc_t), tokens_out(ki_t),
            jnp.stack(gla_p, axis=0), *(field(acc_s, i) for i in range(6)))
```

```python
import functools
import math

import numpy as np
import jax
import jax.numpy as jnp
from jax import lax
from jax.experimental import pallas as pl
from jax.experimental.pallas import tpu as pltpu

F32 = jnp.float32
BF16 = jnp.bfloat16

CHUNK = 64
N_HEADS = 4
HEAD_DIM = 64
HW = N_HEADS * HEAD_DIM
DV_B = 128
GATE_RANK = 16
GATE_TAU = 16.0
TOPK_MAX = 256
EPS = 1e-6
MACARON_W = 0.5
N_MOD = 9

LANE = 128
SUBLANE = 8
VMEM_BYTES = 64 * 1024 * 1024
VMEM_LIMIT_BYTES = VMEM_BYTES * 7 // 8

_NEG_INF = float("-inf")
_KEY_NEG_INF = -2**31 + 0x7FFFFF
_INT_MIN = -2**31


def _pick_tile(n, target, mult):
    if n <= target:
        return n
    t = (target // mult) * mult
    while t >= mult:
        if n % t == 0:
            return t
        t -= mult
    return n


def _cparams(n_axes):
    return pltpu.CompilerParams(dimension_semantics=("arbitrary",) * n_axes,
                                vmem_limit_bytes=VMEM_LIMIT_BYTES)


def _rms_rows(x, g):
    ms = jnp.mean(x * x, axis=-1, keepdims=True)
    return x * lax.rsqrt(ms + EPS) * g


def _silu(x):
    return x * (1.0 / (1.0 + jnp.exp(-x)))


def _log_sigmoid(x):
    return jnp.minimum(x, 0.0) - jnp.log(1.0 + jnp.exp(-jnp.abs(x)))


_LOG2_E = 1.4426950408889634


def _neg_abs(x):
    bits = lax.bitcast_convert_type(x, jnp.int32) | _INT_MIN
    return lax.bitcast_convert_type(bits, F32)


def _dot(a, b):
    return jnp.dot(a, b, preferred_element_type=F32)


def _dot_nt(a, b):
    return lax.dot_general(a, b, (((1,), (1,)), ((), ())), preferred_element_type=F32)


def _dot_tn(a, b):
    return lax.dot_general(a, b, (((0,), (0,)), ((), ())), preferred_element_type=F32)


def _fori_by_two(lo, hi, body, init):
    n = hi - lo

    def two(p, carry):
        j = lo + 2 * p
        return body(j + 1, body(j, carry))

    carry = lax.fori_loop(0, lax.shift_right_logical(n, 1), two, init)
    return lax.cond((n & 1) == 1, lambda c: body(hi - 1, c), lambda c: c, carry)


def _split_bf16(x):
    hi = x.astype(BF16)
    lo = (x - hi.astype(F32)).astype(BF16)
    return hi, lo


def _head_lane_id(width, per_head):
    return lax.broadcasted_iota(jnp.int32, (1, width), 1) // per_head


def _mod_kernel(c_ref, w_ref, b_ref, o_ref):
    a = _silu(c_ref[...]).astype(BF16)
    o_ref[...] = _dot(a, w_ref[...].astype(BF16)) + b_ref[...]


def _mod_call(c, w_ada, b_ada):
    depth, d, nd = w_ada.shape
    n = c.shape[0]
    tn = _pick_tile(nd, 1536, LANE)
    return pl.pallas_call(
        _mod_kernel,
        grid=(depth, nd // tn),
        in_specs=[pl.BlockSpec((n, d), lambda l, j: (0, 0)),
                  pl.BlockSpec((None, d, tn), lambda l, j: (l, 0, j)),
                  pl.BlockSpec((None, 1, tn), lambda l, j: (l, 0, j))],
        out_specs=pl.BlockSpec((None, n, tn), lambda l, j: (l, 0, j)),
        out_shape=jax.ShapeDtypeStruct((depth, n, nd), F32),
        compiler_params=_cparams(2),
        name="adaln_mod",
    )(c, w_ada, b_ada.reshape(depth, 1, nd))


class _Mod:
    def __init__(self, arr, per_token, seq_len=None):
        self.arr, self.per_token, self.seq_len = arr, per_token, seq_len

    def spec(self, k, tm, d):
        if self.per_token:
            return pl.BlockSpec((None, tm, d), lambda i, *_: (k, i, 0))
        assert self.seq_len % tm == 0
        tps = self.seq_len // tm
        return pl.BlockSpec((None, 1, d), lambda i, *_: ((i // tps) * N_MOD + k, 0, 0))


def _gain_spec(k, d):
    return pl.BlockSpec((None, 1, d), lambda i, *_: (k, 0, 0))


def _mix_residual(x, oa_ref, ob_ref, oc_ref, ga_ref, g_ref, w_ref):
    wa = HW
    wb = wa + N_HEADS * DV_B
    y = (_dot(oa_ref[...].astype(BF16), w_ref[0:wa, :])
         + _dot(ob_ref[...].astype(BF16), w_ref[wa:wb, :])
         + _dot(oc_ref[...].astype(BF16), w_ref[wb:wb + HW, :]))
    return x + ga_ref[...] * _rms_rows(y, g_ref[...])


class _Rows:
    def __init__(self, ref, r0, n, full):
        self.ref, self.rows = ref, (slice(r0, r0 + n) if ref.shape[0] == full else slice(None))

    def __getitem__(self, _):
        return self.ref[self.rows, :]


def _ffn_kernel(x_ref, *refs, f, tf, with_mix, n_split):
    tm = x_ref.shape[0]
    o_ref = refs[-1]
    for r0 in range(0, tm, tm // n_split):
        view = lambda ref: _Rows(ref, r0, tm // n_split, tm)
        x = x_ref[r0:r0 + tm // n_split, :]
        rest = refs[:-1]
        if with_mix:
            oa, ob, oc, ga2, g3, w_mix = rest[:6]
            x = _mix_residual(x, view(oa), view(ob), view(oc), view(ga2), g3, w_mix)
            rest = rest[6:]
        sh_ref, sc_ref, ga_ref, gin_ref, gout_ref, wi_ref, wo_ref = rest
        h = (_rms_rows(x, gin_ref[...]) * (1.0 + view(sc_ref)[...]) + view(sh_ref)[...]).astype(BF16)
        y = None
        for c0 in range(0, f, tf):
            gate = _dot(h, wi_ref[:, c0:c0 + tf])
            up = _dot(h, wi_ref[:, f + c0:f + c0 + tf])
            part = _dot((_silu(gate) * up).astype(BF16), wo_ref[c0:c0 + tf, :])
            y = part if y is None else y + part
        o_ref[r0:r0 + tm // n_split, :] = x + MACARON_W * view(ga_ref)[...] * _rms_rows(y, gout_ref[...])


def _ffn_call(x, mod, mod_k, gains, gain_k, w_in, w_out, layer, tm, mix=None):
    rows, d = x.shape
    f = w_out.shape[1]
    tf = _pick_tile(f, 1408, LANE)
    row = lambda w: pl.BlockSpec((tm, w), lambda i: (i, 0))
    resident = lambda shape: pl.BlockSpec((None,) + shape, lambda i: (layer, 0, 0),
                                          pipeline_mode=pl.Buffered(1))
    in_specs, args = [row(d)], [x]
    if mix is not None:
        oa, ob, oc, w_mix = mix
        in_specs += [row(oa.shape[1]), row(ob.shape[1]), row(oc.shape[1]), mod.spec(5, tm, d), _gain_spec(3, d),
                     pl.BlockSpec(w_mix.shape, lambda i: (0, 0), pipeline_mode=pl.Buffered(1))]
        args += [oa, ob, oc, mod.arr, gains, w_mix]
    in_specs += [mod.spec(mod_k, tm, d), mod.spec(mod_k + 1, tm, d), mod.spec(mod_k + 2, tm, d),
                 _gain_spec(gain_k, d), _gain_spec(gain_k + 1, d),
                 resident((d, 2 * f)), resident((f, d))]
    args += [mod.arr, mod.arr, mod.arr, gains, gains, w_in, w_out]
    return pl.pallas_call(
        functools.partial(_ffn_kernel, f=f, tf=tf, with_mix=mix is not None,
                          n_split=2 if tm % (2 * SUBLANE) == 0 else 1),
        grid=(rows // tm,),
        in_specs=in_specs,
        out_specs=row(d),
        out_shape=jax.ShapeDtypeStruct((rows, d), F32),
        compiler_params=_cparams(1),
        name="macaron_ffn",
    )(*args)


_PROJ_GROUPS = (("sbq", HW), ("sbk", HW), ("sbv", HW),
                ("gq", HW), ("gk", HW), ("gv", N_HEADS * DV_B), ("gr", N_HEADS * DV_B),
                ("dq", HW), ("dqi", HW), ("dkv", LANE), ("dkiwg", LANE))
_PROJ_OFFSETS = tuple(int(v) for v in np.cumsum([0] + [w for _, w in _PROJ_GROUPS]))
_PROJ_WIDTH = _PROJ_OFFSETS[-1]
_WI_LANE = HEAD_DIM
_GB_LANE = _WI_LANE + N_HEADS
_GATE = "gate"
_PROJ_OUTPUTS = (("sbq", 0, HW), ("sbk", 0, HW), ("sbv", 0, HW), ("gq", 0, HW), ("gk", 0, HW),
                 ("gv", 0, N_HEADS * DV_B), ("gr", 0, N_HEADS * DV_B), (_GATE, 0, HW),
                 ("dq", 0, HW), ("dqi", 0, HW), ("dkv", 0, HEAD_DIM), ("dkv", HEAD_DIM, HEAD_DIM),
                 ("dkiwg", 0, HEAD_DIM), ("dkiwg", 0, LANE))
_PROJ_ROW_OUTPUTS = (("sbq", 0, HW), ("gq", 0, HW), ("gk", 0, HW), ("gv", 0, N_HEADS * DV_B),
                     ("gr", 0, N_HEADS * DV_B), (_GATE, 0, HW), ("dq", 0, HW), ("dqi", 0, HW),
                     ("dkv", 0, HEAD_DIM), ("dkiwg", 0, HEAD_DIM), ("dkiwg", 0, LANE))
_PROJ_T_OUTPUTS = (("sbk", 0, HW), ("sbv", 0, HW), ("dkv", 0, HEAD_DIM), ("dkv", HEAD_DIM, HEAD_DIM),
                   ("dkiwg", 0, HEAD_DIM))


def _relayout_w_in(w_in):
    d = w_in.shape[0]
    widths = (HW, HW, HW, HW, HW, N_HEADS * DV_B, N_HEADS * DV_B, GATE_RANK,
              HW, HEAD_DIM, HEAD_DIM, HW, HEAD_DIM, N_HEADS)
    offs = np.cumsum((0,) + widths)
    col = lambda i: w_in[:, offs[i]:offs[i + 1]]
    pad = lambda a, w: jnp.pad(a, ((0, 0), (0, w - a.shape[1])))
    parts = [col(0), col(1), col(2), col(3), col(4), col(5), col(6),
             col(8), col(11), jnp.concatenate([col(9), col(10)], axis=1),
             pad(jnp.concatenate([col(12), col(13), col(7)], axis=1), LANE)]
    out = jnp.concatenate(parts, axis=1)
    assert out.shape == (d, _PROJ_WIDTH)
    return out


def _proj_kernel(x_ref, sh_ref, sc_ref, g_ref, w_ref, wgate_ref, bgate_ref, *refs, row_outs, t_outs, n_prev):
    prev_refs = refs[:n_prev]
    row_refs = refs[n_prev:n_prev + len(row_outs)]
    t_refs = refs[n_prev + len(row_outs):]
    for o_ref, prev_ref in zip(t_refs, prev_refs):
        o_ref[:prev_ref.shape[0]] = prev_ref[...]
    h = (_rms_rows(x_ref[...], g_ref[...]) * (1.0 + sc_ref[...]) + sh_ref[...]).astype(BF16)
    for gi, (name, _) in enumerate(_PROJ_GROUPS):
        p = _dot(h, w_ref[:, _PROJ_OFFSETS[gi]:_PROJ_OFFSETS[gi + 1]])
        groups = {name: p}
        if name == "dkiwg":
            pre = _dot(p.astype(BF16), wgate_ref[...]) + bgate_ref[...]
            groups[_GATE] = _log_sigmoid(pre) * (1.0 / GATE_TAU)
        for o_ref, (out_group, start, width) in zip(row_refs, row_outs):
            if out_group in groups:
                o_ref[...] = groups[out_group][:, start:start + width]
        for o_ref, (out_group, start, width) in zip(t_refs, t_outs):
            if out_group in groups:
                o_ref[o_ref.shape[0] - 1] = groups[out_group].T[start:start + width, :]


def _proj_call(x, mod, gains, w_in, w_gate, b_gate, tm, stacked=None):
    rows, d = x.shape
    in_specs = [pl.BlockSpec((tm, d), lambda i: (i, 0)),
                mod.spec(3, tm, d), mod.spec(4, tm, d), _gain_spec(2, d),
                pl.BlockSpec((d, _PROJ_WIDTH), lambda i: (0, 0)),
                pl.BlockSpec((LANE, HW), lambda i: (0, 0)),
                pl.BlockSpec((1, HW), lambda i: (0, 0))]
    args = [x, mod.arr, mod.arr, gains, w_in, w_gate, b_gate]
    if stacked is None:
        row_outs, t_outs, prev = _PROJ_OUTPUTS, (), ()
        t_specs, t_shapes = [], []
    else:
        layer, n_seq, prev = stacked
        row_outs, t_outs = _PROJ_ROW_OUTPUTS, _PROJ_T_OUTPUTS
        t = rows // n_seq
        tps = t // tm
        tile = lambda n, w: pl.BlockSpec((n, None, w, tm), lambda i: (0, i // tps, 0, i % tps))
        t_specs = [tile(layer + 1, w) for _, _, w in t_outs]
        t_shapes = [jax.ShapeDtypeStruct((layer + 1, n_seq, w, t), F32) for _, _, w in t_outs]
        prev = () if prev is None else tuple(prev)
        in_specs += [tile(layer, w) for _, _, w in t_outs[:len(prev)]]
    outs = pl.pallas_call(
        functools.partial(_proj_kernel, row_outs=row_outs, t_outs=t_outs, n_prev=len(prev)),
        grid=(rows // tm,),
        in_specs=in_specs,
        out_specs=[pl.BlockSpec((tm, w), lambda i: (i, 0)) for _, _, w in row_outs] + t_specs,
        out_shape=[jax.ShapeDtypeStruct((rows, w), F32) for _, _, w in row_outs] + t_shapes,
        compiler_params=_cparams(1),
        name="mix_in_proj",
    )(*args, *prev)
    return outs[:len(row_outs)], outs[len(row_outs):]


def _sb_kernel(q_ref, k_ref, v_ref, *rest, tq, tk, past, new_apart):
    if new_apart:
        kn_ref, vn_ref, u_ref, o_ref, acc_scr, c_scr = rest
    else:
        u_ref, o_ref, acc_scr, c_scr = rest
    i = pl.program_id(1)
    head = _head_lane_id(HW, HEAD_DIM)
    q = q_ref[...] * (HEAD_DIM ** -0.5)
    q_st = jnp.concatenate([jnp.where(head == h, q, 0.0) for h in range(N_HEADS)], axis=0).astype(BF16)
    qpos = past + i * tq + lax.broadcasted_iota(jnp.int32, (tq, 1), 0)
    last_blk = (past + (i + 1) * tq - 2) // tk
    acc_scr[...] = jnp.zeros_like(acc_scr)
    c_scr[...] = jnp.zeros_like(c_scr)
    u = u_ref[...]
    rows = N_HEADS * tq

    heads_per_tile = LANE // HEAD_DIM
    low_half = [(lax.broadcasted_iota(jnp.int32, (1, LANE), 1) // HEAD_DIM) == r
                for r in range(heads_per_tile)]

    def body(n, carry, *, masked):
        j = last_blk - n
        k0 = pl.multiple_of(j * tk, tk)
        if new_apart and masked:
            cols = pl.ds(pl.multiple_of(k0 - past, tk), tk)
            kb, vb = kn_ref[:, cols].astype(BF16), vn_ref[:, cols].astype(BF16)
        else:
            kb = k_ref[:, pl.ds(k0, tk)].astype(BF16)
            vb = v_ref[:, pl.ds(k0, tk)].astype(BF16)
        z = _dot(q_st, kb).reshape(N_HEADS, tq, tk) * _LOG2_E
        soft = jnp.log2(1.0 + jnp.exp2(_neg_abs(z)))
        log_beta = jnp.minimum(z, 0.0) - soft
        log_keep = log_beta - z
        if masked:
            kpos = k0 + lax.broadcasted_iota(jnp.int32, (1, tk), 1)
            valid = (kpos < qpos)[None]
            log_keep = jnp.where(valid, log_keep, 0.0)
        newer = _dot(log_keep.astype(BF16).reshape(rows, tk), u).reshape(N_HEADS, tq, tk)
        c = c_scr[...]
        w = jnp.exp2(log_beta + newer + c)
        if masked:
            w = jnp.where(valid, w, 0.0)
        pv = _dot_nt(w.astype(BF16).reshape(rows, tk), vb).reshape(N_HEADS, tq, HW)
        tiles = []
        for t0 in range(HW // LANE):
            sl = slice(t0 * LANE, (t0 + 1) * LANE)
            col = pv[t0 * heads_per_tile][:, sl]
            for r in range(1, heads_per_tile):
                col = jnp.where(low_half[r], pv[t0 * heads_per_tile + r][:, sl], col)
            tiles.append(col)
        acc_scr[...] += jnp.concatenate(tiles, axis=1)
        c_scr[...] = c + newer[:, :, 0:1] + log_keep[:, :, 0:1]
        return carry

    n_masked = last_blk + 1 - jnp.minimum((past + i * tq) // tk, last_blk + 1)
    lax.fori_loop(0, n_masked, functools.partial(body, masked=True), 0)
    _fori_by_two(n_masked, last_blk + 1, functools.partial(body, masked=False), 0)
    o_ref[...] = acc_scr[...]


def _sb_call(q, k_t, v_t, layer, past, tq, tk, new_kv=None):
    b, t, _ = q.shape
    u = (np.arange(tk)[:, None] > np.arange(tk)[None, :]).astype(np.float32)
    kv_spec = pl.BlockSpec((None, None, HW, k_t.shape[3]), lambda bi, i: (layer, bi, 0, 0))
    in_specs = [pl.BlockSpec((None, tq, HW), lambda bi, i: (bi, i, 0)), kv_spec, kv_spec]
    args = [q, k_t, v_t]
    if new_kv is not None:
        assert past % tk == 0 and k_t.shape[3] == past and t <= tk
        in_specs += [pl.BlockSpec((None, HW, new_kv[0].shape[2]), lambda bi, i: (bi, 0, 0))] * 2
        args += list(new_kv)
    return pl.pallas_call(
        functools.partial(_sb_kernel, tq=tq, tk=tk, past=past, new_apart=new_kv is not None),
        grid=(b, t // tq),
        in_specs=in_specs + [pl.BlockSpec((tk, tk), lambda bi, i: (0, 0))],
        out_specs=pl.BlockSpec((None, tq, HW), lambda bi, i: (bi, i, 0)),
        out_shape=jax.ShapeDtypeStruct((b, t, HW), F32),
        scratch_shapes=[pltpu.VMEM((tq, HW), F32), pltpu.VMEM((N_HEADS, tq, 1), F32)],
        compiler_params=_cparams(2),
        name="stick_breaking_attn",
    )(*args, jnp.asarray(u, BF16))


def _gla_tables(c):
    n_lev = int(math.log2(c))
    assert 1 << n_lev == c
    t = np.arange(c)
    rows = []
    for lev in range(1, n_lev + 1):
        blk = c >> (lev - 1)
        ref = (t // blk) * blk + blk // 2 - 1
        lo, hi = np.minimum(t, ref), np.maximum(t, ref)
        rows.append(((t[None, :] > lo[:, None]) & (t[None, :] <= hi[:, None])).astype(np.float32))
    rows.append((t[None, :] <= t[:, None]).astype(np.float32))
    rows.append((t[None, :] > t[:, None]).astype(np.float32))
    table = np.concatenate(rows, axis=0)
    lvl = np.full((c, c), n_lev + 1, np.int32)
    for lev in range(1, n_lev + 1):
        blk = c >> (lev - 1)
        half = blk // 2
        same = (t[:, None] // blk) == (t[None, :] // blk)
        pair = same & ((t[:, None] % blk) >= half) & ((t[None, :] % blk) < half)
        lvl[pair] = lev
    lvl[t, t] = 0
    return np.concatenate([table, table], axis=1), np.tile(lvl.T, (1, N_HEADS)), n_lev


def _gla_kernel(q_seqs, k_seqs, v_seqs, r_seqs, la_seqs, s0_ref, tab_ref, lvl_ref, g_ref,
                o_seqs, s_ref, s_seqs, *, c, n_lev, n_chunks, n_steps):
    j = pl.program_id(1)

    @pl.when(j == 0)
    def _():
        s_seqs[...] = s0_ref[...]

    head = _head_lane_id(HW, HEAD_DIM)
    tab = tab_ref[...]
    lvl = lvl_ref[...]

    def chunk(ci, carry):
        for bi in range(q_seqs.shape[0]):
            seq_chunk(bi, pl.ds(pl.multiple_of(ci * c, c), c))
        return carry

    def seq_chunk(bi, rows):
        q_ref, k_ref, v_ref, r_ref, la_ref, o_ref, s_scr = (
            a.at[bi] for a in (q_seqs, k_seqs, v_seqs, r_seqs, la_seqs, o_seqs, s_seqs))
        q = q_ref[rows, :] * (HEAD_DIM ** -0.5)
        k = k_ref[rows, :]
        decays = jnp.exp(_dot(tab, jnp.concatenate(_split_bf16(la_ref[rows, :]), axis=0)))
        by_head = lambda a: jnp.concatenate([jnp.where(head == h, a, 0.0) for h in range(N_HEADS)],
                                            axis=0).astype(BF16)
        att_t = jnp.zeros((c, N_HEADS * c), F32)
        for lev in range(n_lev + 1):
            if lev == 0:
                ql, kl = q, k
            else:
                e = decays[(lev - 1) * c:lev * c]
                ql, kl = q * e, k * e
            a_t = _dot_nt(kl.astype(BF16), by_head(ql))
            att_t = jnp.where(lvl == lev, a_t, att_t)
        q_decay = decays[n_lev * c:(n_lev + 1) * c]
        k_out = (k * decays[(n_lev + 1) * c:(n_lev + 2) * c]).astype(BF16)
        s = s_scr[...]
        v_all = v_ref[rows, :].astype(BF16)
        intra = _dot_tn(att_t.astype(BF16), v_all)
        inter = _dot(by_head(q * q_decay), s.astype(BF16))
        for h in range(N_HEADS):
            o = intra[h * c:(h + 1) * c, h * DV_B:(h + 1) * DV_B] + inter[h * c:(h + 1) * c]
            y = _rms_rows(o, g_ref[:, h * DV_B:(h + 1) * DV_B])
            o_ref[rows, h * DV_B:(h + 1) * DV_B] = y * _silu(r_ref[rows, h * DV_B:(h + 1) * DV_B])
        kv = _dot_tn(k_out, v_all)
        ds = jnp.concatenate([kv[h * HEAD_DIM:(h + 1) * HEAD_DIM, h * DV_B:(h + 1) * DV_B]
                              for h in range(N_HEADS)], axis=0)
        chunk_decay = jnp.broadcast_to(q_decay[c - 1:c, :], (DV_B, HW)).T
        s_scr[...] = chunk_decay * s + ds

    lax.fori_loop(0, n_chunks, chunk, 0)

    @pl.when(j == n_steps - 1)
    def _():
        s_ref[...] = s_seqs[...]


def _gla_call(q, k, v, r, la, s0, g_gla, c):
    b, t, _ = q.shape
    vw = N_HEADS * DV_B
    tc = _pick_tile(t, 512, c)
    n_steps = t // tc
    nb = max(n for n in (4, 2, 1) if b % n == 0)
    table, lvl, n_lev = _gla_tables(c)
    qk_spec = pl.BlockSpec((nb, tc, HW), lambda bi, j: (bi, j, 0))
    vr_spec = pl.BlockSpec((nb, tc, vw), lambda bi, j: (bi, j, 0))
    st_spec = pl.BlockSpec((nb, HW, DV_B), lambda bi, j: (bi, 0, 0))
    const = lambda shape: pl.BlockSpec(shape, lambda bi, j: (0, 0))
    return pl.pallas_call(
        functools.partial(_gla_kernel, c=c, n_lev=n_lev, n_chunks=tc // c, n_steps=n_steps),
        grid=(b // nb, n_steps),
        in_specs=[qk_spec, qk_spec, vr_spec, vr_spec, qk_spec, st_spec,
                  const(table.shape), const(lvl.shape), const((1, vw))],
        out_specs=[vr_spec, st_spec],
        out_shape=[jax.ShapeDtypeStruct((b, t, vw), F32), jax.ShapeDtypeStruct((b, HW, DV_B), F32)],
        scratch_shapes=[pltpu.VMEM((nb, HW, DV_B), F32)],
        compiler_params=_cparams(2),
        name="gated_linear_attn",
    )(q, k, v, r, la, s0, jnp.asarray(table, BF16), jnp.asarray(lvl), g_gla)


_COUNT_ROWS = 32


def _key_to_f32(key):
    key = jnp.maximum(key, _KEY_NEG_INF)
    bits = jnp.where(key < 0, key ^ 0x7FFFFFFF, key)
    return lax.bitcast_convert_type(bits, F32)


def _dsa_kernel(q_ref, qi_ref, wi_ref, kc_ref, vc_ref, ki_ref, eye_ref, tri_ref, o_ref,
                sc_scr, hi_scr, lo_scr, lg_scr, vct_scr, acc_scr, *, tq, tk, past, l_real, n_sel):
    i = pl.program_id(1)
    scale = HEAD_DIM ** -0.5
    eye = eye_ref[...]
    l_pad = sc_scr.shape[0]

    @pl.when(i == 0)
    def _():
        vct_scr[...] = vc_ref[...].astype(BF16)

    q_t = _dot_nt(eye, (q_ref[...] * scale).astype(BF16)).astype(BF16)
    qi_t = _dot_nt(eye, (qi_ref[...] * scale).astype(BF16)).astype(BF16)
    w_t = wi_ref[...].T * (N_HEADS ** -0.5)
    qpos = past + i * tq + lax.broadcasted_iota(jnp.int32, (1, tq), 1)
    chunk_shift = int(math.log2(CHUNK))
    qchunk = lax.shift_right_logical(qpos, chunk_shift)
    k_end = ((past + (i + 1) * tq - 1) // CHUNK + 1) * CHUNK
    n_blk = jnp.minimum((k_end + tk - 1) // tk, l_pad // tk)

    def rows_of(j):
        return pl.ds(pl.multiple_of(j * tk, tk), tk)

    def admissible(j):
        kpos = j * tk + lax.broadcasted_iota(jnp.int32, (tk, 1), 0)
        return (lax.shift_right_logical(kpos, chunk_shift) <= qchunk) & (kpos < l_real)

    def score_blk(j, carry):
        kib = ki_ref[rows_of(j), :].astype(BF16)
        s = jnp.zeros((tk, tq), F32)
        for h in range(N_HEADS):
            sh = _dot(kib, qi_t[h * HEAD_DIM:(h + 1) * HEAD_DIM])
            s = s + w_t[_WI_LANE + h:_WI_LANE + h + 1] * jnp.maximum(sh, 0.0)
        s = jnp.where(admissible(j), jnp.where(s == 0.0, 0.0, s), _NEG_INF)
        sc_scr[rows_of(j), :] = s
        bits = lax.bitcast_convert_type(s, jnp.int32)
        key = jnp.where(bits < 0, bits ^ 0x7FFFFFFF, bits)
        hi_scr[rows_of(j), :] = lax.shift_right_arithmetic(key, 16).astype(jnp.int16)
        lo_scr[rows_of(j), :] = ((key & 0xFFFF) - 32768).astype(jnp.int16)
        return carry

    _fori_by_two(0, n_blk, score_blk, 0)

    def count(ref, pred, one, zero):
        def blk(j, acc):
            m = jnp.where(pred(ref[rows_of(j), :]), one, zero).reshape(tk // _COUNT_ROWS, _COUNT_ROWS, tq)
            for r in range(tk // _COUNT_ROWS):
                acc = acc + m[r]
            return acc
        acc = _fori_by_two(0, n_blk, blk, jnp.zeros((_COUNT_ROWS, tq), one.dtype))
        return jnp.sum(acc.astype(jnp.int32), axis=0, keepdims=True)

    one16, zero16 = jnp.int16(1), jnp.int16(0)

    def bisect16(ref, want):
        def bit(it, tau):
            cand = tau + lax.shift_left(jnp.int32(1), 15 - it)
            c16 = cand.astype(jnp.int16)
            return jnp.where(count(ref, lambda a: a >= c16, one16, zero16) >= want, cand, tau)
        return lax.fori_loop(0, 16, bit, jnp.full((1, tq), -32768, jnp.int32))

    hi_star = bisect16(hi_scr, n_sel)
    h16 = hi_star.astype(jnp.int16)
    want_lo = n_sel - count(hi_scr, lambda a: a > h16, one16, zero16)

    def mask_lo(j, carry):
        lo_scr[rows_of(j), :] = jnp.where(hi_scr[rows_of(j), :] == h16, lo_scr[rows_of(j), :],
                                          jnp.int16(-32768))
        return carry

    lax.fori_loop(0, n_blk, mask_lo, 0)
    lo_star = bisect16(lo_scr, want_lo)
    thr = _key_to_f32(lax.shift_left(hi_star, 16) + (lo_star + 32768))
    need = (n_sel - count(sc_scr, lambda a: a > thr, jnp.float32(1.0), jnp.float32(0.0))).astype(F32)

    tri = tri_ref[...]
    fold = lambda a, op: op(a.reshape(tk // SUBLANE, SUBLANE, tq), axis=0)
    parts = lambda v: tuple(jnp.full((SUBLANE, tq), v, F32) for _ in range(N_HEADS))

    def logit_blk(j, carry):
        m_parts, ties_before = carry
        s = sc_scr[rows_of(j), :]
        tie = s == thr
        rank = _dot(tri, jnp.where(tie, 1.0, 0.0).astype(BF16)) + ties_before
        sel = admissible(j) & ((s > thr) | (tie & (rank <= need)))
        bias = jnp.where(sel, 0.0, _NEG_INF)
        kcb = kc_ref[rows_of(j), :].astype(BF16)
        new_parts = []
        for h in range(N_HEADS):
            lg = _dot(kcb, q_t[h * HEAD_DIM:(h + 1) * HEAD_DIM]) + bias
            lg_scr[h, rows_of(j), :] = lg
            new_parts.append(jnp.maximum(m_parts[h], fold(lg, jnp.max)))
        return tuple(new_parts), rank[tk - 1:tk, :]

    m_parts, _ = _fori_by_two(0, n_blk, logit_blk, (parts(_NEG_INF), jnp.zeros((1, tq), F32)))
    m_use = []
    for h in range(N_HEADS):
        m = jnp.max(m_parts[h], axis=0, keepdims=True)
        m_use.append(jnp.where(m == _NEG_INF, 0.0, m))
    acc_scr[...] = jnp.zeros_like(acc_scr)

    def pv_blk(j, l_parts):
        vct = vct_scr[:, rows_of(j)]
        new_parts = []
        for h in range(N_HEADS):
            hd = slice(h * HEAD_DIM, (h + 1) * HEAD_DIM)
            p = jnp.exp(lg_scr[h, rows_of(j), :] - m_use[h])
            new_parts.append(l_parts[h] + fold(p, jnp.sum))
            acc_scr[hd, :] += _dot(vct, p.astype(BF16))
        return tuple(new_parts)

    l_parts = _fori_by_two(0, n_blk, pv_blk, parts(0.0))
    out_t = jnp.concatenate(
        [acc_scr[h * HEAD_DIM:(h + 1) * HEAD_DIM, :] / jnp.sum(l_parts[h], axis=0, keepdims=True)
         for h in range(N_HEADS)], axis=0)
    o_ref[...] = out_t.T


def _dsa_call(q, qi, wi, kc, vc_t, ki, layer, past, l_real, tq, tk):
    b, t, _ = q.shape
    l_pad = kc.shape[1]
    n_sel = min(TOPK_MAX, l_real // 4)
    assert tk >= n_sel and l_pad % tk == 0 and tk % _COUNT_ROWS == 0
    row = lambda w: pl.BlockSpec((None, tq, w), lambda bi, i: (bi, i, 0))
    full = pl.BlockSpec((None, l_pad, HEAD_DIM), lambda bi, i: (bi, 0, 0))
    full_t = pl.BlockSpec((None, None, HEAD_DIM, l_pad), lambda bi, i: (layer, bi, 0, 0))
    const = lambda n: pl.BlockSpec((n, n), lambda bi, i: (0, 0))
    eye = np.eye(HW, dtype=np.float32)
    tri = (np.arange(tk)[:, None] >= np.arange(tk)[None, :]).astype(np.float32)
    return pl.pallas_call(
        functools.partial(_dsa_kernel, tq=tq, tk=tk, past=past, l_real=l_real, n_sel=n_sel),
        grid=(b, t // tq),
        in_specs=[row(HW), row(HW), row(LANE), full, full_t, full, const(HW), const(tk)],
        out_specs=row(HW),
        out_shape=jax.ShapeDtypeStruct((b, t, HW), F32),
        scratch_shapes=[pltpu.VMEM((l_pad, tq), F32),
                        pltpu.VMEM((l_pad, tq), jnp.int16), pltpu.VMEM((l_pad, tq), jnp.int16),
                        pltpu.VMEM((N_HEADS, l_pad, tq), F32),
                        pltpu.VMEM((HEAD_DIM, l_pad), BF16), pltpu.VMEM((HW, tq), F32)],
        compiler_params=_cparams(2),
        name="indexer_sparse_attn",
    )(q, qi, wi, kc, vc_t, ki, jnp.asarray(eye, BF16), jnp.asarray(tri, BF16))


def _pad_rows(a, n):
    return a if n == a.shape[1] else jnp.pad(a, ((0, 0), (0, n - a.shape[1]), (0, 0)))


def _pad_last(a, n):
    return a if n == a.shape[-1] else jnp.pad(a, [(0, 0)] * (a.ndim - 1) + [(0, n - a.shape[-1])])


def _encoder_layer(x, n_seq, t, mod, past, w, stacked=None):
    rows, d = x.shape
    tm = _pick_tile(t, 512, SUBLANE) if not mod.per_token else rows
    tm_ffn = _pick_tile(t, 1024, SUBLANE) if not mod.per_token else rows
    x = _ffn_call(x, mod, 0, w["gains"], 0, w["w_f1_in"], w["w_f1_out"], w["layer"], tm_ffn)
    seq = lambda a: a.reshape(n_seq, t, a.shape[-1])
    p_len = 0 if past is None else past["k_sb_t"].shape[3]
    l_real = p_len + t
    tk = 256 if l_real >= 256 else LANE
    l_pad = -(-l_real // tk) * tk
    tq = _pick_tile(t, 256, SUBLANE)
    proj = functools.partial(_proj_call, x, mod, w["gains"], w["w_in"], w["w_gate"], w["b_gate"], tm)
    if stacked is None:
        (sbq, sbk, sbv, gq, gk, gv, gr, la, dq, dqi, dkc, dvc, dki, dwi), _ = proj()
        lyr = past["layer"]
        keys_last = lambda new: jnp.swapaxes(seq(new), 1, 2)
        join_t = lambda old_t, new: _pad_last(jnp.concatenate([old_t, keys_last(new)], axis=2), l_pad)[None]
        join_r = lambda old, new: _pad_rows(jnp.concatenate([old, seq(new)], axis=1), l_pad)
        if p_len % tk == 0 and t <= tk:
            new_kv = tuple(_pad_last(keys_last(a), tk) for a in (sbk, sbv))
            o_a = _sb_call(seq(sbq), past["k_sb_t"], past["v_sb_t"], lyr, p_len, tq, tk, new_kv=new_kv)
        else:
            o_a = _sb_call(seq(sbq), join_t(past["k_sb_t"][lyr], sbk), join_t(past["v_sb_t"][lyr], sbv),
                           0, p_len, tq, tk)
        vc_t = join_t(past["v_dsa_t"], dvc)
        kc_rows, ki_rows = join_r(past["k_dsa"], dkc), join_r(past["k_idx"], dki)
        layer = 0
        new_rows = (seq(sbk).reshape(n_seq, t, N_HEADS, HEAD_DIM), seq(sbv).reshape(n_seq, t, N_HEADS, HEAD_DIM),
                    seq(dkc), seq(dvc), seq(dki))
    else:
        assert past is None and l_pad == t and t % tm == 0 and tm % LANE == 0
        layer, prev = stacked
        (sbq, gq, gk, gv, gr, la, dq, dqi, dkc, dki, dwi), new_rows = proj((layer, n_seq, prev))
        k_t, v_t, _, vc_t, _ = new_rows
        kc_rows, ki_rows = seq(dkc), seq(dki)
        o_a = _sb_call(seq(sbq), k_t, v_t, layer, p_len, tq, tk)

    c = CHUNK if t % CHUNK == 0 else t
    s0 = jnp.zeros((n_seq, HW, DV_B), F32) if past is None else past["gla"]
    o_b, s_new = _gla_call(seq(gq), seq(gk), seq(gv), seq(gr), seq(la), s0, w["g_gla"], c)

    t_c = t if t % LANE == 0 else -(-t // LANE) * LANE
    tq_c = _pick_tile(t_c, 256, LANE)
    qpad = lambda a: _pad_rows(seq(a), t_c)
    o_c = _dsa_call(qpad(dq), qpad(dqi), qpad(dwi), kc_rows, vc_t, ki_rows, layer, p_len, l_real, tq_c, tk)
    o_c = o_c if t_c == t else o_c[:, :t]

    flat = lambda a: a.reshape(rows, a.shape[-1])
    x = _ffn_call(x, mod, 6, w["gains"], 4, w["w_f2_in"], w["w_f2_out"], w["layer"], tm_ffn,
                  mix=(flat(o_a), flat(o_b), flat(o_c), w["w_out"]))
    return x, new_rows, s_new.reshape(n_seq, N_HEADS, HEAD_DIM, DV_B)


def kernel(x_prompt, x_sample, cache_k_sb, cache_v_sb, cache_k_dsa, cache_v_dsa, cache_k_idx, state_gla,
           c_prompt, c_sample, w_ada, b_ada, norm_gains, w_ffn1_in, w_ffn1_out, w_ffn2_in, w_ffn2_out,
           w_in, w_gla_gate, b_gla_gate, gla_norm, w_out):
    bp, tp, d = x_prompt.shape
    bs, ts, _ = x_sample.shape
    depth = w_ada.shape[0]
    p_len = cache_k_sb.shape[2]

    mods = _mod_call(jnp.concatenate([c_prompt, c_sample], axis=0), w_ada, b_ada)
    xp = x_prompt.reshape(bp * tp, d)
    xs = x_sample.reshape(bs * ts, d)
    ffn_w = [a.astype(BF16) for a in (w_ffn1_in, w_ffn1_out, w_ffn2_in, w_ffn2_out)]
    sb_cache_t = [jnp.transpose(c, (0, 1, 3, 4, 2)).reshape(depth, bs, HW, p_len) for c in (cache_k_sb, cache_v_sb)]
    shared_p, gla_p, acc_s = None, [], []
    for l in range(depth):
        w_gate = jnp.pad(w_gla_gate[l], ((_GB_LANE, LANE - GATE_RANK - _GB_LANE), (0, 0))).astype(BF16)
        w = dict(gains=norm_gains[l].reshape(-1, 1, d),
                 layer=l, w_f1_in=ffn_w[0], w_f1_out=ffn_w[1], w_f2_in=ffn_w[2], w_f2_out=ffn_w[3],
                 w_in=_relayout_w_in(w_in[l]).astype(BF16), w_gate=w_gate,
                 b_gate=b_gla_gate[l].reshape(1, HW), g_gla=gla_norm[l].reshape(1, N_HEADS * DV_B),
                 w_out=w_out[l].astype(BF16))
        mod_p = _Mod(mods[l, :bp].reshape(bp * N_MOD, 1, d), False, tp)
        mod_s_rows = jnp.repeat(mods[l, bp:].reshape(bs, N_MOD, d), ts, axis=0)
        mod_s = _Mod(jnp.transpose(mod_s_rows, (1, 0, 2)), True)
        past = dict(layer=l, k_sb_t=sb_cache_t[0], v_sb_t=sb_cache_t[1],
                    k_dsa=cache_k_dsa[l], v_dsa_t=jnp.swapaxes(cache_v_dsa[l], 1, 2), k_idx=cache_k_idx[l],
                    gla=state_gla[l].reshape(bs, HW, DV_B))
        xp, shared_p, s_p = _encoder_layer(xp, bp, tp, mod_p, None, w, stacked=(l, shared_p))
        xs, rows_s, s_s = _encoder_layer(xs, bs, ts, mod_s, past, w)
        gla_p.append(s_p)
        acc_s.append(rows_s + (s_s,))
    k_t, v_t, kc_t, vc_t, ki_t = shared_p
    heads_out = lambda a: jnp.transpose(a.reshape(depth, bp, N_HEADS, HEAD_DIM, tp), (0, 1, 4, 2, 3))
    tokens_out = lambda a: jnp.swapaxes(a, 2, 3)
    field = lambda acc, i: jnp.stack([r[i] for r in acc], axis=0)
    return (xp.reshape(bp, tp, d), xs.reshape(bs, ts, d),
            heads_out(k_t), heads_out(v_t), tokens_out(kc_t), tokens_out(vc_t), tokens_out(ki_t),
            jnp.stack(gla_p, axis=0), *(field(acc_s, i) for i in range(6)))
```

```python
import functools
import math

import numpy as np
import jax
import jax.numpy as jnp
from jax import lax
from jax.experimental import pallas as pl
from jax.experimental.pallas import tpu as pltpu

F32 = jnp.float32
BF16 = jnp.bfloat16

CHUNK = 64
N_HEADS = 4
HEAD_DIM = 64
HW = N_HEADS * HEAD_DIM
DV_B = 128
GATE_RANK = 16
GATE_TAU = 16.0
TOPK_MAX = 256
EPS = 1e-6
MACARON_W = 0.5
N_MOD = 9

LANE = 128
SUBLANE = 8
VMEM_BYTES = 64 * 1024 * 1024
VMEM_LIMIT_BYTES = VMEM_BYTES * 7 // 8

_NEG_INF = float("-inf")
_KEY_NEG_INF = -2**31 + 0x7FFFFF
_INT_MIN = -2**31


def _pick_tile(n, target, mult):
    if n <= target:
        return n
    t = (target // mult) * mult
    while t >= mult:
        if n % t == 0:
            return t
        t -= mult
    return n


def _cparams(n_axes):
    return pltpu.CompilerParams(dimension_semantics=("arbitrary",) * n_axes,
                                vmem_limit_bytes=VMEM_LIMIT_BYTES)


def _rms_rows(x, g):
    ms = jnp.mean(x * x, axis=-1, keepdims=True)
    return x * lax.rsqrt(ms + EPS) * g


def _silu(x):
    return x * (1.0 / (1.0 + jnp.exp(-x)))


def _log_sigmoid(x):
    return jnp.minimum(x, 0.0) - jnp.log(1.0 + jnp.exp(-jnp.abs(x)))


_LOG2_E = 1.4426950408889634


def _neg_abs(x):
    bits = lax.bitcast_convert_type(x, jnp.int32) | _INT_MIN
    return lax.bitcast_convert_type(bits, F32)


def _dot(a, b):
    return jnp.dot(a, b, preferred_element_type=F32)


def _dot_nt(a, b):
    return lax.dot_general(a, b, (((1,), (1,)), ((), ())), preferred_element_type=F32)


def _dot_tn(a, b):
    return lax.dot_general(a, b, (((0,), (0,)), ((), ())), preferred_element_type=F32)


def _fori_by_two(lo, hi, body, init):
    n = hi - lo

    def two(p, carry):
        j = lo + 2 * p
        return body(j + 1, body(j, carry))

    carry = lax.fori_loop(0, lax.shift_right_logical(n, 1), two, init)
    return lax.cond((n & 1) == 1, lambda c: body(hi - 1, c), lambda c: c, carry)


def _split_bf16(x):
    hi = x.astype(BF16)
    lo = (x - hi.astype(F32)).astype(BF16)
    return hi, lo


def _head_lane_id(width, per_head):
    return lax.broadcasted_iota(jnp.int32, (1, width), 1) // per_head


def _mod_kernel(c_ref, w_ref, b_ref, o_ref):
    a = _silu(c_ref[...]).astype(BF16)
    o_ref[...] = _dot(a, w_ref[...].astype(BF16)) + b_ref[...]


def _mod_call(c, w_ada, b_ada):
    depth, d, nd = w_ada.shape
    n = c.shape[0]
    tn = _pick_tile(nd, 1536, LANE)
    return pl.pallas_call(
        _mod_kernel,
        grid=(depth, nd // tn),
        in_specs=[pl.BlockSpec((n, d), lambda l, j: (0, 0)),
                  pl.BlockSpec((None, d, tn), lambda l, j: (l, 0, j)),
                  pl.BlockSpec((None, 1, tn), lambda l, j: (l, 0, j))],
        out_specs=pl.BlockSpec((None, n, tn), lambda l, j: (l, 0, j)),
        out_shape=jax.ShapeDtypeStruct((depth, n, nd), F32),
        compiler_params=_cparams(2),
        name="adaln_mod",
    )(c, w_ada, b_ada.reshape(depth, 1, nd))


class _Mod:
    def __init__(self, arr, per_token, seq_len=None):
        self.arr, self.per_token, self.seq_len = arr, per_token, seq_len

    def spec(self, k, tm, d):
        if self.per_token:
            return pl.BlockSpec((None, tm, d), lambda i, *_: (k, i, 0))
        assert self.seq_len % tm == 0
        tps = self.seq_len // tm
        return pl.BlockSpec((None, 1, d), lambda i, *_: ((i // tps) * N_MOD + k, 0, 0))


def _gain_spec(k, d):
    return pl.BlockSpec((None, 1, d), lambda i, *_: (k, 0, 0))


def _mix_residual(x, oa_ref, ob_ref, oc_ref, ga_ref, g_ref, w_ref):
    wa = HW
    wb = wa + N_HEADS * DV_B
    y = (_dot(oa_ref[...].astype(BF16), w_ref[0:wa, :])
         + _dot(ob_ref[...].astype(BF16), w_ref[wa:wb, :])
         + _dot(oc_ref[...].astype(BF16), w_ref[wb:wb + HW, :]))
    return x + ga_ref[...] * _rms_rows(y, g_ref[...])


class _Rows:
    def __init__(self, ref, r0, n, full):
        self.ref, self.rows = ref, (slice(r0, r0 + n) if ref.shape[0] == full else slice(None))

    def __getitem__(self, _):
        return self.ref[self.rows, :]


def _ffn_kernel(x_ref, *refs, f, tf, with_mix, n_split):
    tm = x_ref.shape[0]
    o_ref = refs[-1]
    for r0 in range(0, tm, tm // n_split):
        view = lambda ref: _Rows(ref, r0, tm // n_split, tm)
        x = x_ref[r0:r0 + tm // n_split, :]
        rest = refs[:-1]
        if with_mix:
            oa, ob, oc, ga2, g3, w_mix = rest[:6]
            x = _mix_residual(x, view(oa), view(ob), view(oc), view(ga2), g3, w_mix)
            rest = rest[6:]
        sh_ref, sc_ref, ga_ref, gin_ref, gout_ref, wi_ref, wo_ref = rest
        h = (_rms_rows(x, gin_ref[...]) * (1.0 + view(sc_ref)[...]) + view(sh_ref)[...]).astype(BF16)
        y = None
        for c0 in range(0, f, tf):
            gate = _dot(h, wi_ref[:, c0:c0 + tf])
            up = _dot(h, wi_ref[:, f + c0:f + c0 + tf])
            part = _dot((_silu(gate) * up).astype(BF16), wo_ref[c0:c0 + tf, :])
            y = part if y is None else y + part
        o_ref[r0:r0 + tm // n_split, :] = x + MACARON_W * view(ga_ref)[...] * _rms_rows(y, gout_ref[...])


def _ffn_call(x, mod, mod_k, gains, gain_k, w_in, w_out, layer, tm, mix=None):
    rows, d = x.shape
    f = w_out.shape[1]
    tf = _pick_tile(f, 1408, LANE)
    row = lambda w: pl.BlockSpec((tm, w), lambda i: (i, 0))
    resident = lambda shape: pl.BlockSpec((None,) + shape, lambda i: (layer, 0, 0),
                                          pipeline_mode=pl.Buffered(1))
    in_specs, args = [row(d)], [x]
    if mix is not None:
        oa, ob, oc, w_mix = mix
        in_specs += [row(oa.shape[1]), row(ob.shape[1]), row(oc.shape[1]), mod.spec(5, tm, d), _gain_spec(3, d),
                     pl.BlockSpec(w_mix.shape, lambda i: (0, 0), pipeline_mode=pl.Buffered(1))]
        args += [oa, ob, oc, mod.arr, gains, w_mix]
    in_specs += [mod.spec(mod_k, tm, d), mod.spec(mod_k + 1, tm, d), mod.spec(mod_k + 2, tm, d),
                 _gain_spec(gain_k, d), _gain_spec(gain_k + 1, d),
                 resident((d, 2 * f)), resident((f, d))]
    args += [mod.arr, mod.arr, mod.arr, gains, gains, w_in, w_out]
    return pl.pallas_call(
        functools.partial(_ffn_kernel, f=f, tf=tf, with_mix=mix is not None,
                          n_split=2 if tm % (2 * SUBLANE) == 0 else 1),
        grid=(rows // tm,),
        in_specs=in_specs,
        out_specs=row(d),
        out_shape=jax.ShapeDtypeStruct((rows, d), F32),
        compiler_params=_cparams(1),
        name="macaron_ffn",
    )(*args)


_PROJ_GROUPS = (("sbq", HW), ("sbk", HW), ("sbv", HW),
                ("gq", HW), ("gk", HW), ("gv", N_HEADS * DV_B), ("gr", N_HEADS * DV_B),
                ("dq", HW), ("dqi", HW), ("dkv", LANE), ("dkiwg", LANE))
_PROJ_OFFSETS = tuple(int(v) for v in np.cumsum([0] + [w for _, w in _PROJ_GROUPS]))
_PROJ_WIDTH = _PROJ_OFFSETS[-1]
_WI_LANE = HEAD_DIM
_GB_LANE = _WI_LANE + N_HEADS
_GATE = "gate"
_PROJ_OUTPUTS = (("sbq", 0, HW), ("sbk", 0, HW), ("sbv", 0, HW), ("gq", 0, HW), ("gk", 0, HW),
                 ("gv", 0, N_HEADS * DV_B), ("gr", 0, N_HEADS * DV_B), (_GATE, 0, HW),
                 ("dq", 0, HW), ("dqi", 0, HW), ("dkv", 0, HEAD_DIM), ("dkv", HEAD_DIM, HEAD_DIM),
                 ("dkiwg", 0, HEAD_DIM), ("dkiwg", 0, LANE))
_PROJ_ROW_OUTPUTS = (("sbq", 0, HW), ("gq", 0, HW), ("gk", 0, HW), ("gv", 0, N_HEADS * DV_B),
                     ("gr", 0, N_HEADS * DV_B), (_GATE, 0, HW), ("dq", 0, HW), ("dqi", 0, HW),
                     ("dkv", 0, HEAD_DIM), ("dkiwg", 0, HEAD_DIM), ("dkiwg", 0, LANE))
_PROJ_T_OUTPUTS = (("sbk", 0, HW), ("sbv", 0, HW), ("dkv", 0, HEAD_DIM), ("dkv", HEAD_DIM, HEAD_DIM),
                   ("dkiwg", 0, HEAD_DIM))


def _relayout_w_in(w_in):
    d = w_in.shape[0]
    widths = (HW, HW, HW, HW, HW, N_HEADS * DV_B, N_HEADS * DV_B, GATE_RANK,
              HW, HEAD_DIM, HEAD_DIM, HW, HEAD_DIM, N_HEADS)
    offs = np.cumsum((0,) + widths)
    col = lambda i: w_in[:, offs[i]:offs[i + 1]]
    pad = lambda a, w: jnp.pad(a, ((0, 0), (0, w - a.shape[1])))
    parts = [col(0), col(1), col(2), col(3), col(4), col(5), col(6),
             col(8), col(11), jnp.concatenate([col(9), col(10)], axis=1),
             pad(jnp.concatenate([col(12), col(13), col(7)], axis=1), LANE)]
    out = jnp.concatenate(parts, axis=1)
    assert out.shape == (d, _PROJ_WIDTH)
    return out


def _proj_kernel(x_ref, sh_ref, sc_ref, g_ref, w_ref, wgate_ref, bgate_ref, *refs, row_outs, t_outs, n_prev):
    prev_refs = refs[:n_prev]
    row_refs = refs[n_prev:n_prev + len(row_outs)]
    t_refs = refs[n_prev + len(row_outs):]
    for o_ref, prev_ref in zip(t_refs, prev_refs):
        o_ref[:prev_ref.shape[0]] = prev_ref[...]
    h = (_rms_rows(x_ref[...], g_ref[...]) * (1.0 + sc_ref[...]) + sh_ref[...]).astype(BF16)
    for gi, (name, _) in enumerate(_PROJ_GROUPS):
        p = _dot(h, w_ref[:, _PROJ_OFFSETS[gi]:_PROJ_OFFSETS[gi + 1]])
        groups = {name: p}
        if name == "dkiwg":
            pre = _dot(p.astype(BF16), wgate_ref[...]) + bgate_ref[...]
            groups[_GATE] = _log_sigmoid(pre) * (1.0 / GATE_TAU)
        for o_ref, (out_group, start, width) in zip(row_refs, row_outs):
            if out_group in groups:
                o_ref[...] = groups[out_group][:, start:start + width]
        for o_ref, (out_group, start, width) in zip(t_refs, t_outs):
            if out_group in groups:
                o_ref[o_ref.shape[0] - 1] = groups[out_group].T[start:start + width, :]


def _proj_call(x, mod, gains, w_in, w_gate, b_gate, tm, stacked=None):
    rows, d = x.shape
    in_specs = [pl.BlockSpec((tm, d), lambda i: (i, 0)),
                mod.spec(3, tm, d), mod.spec(4, tm, d), _gain_spec(2, d),
                pl.BlockSpec((d, _PROJ_WIDTH), lambda i: (0, 0)),
                pl.BlockSpec((LANE, HW), lambda i: (0, 0)),
                pl.BlockSpec((1, HW), lambda i: (0, 0))]
    args = [x, mod.arr, mod.arr, gains, w_in, w_gate, b_gate]
    if stacked is None:
        row_outs, t_outs, prev = _PROJ_OUTPUTS, (), ()
        t_specs, t_shapes = [], []
    else:
        layer, n_seq, prev = stacked
        row_outs, t_outs = _PROJ_ROW_OUTPUTS, _PROJ_T_OUTPUTS
        t = rows // n_seq
        tps = t // tm
        tile = lambda n, w: pl.BlockSpec((n, None, w, tm), lambda i: (0, i // tps, 0, i % tps))
        t_specs = [tile(layer + 1, w) for _, _, w in t_outs]
        t_shapes = [jax.ShapeDtypeStruct((layer + 1, n_seq, w, t), F32) for _, _, w in t_outs]
        prev = () if prev is None else tuple(prev)
        in_specs += [tile(layer, w) for _, _, w in t_outs[:len(prev)]]
    outs = pl.pallas_call(
        functools.partial(_proj_kernel, row_outs=row_outs, t_outs=t_outs, n_prev=len(prev)),
        grid=(rows // tm,),
        in_specs=in_specs,
        out_specs=[pl.BlockSpec((tm, w), lambda i: (i, 0)) for _, _, w in row_outs] + t_specs,
        out_shape=[jax.ShapeDtypeStruct((rows, w), F32) for _, _, w in row_outs] + t_shapes,
        compiler_params=_cparams(1),
        name="mix_in_proj",
    )(*args, *prev)
    return outs[:len(row_outs)], outs[len(row_outs):]


def _sb_kernel(q_ref, k_ref, v_ref, *rest, tq, tk, past, new_apart):
    if new_apart:
        kn_ref, vn_ref, u_ref, o_ref, acc_scr, c_scr = rest
    else:
        u_ref, o_ref, acc_scr, c_scr = rest
    i = pl.program_id(1)
    head = _head_lane_id(HW, HEAD_DIM)
    q = q_ref[...] * (HEAD_DIM ** -0.5)
    q_st = jnp.concatenate([jnp.where(head == h, q, 0.0) for h in range(N_HEADS)], axis=0).astype(BF16)
    qpos = past + i * tq + lax.broadcasted_iota(jnp.int32, (tq, 1), 0)
    last_blk = (past + (i + 1) * tq - 2) // tk
    acc_scr[...] = jnp.zeros_like(acc_scr)
    c_scr[...] = jnp.zeros_like(c_scr)
    u = u_ref[...]
    rows = N_HEADS * tq

    heads_per_tile = LANE // HEAD_DIM
    low_half = [(lax.broadcasted_iota(jnp.int32, (1, LANE), 1) // HEAD_DIM) == r
                for r in range(heads_per_tile)]

    def body(n, carry, *, masked):
        j = last_blk - n
        k0 = pl.multiple_of(j * tk, tk)
        if new_apart and masked:
            cols = pl.ds(pl.multiple_of(k0 - past, tk), tk)
            kb, vb = kn_ref[:, cols].astype(BF16), vn_ref[:, cols].astype(BF16)
        else:
            kb = k_ref[:, pl.ds(k0, tk)].astype(BF16)
            vb = v_ref[:, pl.ds(k0, tk)].astype(BF16)
        z = _dot(q_st, kb).reshape(N_HEADS, tq, tk) * _LOG2_E
        soft = jnp.log2(1.0 + jnp.exp2(-jnp.abs(z)))
        log_beta = jnp.minimum(z, 0.0) - soft
        log_keep = log_beta - z
        if masked:
            kpos = k0 + lax.broadcasted_iota(jnp.int32, (1, tk), 1)
            valid = (kpos < qpos)[None]
            log_keep = jnp.where(valid, log_keep, 0.0)
        newer = _dot(log_keep.astype(BF16).reshape(rows, tk), u).reshape(N_HEADS, tq, tk)
        c = c_scr[...]
        w = jnp.exp2(log_beta + newer + c)
        if masked:
            w = jnp.where(valid, w, 0.0)
        pv = _dot_nt(w.astype(BF16).reshape(rows, tk), vb).reshape(N_HEADS, tq, HW)
        tiles = []
        for t0 in range(HW // LANE):
            sl = slice(t0 * LANE, (t0 + 1) * LANE)
            col = pv[t0 * heads_per_tile][:, sl]
            for r in range(1, heads_per_tile):
                col = jnp.where(low_half[r], pv[t0 * heads_per_tile + r][:, sl], col)
            tiles.append(col)
        acc_scr[...] += jnp.concatenate(tiles, axis=1)
        c_scr[...] = c + newer[:, :, 0:1] + log_keep[:, :, 0:1]
        return carry

    n_masked = last_blk + 1 - jnp.minimum((past + i * tq) // tk, last_blk + 1)
    lax.fori_loop(0, n_masked, functools.partial(body, masked=True), 0)
    _fori_by_two(n_masked, last_blk + 1, functools.partial(body, masked=False), 0)
    o_ref[...] = acc_scr[...]


def _sb_call(q, k_t, v_t, layer, past, tq, tk, new_kv=None):
    b, t, _ = q.shape
    u = (np.arange(tk)[:, None] > np.arange(tk)[None, :]).astype(np.float32)
    kv_spec = pl.BlockSpec((None, None, HW, k_t.shape[3]), lambda bi, i: (layer, bi, 0, 0))
    in_specs = [pl.BlockSpec((None, tq, HW), lambda bi, i: (bi, i, 0)), kv_spec, kv_spec]
    args = [q, k_t, v_t]
    if new_kv is not None:
        assert past % tk == 0 and k_t.shape[3] == past and t <= tk
        in_specs += [pl.BlockSpec((None, HW, new_kv[0].shape[2]), lambda bi, i: (bi, 0, 0))] * 2
        args += list(new_kv)
    return pl.pallas_call(
        functools.partial(_sb_kernel, tq=tq, tk=tk, past=past, new_apart=new_kv is not None),
        grid=(b, t // tq),
        in_specs=in_specs + [pl.BlockSpec((tk, tk), lambda bi, i: (0, 0))],
        out_specs=pl.BlockSpec((None, tq, HW), lambda bi, i: (bi, i, 0)),
        out_shape=jax.ShapeDtypeStruct((b, t, HW), F32),
        scratch_shapes=[pltpu.VMEM((tq, HW), F32), pltpu.VMEM((N_HEADS, tq, 1), F32)],
        compiler_params=_cparams(2),
        name="stick_breaking_attn",
    )(*args, jnp.asarray(u, BF16))


def _gla_tables(c):
    n_lev = int(math.log2(c))
    assert 1 << n_lev == c
    t = np.arange(c)
    rows = []
    for lev in range(1, n_lev + 1):
        blk = c >> (lev - 1)
        ref = (t // blk) * blk + blk // 2 - 1
        lo, hi = np.minimum(t, ref), np.maximum(t, ref)
        rows.append(((t[None, :] > lo[:, None]) & (t[None, :] <= hi[:, None])).astype(np.float32))
    rows.append((t[None, :] <= t[:, None]).astype(np.float32))
    rows.append((t[None, :] > t[:, None]).astype(np.float32))
    table = np.concatenate(rows, axis=0)
    lvl = np.full((c, c), n_lev + 1, np.int32)
    for lev in range(1, n_lev + 1):
        blk = c >> (lev - 1)
        half = blk // 2
        same = (t[:, None] // blk) == (t[None, :] // blk)
        pair = same & ((t[:, None] % blk) >= half) & ((t[None, :] % blk) < half)
        lvl[pair] = lev
    lvl[t, t] = 0
    return np.concatenate([table, table], axis=1), np.tile(lvl.T, (1, N_HEADS)), n_lev


def _gla_kernel(q_seqs, k_seqs, v_seqs, r_seqs, la_seqs, s0_ref, tab_ref, lvl_ref, g_ref,
                o_seqs, s_ref, s_seqs, *, c, n_lev, n_chunks, n_steps):
    j = pl.program_id(1)

    @pl.when(j == 0)
    def _():
        s_seqs[...] = s0_ref[...]

    head = _head_lane_id(HW, HEAD_DIM)
    tab = tab_ref[...]
    lvl = lvl_ref[...]

    def chunk(ci, carry):
        for bi in range(q_seqs.shape[0]):
            seq_chunk(bi, pl.ds(pl.multiple_of(ci * c, c), c))
        return carry

    def seq_chunk(bi, rows):
        q_ref, k_ref, v_ref, r_ref, la_ref, o_ref, s_scr = (
            a.at[bi] for a in (q_seqs, k_seqs, v_seqs, r_seqs, la_seqs, o_seqs, s_seqs))
        q = q_ref[rows, :] * (HEAD_DIM ** -0.5)
        k = k_ref[rows, :]
        decays = jnp.exp(_dot(tab, jnp.concatenate(_split_bf16(la_ref[rows, :]), axis=0)))
        by_head = lambda a: jnp.concatenate([jnp.where(head == h, a, 0.0) for h in range(N_HEADS)],
                                            axis=0).astype(BF16)
        att_t = jnp.zeros((c, N_HEADS * c), F32)
        for lev in range(n_lev + 1):
            if lev == 0:
                ql, kl = q, k
            else:
                e = decays[(lev - 1) * c:lev * c]
                ql, kl = q * e, k * e
            a_t = _dot_nt(kl.astype(BF16), by_head(ql))
            att_t = jnp.where(lvl == lev, a_t, att_t)
        q_decay = decays[n_lev * c:(n_lev + 1) * c]
        k_out = (k * decays[(n_lev + 1) * c:(n_lev + 2) * c]).astype(BF16)
        s = s_scr[...]
        v_all = v_ref[rows, :].astype(BF16)
        intra = _dot_tn(att_t.astype(BF16), v_all)
        inter = _dot(by_head(q * q_decay), s.astype(BF16))
        for h in range(N_HEADS):
            o = intra[h * c:(h + 1) * c, h * DV_B:(h + 1) * DV_B] + inter[h * c:(h + 1) * c]
            y = _rms_rows(o, g_ref[:, h * DV_B:(h + 1) * DV_B])
            o_ref[rows, h * DV_B:(h + 1) * DV_B] = y * _silu(r_ref[rows, h * DV_B:(h + 1) * DV_B])
        kv = _dot_tn(k_out, v_all)
        ds = jnp.concatenate([kv[h * HEAD_DIM:(h + 1) * HEAD_DIM, h * DV_B:(h + 1) * DV_B]
                              for h in range(N_HEADS)], axis=0)
        chunk_decay = jnp.broadcast_to(q_decay[c - 1:c, :], (DV_B, HW)).T
        s_scr[...] = chunk_decay * s + ds

    lax.fori_loop(0, n_chunks, chunk, 0)

    @pl.when(j == n_steps - 1)
    def _():
        s_ref[...] = s_seqs[...]


def _gla_call(q, k, v, r, la, s0, g_gla, c):
    b, t, _ = q.shape
    vw = N_HEADS * DV_B
    tc = _pick_tile(t, 512, c)
    n_steps = t // tc
    nb = max(n for n in (4, 2, 1) if b % n == 0)
    table, lvl, n_lev = _gla_tables(c)
    qk_spec = pl.BlockSpec((nb, tc, HW), lambda bi, j: (bi, j, 0))
    vr_spec = pl.BlockSpec((nb, tc, vw), lambda bi, j: (bi, j, 0))
    st_spec = pl.BlockSpec((nb, HW, DV_B), lambda bi, j: (bi, 0, 0))
    const = lambda shape: pl.BlockSpec(shape, lambda bi, j: (0, 0))
    return pl.pallas_call(
        functools.partial(_gla_kernel, c=c, n_lev=n_lev, n_chunks=tc // c, n_steps=n_steps),
        grid=(b // nb, n_steps),
        in_specs=[qk_spec, qk_spec, vr_spec, vr_spec, qk_spec, st_spec,
                  const(table.shape), const(lvl.shape), const((1, vw))],
        out_specs=[vr_spec, st_spec],
        out_shape=[jax.ShapeDtypeStruct((b, t, vw), F32), jax.ShapeDtypeStruct((b, HW, DV_B), F32)],
        scratch_shapes=[pltpu.VMEM((nb, HW, DV_B), F32)],
        compiler_params=_cparams(2),
        name="gated_linear_attn",
    )(q, k, v, r, la, s0, jnp.asarray(table, BF16), jnp.asarray(lvl), g_gla)


_COUNT_ROWS = 32


def _key_to_f32(key):
    key = jnp.maximum(key, _KEY_NEG_INF)
    bits = jnp.where(key < 0, key ^ 0x7FFFFFFF, key)
    return lax.bitcast_convert_type(bits, F32)


def _dsa_kernel(q_ref, qi_ref, wi_ref, kc_ref, vc_ref, ki_ref, eye_ref, tri_ref, o_ref,
                sc_scr, hi_scr, lo_scr, lg_scr, vct_scr, acc_scr, *, tq, tk, past, l_real, n_sel):
    i = pl.program_id(1)
    scale = HEAD_DIM ** -0.5
    eye = eye_ref[...]
    l_pad = sc_scr.shape[0]

    @pl.when(i == 0)
    def _():
        vct_scr[...] = vc_ref[...].astype(BF16)

    q_t = _dot_nt(eye, (q_ref[...] * scale).astype(BF16)).astype(BF16)
    qi_t = _dot_nt(eye, (qi_ref[...] * scale).astype(BF16)).astype(BF16)
    w_t = wi_ref[...].T * (N_HEADS ** -0.5)
    qpos = past + i * tq + lax.broadcasted_iota(jnp.int32, (1, tq), 1)
    chunk_shift = int(math.log2(CHUNK))
    qchunk = lax.shift_right_logical(qpos, chunk_shift)
    k_end = ((past + (i + 1) * tq - 1) // CHUNK + 1) * CHUNK
    n_blk = jnp.minimum((k_end + tk - 1) // tk, l_pad // tk)

    def rows_of(j):
        return pl.ds(pl.multiple_of(j * tk, tk), tk)

    def admissible(j):
        kpos = j * tk + lax.broadcasted_iota(jnp.int32, (tk, 1), 0)
        return (lax.shift_right_logical(kpos, chunk_shift) <= qchunk) & (kpos < l_real)

    def score_blk(j, carry):
        kib = ki_ref[rows_of(j), :].astype(BF16)
        s = jnp.zeros((tk, tq), F32)
        for h in range(N_HEADS):
            sh = _dot(kib, qi_t[h * HEAD_DIM:(h + 1) * HEAD_DIM])
            s = s + w_t[_WI_LANE + h:_WI_LANE + h + 1] * jnp.maximum(sh, 0.0)
        s = jnp.where(admissible(j), jnp.where(s == 0.0, 0.0, s), _NEG_INF)
        sc_scr[rows_of(j), :] = s
        bits = lax.bitcast_convert_type(s, jnp.int32)
        key = jnp.where(bits < 0, bits ^ 0x7FFFFFFF, bits)
        hi_scr[rows_of(j), :] = lax.shift_right_arithmetic(key, 16).astype(jnp.int16)
        lo_scr[rows_of(j), :] = ((key & 0xFFFF) - 32768).astype(jnp.int16)
        return carry

    _fori_by_two(0, n_blk, score_blk, 0)

    def count(ref, pred, one, zero):
        def blk(j, acc):
            m = jnp.where(pred(ref[rows_of(j), :]), one, zero).reshape(tk // _COUNT_ROWS, _COUNT_ROWS, tq)
            for r in range(tk // _COUNT_ROWS):
                acc = acc + m[r]
            return acc
        acc = _fori_by_two(0, n_blk, blk, jnp.zeros((_COUNT_ROWS, tq), one.dtype))
        return jnp.sum(acc.astype(jnp.int32), axis=0, keepdims=True)

    one16, zero16 = jnp.int16(1), jnp.int16(0)

    def bisect16(ref, want):
        def bit(it, tau):
            cand = tau + lax.shift_left(jnp.int32(1), 15 - it)
            c16 = cand.astype(jnp.int16)
            return jnp.where(count(ref, lambda a: a >= c16, one16, zero16) >= want, cand, tau)
        return lax.fori_loop(0, 16, bit, jnp.full((1, tq), -32768, jnp.int32))

    hi_star = bisect16(hi_scr, n_sel)
    h16 = hi_star.astype(jnp.int16)
    want_lo = n_sel - count(hi_scr, lambda a: a > h16, one16, zero16)

    def mask_lo(j, carry):
        lo_scr[rows_of(j), :] = jnp.where(hi_scr[rows_of(j), :] == h16, lo_scr[rows_of(j), :],
                                          jnp.int16(-32768))
        return carry

    lax.fori_loop(0, n_blk, mask_lo, 0)
    lo_star = bisect16(lo_scr, want_lo)
    thr_fast = _key_to_f32(lax.shift_left(hi_star, 16) + (lo_star + 32768))

    f_one, f_zero = jnp.float32(1.0), jnp.float32(0.0)
    count_ge = lambda t: count(sc_scr, lambda a: a >= t, f_one, f_zero)
    count_gt = lambda t: count(sc_scr, lambda a: a > t, f_one, f_zero)
    n_gt = count_gt(thr_fast)
    settled = jnp.min(jnp.where((n_gt < n_sel) & (count_ge(thr_fast) >= n_sel), 1, 0)) == 1

    def float_bisection():
        def value_bit(it, tau):
            cand = tau + lax.shift_left(jnp.int32(1), 31 - it)
            return jnp.where(count_ge(_key_to_f32(cand)) >= n_sel, cand, tau)
        t = _key_to_f32(lax.fori_loop(0, 32, value_bit, jnp.full((1, tq), _INT_MIN, jnp.int32)))
        return t, count_gt(t)

    thr, n_gt = lax.cond(settled, lambda: (thr_fast, n_gt), float_bisection)
    need = (n_sel - n_gt).astype(F32)

    tri = tri_ref[...]
    fold = lambda a, op: op(a.reshape(tk // SUBLANE, SUBLANE, tq), axis=0)
    parts = lambda v: tuple(jnp.full((SUBLANE, tq), v, F32) for _ in range(N_HEADS))

    def logit_blk(j, carry):
        m_parts, ties_before = carry
        s = sc_scr[rows_of(j), :]
        tie = s == thr
        rank = _dot(tri, jnp.where(tie, 1.0, 0.0).astype(BF16)) + ties_before
        sel = admissible(j) & ((s > thr) | (tie & (rank <= need)))
        bias = jnp.where(sel, 0.0, _NEG_INF)
        kcb = kc_ref[rows_of(j), :].astype(BF16)
        new_parts = []
        for h in range(N_HEADS):
            lg = _dot(kcb, q_t[h * HEAD_DIM:(h + 1) * HEAD_DIM]) + bias
            lg_scr[h, rows_of(j), :] = lg
            new_parts.append(jnp.maximum(m_parts[h], fold(lg, jnp.max)))
        return tuple(new_parts), rank[tk - 1:tk, :]

    m_parts, _ = _fori_by_two(0, n_blk, logit_blk, (parts(_NEG_INF), jnp.zeros((1, tq), F32)))
    m_use = []
    for h in range(N_HEADS):
        m = jnp.max(m_parts[h], axis=0, keepdims=True)
        m_use.append(jnp.where(m == _NEG_INF, 0.0, m))
    acc_scr[...] = jnp.zeros_like(acc_scr)

    def pv_blk(j, l_parts):
        vct = vct_scr[:, rows_of(j)]
        new_parts = []
        for h in range(N_HEADS):
            hd = slice(h * HEAD_DIM, (h + 1) * HEAD_DIM)
            p = jnp.exp(lg_scr[h, rows_of(j), :] - m_use[h])
            new_parts.append(l_parts[h] + fold(p, jnp.sum))
            acc_scr[hd, :] += _dot(vct, p.astype(BF16))
        return tuple(new_parts)

    l_parts = _fori_by_two(0, n_blk, pv_blk, parts(0.0))
    out_t = jnp.concatenate(
        [acc_scr[h * HEAD_DIM:(h + 1) * HEAD_DIM, :] / jnp.sum(l_parts[h], axis=0, keepdims=True)
         for h in range(N_HEADS)], axis=0)
    o_ref[...] = out_t.T


def _dsa_call(q, qi, wi, kc, vc_t, ki, layer, past, l_real, tq, tk):
    b, t, _ = q.shape
    l_pad = kc.shape[1]
    n_sel = min(TOPK_MAX, l_real // 4)
    assert tk >= n_sel and l_pad % tk == 0 and tk % _COUNT_ROWS == 0
    row = lambda w: pl.BlockSpec((None, tq, w), lambda bi, i: (bi, i, 0))
    full = pl.BlockSpec((None, l_pad, HEAD_DIM), lambda bi, i: (bi, 0, 0))
    full_t = pl.BlockSpec((None, None, HEAD_DIM, l_pad), lambda bi, i: (layer, bi, 0, 0))
    const = lambda n: pl.BlockSpec((n, n), lambda bi, i: (0, 0))
    eye = np.eye(HW, dtype=np.float32)
    tri = (np.arange(tk)[:, None] >= np.arange(tk)[None, :]).astype(np.float32)
    return pl.pallas_call(
        functools.partial(_dsa_kernel, tq=tq, tk=tk, past=past, l_real=l_real, n_sel=n_sel),
        grid=(b, t // tq),
        in_specs=[row(HW), row(HW), row(LANE), full, full_t, full, const(HW), const(tk)],
        out_specs=row(HW),
        out_shape=jax.ShapeDtypeStruct((b, t, HW), F32),
        scratch_shapes=[pltpu.VMEM((l_pad, tq), F32),
                        pltpu.VMEM((l_pad, tq), jnp.int16), pltpu.VMEM((l_pad, tq), jnp.int16),
                        pltpu.VMEM((N_HEADS, l_pad, tq), F32),
                        pltpu.VMEM((HEAD_DIM, l_pad), BF16), pltpu.VMEM((HW, tq), F32)],
        compiler_params=_cparams(2),
        name="indexer_sparse_attn",
    )(q, qi, wi, kc, vc_t, ki, jnp.asarray(eye, BF16), jnp.asarray(tri, BF16))


def _pad_rows(a, n):
    return a if n == a.shape[1] else jnp.pad(a, ((0, 0), (0, n - a.shape[1]), (0, 0)))


def _pad_last(a, n):
    return a if n == a.shape[-1] else jnp.pad(a, [(0, 0)] * (a.ndim - 1) + [(0, n - a.shape[-1])])


def _encoder_layer(x, n_seq, t, mod, past, w, stacked=None):
    rows, d = x.shape
    tm = _pick_tile(t, 512, SUBLANE) if not mod.per_token else rows
    tm_ffn = _pick_tile(t, 1024, SUBLANE) if not mod.per_token else rows
    x = _ffn_call(x, mod, 0, w["gains"], 0, w["w_f1_in"], w["w_f1_out"], w["layer"], tm_ffn)
    seq = lambda a: a.reshape(n_seq, t, a.shape[-1])
    p_len = 0 if past is None else past["k_sb_t"].shape[3]
    l_real = p_len + t
    tk = 256 if l_real >= 256 else LANE
    l_pad = -(-l_real // tk) * tk
    tq = _pick_tile(t, 256, SUBLANE)
    proj = functools.partial(_proj_call, x, mod, w["gains"], w["w_in"], w["w_gate"], w["b_gate"], tm)
    if stacked is None:
        (sbq, sbk, sbv, gq, gk, gv, gr, la, dq, dqi, dkc, dvc, dki, dwi), _ = proj()
        lyr = past["layer"]
        keys_last = lambda new: jnp.swapaxes(seq(new), 1, 2)
        join_t = lambda old_t, new: _pad_last(jnp.concatenate([old_t, keys_last(new)], axis=2), l_pad)[None]
        join_r = lambda old, new: _pad_rows(jnp.concatenate([old, seq(new)], axis=1), l_pad)
        if p_len % tk == 0 and t <= tk:
            new_kv = tuple(_pad_last(keys_last(a), tk) for a in (sbk, sbv))
            o_a = _sb_call(seq(sbq), past["k_sb_t"], past["v_sb_t"], lyr, p_len, tq, tk, new_kv=new_kv)
        else:
            o_a = _sb_call(seq(sbq), join_t(past["k_sb_t"][lyr], sbk), join_t(past["v_sb_t"][lyr], sbv),
                           0, p_len, tq, tk)
        vc_t = join_t(past["v_dsa_t"], dvc)
        kc_rows, ki_rows = join_r(past["k_dsa"], dkc), join_r(past["k_idx"], dki)
        layer = 0
        new_rows = (seq(sbk).reshape(n_seq, t, N_HEADS, HEAD_DIM), seq(sbv).reshape(n_seq, t, N_HEADS, HEAD_DIM),
                    seq(dkc), seq(dvc), seq(dki))
    else:
        assert past is None and l_pad == t and t % tm == 0 and tm % LANE == 0
        layer, prev = stacked
        (sbq, gq, gk, gv, gr, la, dq, dqi, dkc, dki, dwi), new_rows = proj((layer, n_seq, prev))
        k_t, v_t, _, vc_t, _ = new_rows
        kc_rows, ki_rows = seq(dkc), seq(dki)
        o_a = _sb_call(seq(sbq), k_t, v_t, layer, p_len, tq, tk)

    c = CHUNK if t % CHUNK == 0 else t
    s0 = jnp.zeros((n_seq, HW, DV_B), F32) if past is None else past["gla"]
    o_b, s_new = _gla_call(seq(gq), seq(gk), seq(gv), seq(gr), seq(la), s0, w["g_gla"], c)

    t_c = t if t % LANE == 0 else -(-t // LANE) * LANE
    tq_c = _pick_tile(t_c, 256, LANE)
    qpad = lambda a: _pad_rows(seq(a), t_c)
    o_c = _dsa_call(qpad(dq), qpad(dqi), qpad(dwi), kc_rows, vc_t, ki_rows, layer, p_len, l_real, tq_c, tk)
    o_c = o_c if t_c == t else o_c[:, :t]

    flat = lambda a: a.reshape(rows, a.shape[-1])
    x = _ffn_call(x, mod, 6, w["gains"], 4, w["w_f2_in"], w["w_f2_out"], w["layer"], tm_ffn,
                  mix=(flat(o_a), flat(o_b), flat(o_c), w["w_out"]))
    return x, new_rows, s_new.reshape(n_seq, N_HEADS, HEAD_DIM, DV_B)


def kernel(x_prompt, x_sample, cache_k_sb, cache_v_sb, cache_k_dsa, cache_v_dsa, cache_k_idx, state_gla,
           c_prompt, c_sample, w_ada, b_ada, norm_gains, w_ffn1_in, w_ffn1_out, w_ffn2_in, w_ffn2_out,
           w_in, w_gla_gate, b_gla_gate, gla_norm, w_out):
    bp, tp, d = x_prompt.shape
    bs, ts, _ = x_sample.shape
    depth = w_ada.shape[0]
    p_len = cache_k_sb.shape[2]

    mods = _mod_call(jnp.concatenate([c_prompt, c_sample], axis=0), w_ada, b_ada)
    xp = x_prompt.reshape(bp * tp, d)
    xs = x_sample.reshape(bs * ts, d)
    ffn_w = [a.astype(BF16) for a in (w_ffn1_in, w_ffn1_out, w_ffn2_in, w_ffn2_out)]
    sb_cache_t = [jnp.transpose(c, (0, 1, 3, 4, 2)).reshape(depth, bs, HW, p_len) for c in (cache_k_sb, cache_v_sb)]
    shared_p, gla_p, acc_s = None, [], []
    for l in range(depth):
        w_gate = jnp.pad(w_gla_gate[l], ((_GB_LANE, LANE - GATE_RANK - _GB_LANE), (0, 0))).astype(BF16)
        w = dict(gains=norm_gains[l].reshape(-1, 1, d),
                 layer=l, w_f1_in=ffn_w[0], w_f1_out=ffn_w[1], w_f2_in=ffn_w[2], w_f2_out=ffn_w[3],
                 w_in=_relayout_w_in(w_in[l]).astype(BF16), w_gate=w_gate,
                 b_gate=b_gla_gate[l].reshape(1, HW), g_gla=gla_norm[l].reshape(1, N_HEADS * DV_B),
                 w_out=w_out[l].astype(BF16))
        mod_p = _Mod(mods[l, :bp].reshape(bp * N_MOD, 1, d), False, tp)
        mod_s_rows = jnp.repeat(mods[l, bp:].reshape(bs, N_MOD, d), ts, axis=0)
        mod_s = _Mod(jnp.transpose(mod_s_rows, (1, 0, 2)), True)
        past = dict(layer=l, k_sb_t=sb_cache_t[0], v_sb_t=sb_cache_t[1],
                    k_dsa=cache_k_dsa[l], v_dsa_t=jnp.swapaxes(cache_v_dsa[l], 1, 2), k_idx=cache_k_idx[l],
                    gla=state_gla[l].reshape(bs, HW, DV_B))
        xp, shared_p, s_p = _encoder_layer(xp, bp, tp, mod_p, None, w, stacked=(l, shared_p))
        xs, rows_s, s_s = _encoder_layer(xs, bs, ts, mod_s, past, w)
        gla_p.append(s_p)
        acc_s.append(rows_s + (s_s,))
    k_t, v_t, kc_t, vc_t, ki_t = shared_p
    heads_out = lambda a: jnp.transpose(a.reshape(depth, bp, N_HEADS, HEAD_DIM, tp), (0, 1, 4, 2, 3))
    tokens_out = lambda a: jnp.swapaxes(a, 2, 3)
    field = lambda acc, i: jnp.stack([r[i] for r in acc], axis=0)
    return (xp.reshape(bp, tp, d), xs.reshape(bs, ts, d),
            heads_out(k_t), heads_out(v_t), tokens_out(kc_t), tokens_out(vc_t), tokens_out(ki_t),
            jnp.stack(gla_p, axis=0), *(field(acc_s, i) for i in range(6)))
```

```python
import functools
import math

import numpy as np
import jax
import jax.numpy as jnp
from jax import lax
from jax.experimental import pallas as pl
from jax.experimental.pallas import tpu as pltpu

F32 = jnp.float32
BF16 = jnp.bfloat16

CHUNK = 64
N_HEADS = 4
HEAD_DIM = 64
HW = N_HEADS * HEAD_DIM
DV_B = 128
GATE_RANK = 16
GATE_TAU = 16.0
TOPK_MAX = 256
EPS = 1e-6
MACARON_W = 0.5
N_MOD = 9

LANE = 128
SUBLANE = 8
VMEM_BYTES = 64 * 1024 * 1024
VMEM_LIMIT_BYTES = VMEM_BYTES * 7 // 8

_NEG_INF = float("-inf")
_KEY_NEG_INF = -2**31 + 0x7FFFFF
_INT_MIN = -2**31


def _pick_tile(n, target, mult):
    if n <= target:
        return n
    t = (target // mult) * mult
    while t >= mult:
        if n % t == 0:
            return t
        t -= mult
    return n


def _cparams(n_axes):
    return pltpu.CompilerParams(dimension_semantics=("arbitrary",) * n_axes,
                                vmem_limit_bytes=VMEM_LIMIT_BYTES)


def _rms_rows(x, g):
    ms = jnp.mean(x * x, axis=-1, keepdims=True)
    return x * lax.rsqrt(ms + EPS) * g


def _silu(x):
    return x * (1.0 / (1.0 + jnp.exp(-x)))


def _log_sigmoid(x):
    return jnp.minimum(x, 0.0) - jnp.log(1.0 + jnp.exp(-jnp.abs(x)))


_LOG2_E = 1.4426950408889634


def _neg_abs(x):
    bits = lax.bitcast_convert_type(x, jnp.int32) | _INT_MIN
    return lax.bitcast_convert_type(bits, F32)


def _dot(a, b):
    return jnp.dot(a, b, preferred_element_type=F32)


def _dot_nt(a, b):
    return lax.dot_general(a, b, (((1,), (1,)), ((), ())), preferred_element_type=F32)


def _dot_tn(a, b):
    return lax.dot_general(a, b, (((0,), (0,)), ((), ())), preferred_element_type=F32)


def _fori_by_two(lo, hi, body, init):
    n = hi - lo

    def two(p, carry):
        j = lo + 2 * p
        return body(j + 1, body(j, carry))

    carry = lax.fori_loop(0, lax.shift_right_logical(n, 1), two, init)
    return lax.cond((n & 1) == 1, lambda c: body(hi - 1, c), lambda c: c, carry)


def _split_bf16(x):
    hi = x.astype(BF16)
    lo = (x - hi.astype(F32)).astype(BF16)
    return hi, lo


def _head_lane_id(width, per_head):
    return lax.broadcasted_iota(jnp.int32, (1, width), 1) // per_head


def _mod_kernel(c_ref, w_ref, b_ref, o_ref):
    a = _silu(c_ref[...]).astype(BF16)
    o_ref[...] = _dot(a, w_ref[...].astype(BF16)) + b_ref[...]


def _mod_call(c, w_ada, b_ada):
    depth, d, nd = w_ada.shape
    n = c.shape[0]
    tn = _pick_tile(nd, 1536, LANE)
    return pl.pallas_call(
        _mod_kernel,
        grid=(depth, nd // tn),
        in_specs=[pl.BlockSpec((n, d), lambda l, j: (0, 0)),
                  pl.BlockSpec((None, d, tn), lambda l, j: (l, 0, j)),
                  pl.BlockSpec((None, 1, tn), lambda l, j: (l, 0, j))],
        out_specs=pl.BlockSpec((None, n, tn), lambda l, j: (l, 0, j)),
        out_shape=jax.ShapeDtypeStruct((depth, n, nd), F32),
        compiler_params=_cparams(2),
        name="adaln_mod",
    )(c, w_ada, b_ada.reshape(depth, 1, nd))


class _Mod:
    def __init__(self, arr, per_token, seq_len=None):
        self.arr, self.per_token, self.seq_len = arr, per_token, seq_len

    def spec(self, k, tm, d):
        if self.per_token:
            return pl.BlockSpec((None, tm, d), lambda i, *_: (k, i, 0))
        assert self.seq_len % tm == 0
        tps = self.seq_len // tm
        return pl.BlockSpec((None, 1, d), lambda i, *_: ((i // tps) * N_MOD + k, 0, 0))


def _gain_spec(k, d):
    return pl.BlockSpec((None, 1, d), lambda i, *_: (k, 0, 0))


def _mix_residual(x, oa_ref, ob_ref, oc_ref, ga_ref, g_ref, w_ref):
    wa = HW
    wb = wa + N_HEADS * DV_B
    y = (_dot(oa_ref[...].astype(BF16), w_ref[0:wa, :])
         + _dot(ob_ref[...].astype(BF16), w_ref[wa:wb, :])
         + _dot(oc_ref[...].astype(BF16), w_ref[wb:wb + HW, :]))
    return x + ga_ref[...] * _rms_rows(y, g_ref[...])


class _Rows:
    def __init__(self, ref, r0, n, full):
        self.ref, self.rows = ref, (slice(r0, r0 + n) if ref.shape[0] == full else slice(None))

    def __getitem__(self, _):
        return self.ref[self.rows, :]


def _ffn_kernel(x_ref, *refs, f, tf, with_mix, n_split):
    tm = x_ref.shape[0]
    o_ref = refs[-1]
    for r0 in range(0, tm, tm // n_split):
        view = lambda ref: _Rows(ref, r0, tm // n_split, tm)
        x = x_ref[r0:r0 + tm // n_split, :]
        rest = refs[:-1]
        if with_mix:
            oa, ob, oc, ga2, g3, w_mix = rest[:6]
            x = _mix_residual(x, view(oa), view(ob), view(oc), view(ga2), g3, w_mix)
            rest = rest[6:]
        sh_ref, sc_ref, ga_ref, gin_ref, gout_ref, wi_ref, wo_ref = rest
        h = (_rms_rows(x, gin_ref[...]) * (1.0 + view(sc_ref)[...]) + view(sh_ref)[...]).astype(BF16)
        y = None
        for c0 in range(0, f, tf):
            gate = _dot(h, wi_ref[:, c0:c0 + tf])
            up = _dot(h, wi_ref[:, f + c0:f + c0 + tf])
            part = _dot((_silu(gate) * up).astype(BF16), wo_ref[c0:c0 + tf, :])
            y = part if y is None else y + part
        o_ref[r0:r0 + tm // n_split, :] = x + MACARON_W * view(ga_ref)[...] * _rms_rows(y, gout_ref[...])


def _ffn_call(x, mod, mod_k, gains, gain_k, w_in, w_out, layer, tm, mix=None):
    rows, d = x.shape
    f = w_out.shape[1]
    tf = _pick_tile(f, 1408, LANE)
    row = lambda w: pl.BlockSpec((tm, w), lambda i: (i, 0))
    resident = lambda shape: pl.BlockSpec((None,) + shape, lambda i: (layer, 0, 0),
                                          pipeline_mode=pl.Buffered(1))
    in_specs, args = [row(d)], [x]
    if mix is not None:
        oa, ob, oc, w_mix = mix
        in_specs += [row(oa.shape[1]), row(ob.shape[1]), row(oc.shape[1]), mod.spec(5, tm, d), _gain_spec(3, d),
                     pl.BlockSpec(w_mix.shape, lambda i: (0, 0), pipeline_mode=pl.Buffered(1))]
        args += [oa, ob, oc, mod.arr, gains, w_mix]
    in_specs += [mod.spec(mod_k, tm, d), mod.spec(mod_k + 1, tm, d), mod.spec(mod_k + 2, tm, d),
                 _gain_spec(gain_k, d), _gain_spec(gain_k + 1, d),
                 resident((d, 2 * f)), resident((f, d))]
    args += [mod.arr, mod.arr, mod.arr, gains, gains, w_in, w_out]
    return pl.pallas_call(
        functools.partial(_ffn_kernel, f=f, tf=tf, with_mix=mix is not None,
                          n_split=2 if tm % (2 * SUBLANE) == 0 else 1),
        grid=(rows // tm,),
        in_specs=in_specs,
        out_specs=row(d),
        out_shape=jax.ShapeDtypeStruct((rows, d), F32),
        compiler_params=_cparams(1),
        name="macaron_ffn",
    )(*args)


_PROJ_GROUPS = (("sbq", HW), ("sbk", HW), ("sbv", HW),
                ("gq", HW), ("gk", HW), ("gv", N_HEADS * DV_B), ("gr", N_HEADS * DV_B),
                ("dq", HW), ("dqi", HW), ("dkv", LANE), ("dkiwg", LANE))
_PROJ_OFFSETS = tuple(int(v) for v in np.cumsum([0] + [w for _, w in _PROJ_GROUPS]))
_PROJ_WIDTH = _PROJ_OFFSETS[-1]
_WI_LANE = HEAD_DIM
_GB_LANE = _WI_LANE + N_HEADS
_GATE = "gate"
_PROJ_OUTPUTS = (("sbq", 0, HW), ("sbk", 0, HW), ("sbv", 0, HW), ("gq", 0, HW), ("gk", 0, HW),
                 ("gv", 0, N_HEADS * DV_B), ("gr", 0, N_HEADS * DV_B), (_GATE, 0, HW),
                 ("dq", 0, HW), ("dqi", 0, HW), ("dkv", 0, HEAD_DIM), ("dkv", HEAD_DIM, HEAD_DIM),
                 ("dkiwg", 0, HEAD_DIM), ("dkiwg", 0, LANE))
_PROJ_ROW_OUTPUTS = (("sbq", 0, HW), ("gq", 0, HW), ("gk", 0, HW), ("gv", 0, N_HEADS * DV_B),
                     ("gr", 0, N_HEADS * DV_B), (_GATE, 0, HW), ("dq", 0, HW), ("dqi", 0, HW),
                     ("dkv", 0, HEAD_DIM), ("dkiwg", 0, HEAD_DIM), ("dkiwg", 0, LANE))
_PROJ_T_OUTPUTS = (("sbk", 0, HW), ("sbv", 0, HW), ("dkv", 0, HEAD_DIM), ("dkv", HEAD_DIM, HEAD_DIM),
                   ("dkiwg", 0, HEAD_DIM))


def _relayout_w_in(w_in):
    d = w_in.shape[0]
    widths = (HW, HW, HW, HW, HW, N_HEADS * DV_B, N_HEADS * DV_B, GATE_RANK,
              HW, HEAD_DIM, HEAD_DIM, HW, HEAD_DIM, N_HEADS)
    offs = np.cumsum((0,) + widths)
    col = lambda i: w_in[:, offs[i]:offs[i + 1]]
    pad = lambda a, w: jnp.pad(a, ((0, 0), (0, w - a.shape[1])))
    parts = [col(0), col(1), col(2), col(3), col(4), col(5), col(6),
             col(8), col(11), jnp.concatenate([col(9), col(10)], axis=1),
             pad(jnp.concatenate([col(12), col(13), col(7)], axis=1), LANE)]
    out = jnp.concatenate(parts, axis=1)
    assert out.shape == (d, _PROJ_WIDTH)
    return out


def _proj_kernel(x_ref, sh_ref, sc_ref, g_ref, w_ref, wgate_ref, bgate_ref, *refs, row_outs, t_outs, n_prev):
    prev_refs = refs[:n_prev]
    row_refs = refs[n_prev:n_prev + len(row_outs)]
    t_refs = refs[n_prev + len(row_outs):]
    for o_ref, prev_ref in zip(t_refs, prev_refs):
        o_ref[:prev_ref.shape[0]] = prev_ref[...]
    h = (_rms_rows(x_ref[...], g_ref[...]) * (1.0 + sc_ref[...]) + sh_ref[...]).astype(BF16)
    for gi, (name, _) in enumerate(_PROJ_GROUPS):
        p = _dot(h, w_ref[:, _PROJ_OFFSETS[gi]:_PROJ_OFFSETS[gi + 1]])
        groups = {name: p}
        if name == "dkiwg":
            pre = _dot(p.astype(BF16), wgate_ref[...]) + bgate_ref[...]
            groups[_GATE] = _log_sigmoid(pre) * (1.0 / GATE_TAU)
        for o_ref, (out_group, start, width) in zip(row_refs, row_outs):
            if out_group in groups:
                o_ref[...] = groups[out_group][:, start:start + width]
        for o_ref, (out_group, start, width) in zip(t_refs, t_outs):
            if out_group in groups:
                o_ref[o_ref.shape[0] - 1] = groups[out_group].T[start:start + width, :]


def _proj_call(x, mod, gains, w_in, w_gate, b_gate, tm, stacked=None):
    rows, d = x.shape
    in_specs = [pl.BlockSpec((tm, d), lambda i: (i, 0)),
                mod.spec(3, tm, d), mod.spec(4, tm, d), _gain_spec(2, d),
                pl.BlockSpec((d, _PROJ_WIDTH), lambda i: (0, 0)),
                pl.BlockSpec((LANE, HW), lambda i: (0, 0)),
                pl.BlockSpec((1, HW), lambda i: (0, 0))]
    args = [x, mod.arr, mod.arr, gains, w_in, w_gate, b_gate]
    if stacked is None:
        row_outs, t_outs, prev = _PROJ_OUTPUTS, (), ()
        t_specs, t_shapes = [], []
    else:
        layer, n_seq, prev = stacked
        row_outs, t_outs = _PROJ_ROW_OUTPUTS, _PROJ_T_OUTPUTS
        t = rows // n_seq
        tps = t // tm
        tile = lambda n, w: pl.BlockSpec((n, None, w, tm), lambda i: (0, i // tps, 0, i % tps))
        t_specs = [tile(layer + 1, w) for _, _, w in t_outs]
        t_shapes = [jax.ShapeDtypeStruct((layer + 1, n_seq, w, t), F32) for _, _, w in t_outs]
        prev = () if prev is None else tuple(prev)
        in_specs += [tile(layer, w) for _, _, w in t_outs[:len(prev)]]
    outs = pl.pallas_call(
        functools.partial(_proj_kernel, row_outs=row_outs, t_outs=t_outs, n_prev=len(prev)),
        grid=(rows // tm,),
        in_specs=in_specs,
        out_specs=[pl.BlockSpec((tm, w), lambda i: (i, 0)) for _, _, w in row_outs] + t_specs,
        out_shape=[jax.ShapeDtypeStruct((rows, w), F32) for _, _, w in row_outs] + t_shapes,
        compiler_params=_cparams(1),
        name="mix_in_proj",
    )(*args, *prev)
    return outs[:len(row_outs)], outs[len(row_outs):]


def _sb_kernel(q_ref, k_ref, v_ref, *rest, tq, tk, past, new_apart):
    if new_apart:
        kn_ref, vn_ref, u_ref, o_ref, acc_scr, c_scr = rest
    else:
        u_ref, o_ref, acc_scr, c_scr = rest
    i = pl.program_id(1)
    head = _head_lane_id(HW, HEAD_DIM)
    q = q_ref[...] * (HEAD_DIM ** -0.5)
    q_st = jnp.concatenate([jnp.where(head == h, q, 0.0) for h in range(N_HEADS)], axis=0).astype(BF16)
    qpos = past + i * tq + lax.broadcasted_iota(jnp.int32, (tq, 1), 0)
    last_blk = (past + (i + 1) * tq - 2) // tk
    acc_scr[...] = jnp.zeros_like(acc_scr)
    c_scr[...] = jnp.zeros_like(c_scr)
    u = u_ref[...]
    rows = N_HEADS * tq

    heads_per_tile = LANE // HEAD_DIM
    low_half = [(lax.broadcasted_iota(jnp.int32, (1, LANE), 1) // HEAD_DIM) == r
                for r in range(heads_per_tile)]

    def body(n, carry, *, masked):
        j = last_blk - n
        k0 = pl.multiple_of(j * tk, tk)
        if new_apart and masked:
            cols = pl.ds(pl.multiple_of(k0 - past, tk), tk)
            kb, vb = kn_ref[:, cols].astype(BF16), vn_ref[:, cols].astype(BF16)
        else:
            kb = k_ref[:, pl.ds(k0, tk)].astype(BF16)
            vb = v_ref[:, pl.ds(k0, tk)].astype(BF16)
        z = _dot(q_st, kb).reshape(N_HEADS, tq, tk) * _LOG2_E
        soft = jnp.log2(1.0 + jnp.exp2(-jnp.abs(z)))
        log_beta = jnp.minimum(z, 0.0) - soft
        log_keep = log_beta - z
        if masked:
            kpos = k0 + lax.broadcasted_iota(jnp.int32, (1, tk), 1)
            valid = (kpos < qpos)[None]
            log_keep = jnp.where(valid, log_keep, 0.0)
        newer = _dot(log_keep.astype(BF16).reshape(rows, tk), u).reshape(N_HEADS, tq, tk)
        c = c_scr[...]
        w = jnp.exp2(log_beta + newer + c)
        if masked:
            w = jnp.where(valid, w, 0.0)
        pv = _dot_nt(w.astype(BF16).reshape(rows, tk), vb).reshape(N_HEADS, tq, HW)
        tiles = []
        for t0 in range(HW // LANE):
            sl = slice(t0 * LANE, (t0 + 1) * LANE)
            col = pv[t0 * heads_per_tile][:, sl]
            for r in range(1, heads_per_tile):
                col = jnp.where(low_half[r], pv[t0 * heads_per_tile + r][:, sl], col)
            tiles.append(col)
        acc_scr[...] += jnp.concatenate(tiles, axis=1)
        c_scr[...] = c + newer[:, :, 0:1] + log_keep[:, :, 0:1]
        return carry

    n_masked = last_blk + 1 - jnp.minimum((past + i * tq) // tk, last_blk + 1)
    lax.fori_loop(0, n_masked, functools.partial(body, masked=True), 0)
    _fori_by_two(n_masked, last_blk + 1, functools.partial(body, masked=False), 0)
    o_ref[...] = acc_scr[...]


def _sb_call(q, k_t, v_t, layer, past, tq, tk, new_kv=None):
    b, t, _ = q.shape
    u = (np.arange(tk)[:, None] > np.arange(tk)[None, :]).astype(np.float32)
    kv_spec = pl.BlockSpec((None, None, HW, k_t.shape[3]), lambda bi, i: (layer, bi, 0, 0))
    in_specs = [pl.BlockSpec((None, tq, HW), lambda bi, i: (bi, i, 0)), kv_spec, kv_spec]
    args = [q, k_t, v_t]
    if new_kv is not None:
        assert past % tk == 0 and k_t.shape[3] == past and t <= tk
        in_specs += [pl.BlockSpec((None, HW, new_kv[0].shape[2]), lambda bi, i: (bi, 0, 0))] * 2
        args += list(new_kv)
    return pl.pallas_call(
        functools.partial(_sb_kernel, tq=tq, tk=tk, past=past, new_apart=new_kv is not None),
        grid=(b, t // tq),
        in_specs=in_specs + [pl.BlockSpec((tk, tk), lambda bi, i: (0, 0))],
        out_specs=pl.BlockSpec((None, tq, HW), lambda bi, i: (bi, i, 0)),
        out_shape=jax.ShapeDtypeStruct((b, t, HW), F32),
        scratch_shapes=[pltpu.VMEM((tq, HW), F32), pltpu.VMEM((N_HEADS, tq, 1), F32)],
        compiler_params=_cparams(2),
        name="stick_breaking_attn",
    )(*args, jnp.asarray(u, BF16))


def _gla_tables(c):
    n_lev = int(math.log2(c))
    assert 1 << n_lev == c
    t = np.arange(c)
    rows = []
    for lev in range(1, n_lev + 1):
        blk = c >> (lev - 1)
        ref = (t // blk) * blk + blk // 2 - 1
        lo, hi = np.minimum(t, ref), np.maximum(t, ref)
        rows.append(((t[None, :] > lo[:, None]) & (t[None, :] <= hi[:, None])).astype(np.float32))
    rows.append((t[None, :] <= t[:, None]).astype(np.float32))
    rows.append((t[None, :] > t[:, None]).astype(np.float32))
    table = np.concatenate(rows, axis=0)
    lvl = np.full((c, c), n_lev + 1, np.int32)
    for lev in range(1, n_lev + 1):
        blk = c >> (lev - 1)
        half = blk // 2
        same = (t[:, None] // blk) == (t[None, :] // blk)
        pair = same & ((t[:, None] % blk) >= half) & ((t[None, :] % blk) < half)
        lvl[pair] = lev
    lvl[t, t] = 0
    return np.concatenate([table, table], axis=1), np.tile(lvl.T, (1, N_HEADS)), n_lev


def _gla_kernel(q_seqs, k_seqs, v_seqs, r_seqs, la_seqs, s0_ref, tab_ref, lvl_ref, g_ref,
                o_seqs, s_ref, s_seqs, *, c, n_lev, n_chunks, n_steps):
    j = pl.program_id(1)

    @pl.when(j == 0)
    def _():
        s_seqs[...] = s0_ref[...]

    head = _head_lane_id(HW, HEAD_DIM)
    tab = tab_ref[...]
    lvl = lvl_ref[...]

    def chunk(ci, carry):
        for bi in range(q_seqs.shape[0]):
            seq_chunk(bi, pl.ds(pl.multiple_of(ci * c, c), c))
        return carry

    def seq_chunk(bi, rows):
        q_ref, k_ref, v_ref, r_ref, la_ref, o_ref, s_scr = (
            a.at[bi] for a in (q_seqs, k_seqs, v_seqs, r_seqs, la_seqs, o_seqs, s_seqs))
        q = q_ref[rows, :] * (HEAD_DIM ** -0.5)
        k = k_ref[rows, :]
        decays = jnp.exp(_dot(tab, jnp.concatenate(_split_bf16(la_ref[rows, :]), axis=0)))
        by_head = lambda a: jnp.concatenate([jnp.where(head == h, a, 0.0) for h in range(N_HEADS)],
                                            axis=0).astype(BF16)
        att_t = jnp.zeros((c, N_HEADS * c), F32)
        for lev in range(n_lev + 1):
            if lev == 0:
                ql, kl = q, k
            else:
                e = decays[(lev - 1) * c:lev * c]
                ql, kl = q * e, k * e
            a_t = _dot_nt(kl.astype(BF16), by_head(ql))
            att_t = jnp.where(lvl == lev, a_t, att_t)
        q_decay = decays[n_lev * c:(n_lev + 1) * c]
        k_out = (k * decays[(n_lev + 1) * c:(n_lev + 2) * c]).astype(BF16)
        s = s_scr[...]
        v_all = v_ref[rows, :].astype(BF16)
        intra = _dot_tn(att_t.astype(BF16), v_all)
        inter = _dot(by_head(q * q_decay), s.astype(BF16))
        for h in range(N_HEADS):
            o = intra[h * c:(h + 1) * c, h * DV_B:(h + 1) * DV_B] + inter[h * c:(h + 1) * c]
            y = _rms_rows(o, g_ref[:, h * DV_B:(h + 1) * DV_B])
            o_ref[rows, h * DV_B:(h + 1) * DV_B] = y * _silu(r_ref[rows, h * DV_B:(h + 1) * DV_B])
        kv = _dot_tn(k_out, v_all)
        ds = jnp.concatenate([kv[h * HEAD_DIM:(h + 1) * HEAD_DIM, h * DV_B:(h + 1) * DV_B]
                              for h in range(N_HEADS)], axis=0)
        chunk_decay = jnp.broadcast_to(q_decay[c - 1:c, :], (DV_B, HW)).T
        s_scr[...] = chunk_decay * s + ds

    lax.fori_loop(0, n_chunks, chunk, 0)

    @pl.when(j == n_steps - 1)
    def _():
        s_ref[...] = s_seqs[...]


def _gla_call(q, k, v, r, la, s0, g_gla, c):
    b, t, _ = q.shape
    vw = N_HEADS * DV_B
    tc = _pick_tile(t, 512, c)
    n_steps = t // tc
    nb = max(n for n in (4, 2, 1) if b % n == 0)
    table, lvl, n_lev = _gla_tables(c)
    qk_spec = pl.BlockSpec((nb, tc, HW), lambda bi, j: (bi, j, 0))
    vr_spec = pl.BlockSpec((nb, tc, vw), lambda bi, j: (bi, j, 0))
    st_spec = pl.BlockSpec((nb, HW, DV_B), lambda bi, j: (bi, 0, 0))
    const = lambda shape: pl.BlockSpec(shape, lambda bi, j: (0, 0))
    return pl.pallas_call(
        functools.partial(_gla_kernel, c=c, n_lev=n_lev, n_chunks=tc // c, n_steps=n_steps),
        grid=(b // nb, n_steps),
        in_specs=[qk_spec, qk_spec, vr_spec, vr_spec, qk_spec, st_spec,
                  const(table.shape), const(lvl.shape), const((1, vw))],
        out_specs=[vr_spec, st_spec],
        out_shape=[jax.ShapeDtypeStruct((b, t, vw), F32), jax.ShapeDtypeStruct((b, HW, DV_B), F32)],
        scratch_shapes=[pltpu.VMEM((nb, HW, DV_B), F32)],
        compiler_params=_cparams(2),
        name="gated_linear_attn",
    )(q, k, v, r, la, s0, jnp.asarray(table, BF16), jnp.asarray(lvl), g_gla)


_COUNT_ROWS = 32


def _key_to_f32(key):
    key = jnp.maximum(key, _KEY_NEG_INF)
    bits = jnp.where(key < 0, key ^ 0x7FFFFFFF, key)
    return lax.bitcast_convert_type(bits, F32)


def _dsa_kernel(q_ref, qi_ref, wi_ref, kc_ref, vc_ref, ki_ref, eye_ref, tri_ref, o_ref,
                sc_scr, hi_scr, lo_scr, lg_scr, vct_scr, acc_scr, *, tq, tk, past, l_real, n_sel):
    i = pl.program_id(1)
    scale = HEAD_DIM ** -0.5
    eye = eye_ref[...]
    l_pad = sc_scr.shape[0]

    @pl.when(i == 0)
    def _():
        vct_scr[...] = vc_ref[...].astype(BF16)

    q_t = _dot_nt(eye, (q_ref[...] * scale).astype(BF16)).astype(BF16)
    qi_t = _dot_nt(eye, (qi_ref[...] * scale).astype(BF16)).astype(BF16)
    w_t = wi_ref[...].T * (N_HEADS ** -0.5)
    qpos = past + i * tq + lax.broadcasted_iota(jnp.int32, (1, tq), 1)
    chunk_shift = int(math.log2(CHUNK))
    qchunk = lax.shift_right_logical(qpos, chunk_shift)
    k_end = ((past + (i + 1) * tq - 1) // CHUNK + 1) * CHUNK
    n_blk = jnp.minimum((k_end + tk - 1) // tk, l_pad // tk)

    def rows_of(j):
        return pl.ds(pl.multiple_of(j * tk, tk), tk)

    def admissible(j):
        kpos = j * tk + lax.broadcasted_iota(jnp.int32, (tk, 1), 0)
        return (lax.shift_right_logical(kpos, chunk_shift) <= qchunk) & (kpos < l_real)

    def score_blk(j, carry):
        kib = ki_ref[rows_of(j), :].astype(BF16)
        s = jnp.zeros((tk, tq), F32)
        for h in range(N_HEADS):
            sh = _dot(kib, qi_t[h * HEAD_DIM:(h + 1) * HEAD_DIM])
            s = s + w_t[_WI_LANE + h:_WI_LANE + h + 1] * jnp.maximum(sh, 0.0)
        s = jnp.where(admissible(j), jnp.where(s == 0.0, 0.0, s), _NEG_INF)
        sc_scr[rows_of(j), :] = s
        bits = lax.bitcast_convert_type(s, jnp.int32)
        key = jnp.where(bits < 0, bits ^ 0x7FFFFFFF, bits)
        hi_scr[rows_of(j), :] = lax.shift_right_arithmetic(key, 16).astype(jnp.int16)
        lo_scr[rows_of(j), :] = ((key & 0xFFFF) - 32768).astype(jnp.int16)
        return carry

    _fori_by_two(0, n_blk, score_blk, 0)

    def counts(ref, preds, one, zero):
        def blk(j, accs):
            a = ref[rows_of(j), :]
            out = []
            for pred, acc in zip(preds, accs):
                m = jnp.where(pred(a), one, zero).reshape(tk // _COUNT_ROWS, _COUNT_ROWS, tq)
                for r in range(tk // _COUNT_ROWS):
                    acc = acc + m[r]
                out.append(acc)
            return tuple(out)
        accs = _fori_by_two(0, n_blk, blk, tuple(jnp.zeros((_COUNT_ROWS, tq), one.dtype) for _ in preds))
        return [jnp.sum(acc.astype(jnp.int32), axis=0, keepdims=True) for acc in accs]

    def count(ref, pred, one, zero):
        return counts(ref, [pred], one, zero)[0]

    one16, zero16 = jnp.int16(1), jnp.int16(0)

    def bisect16(ref, want):
        def bit(it, tau):
            cand = tau + lax.shift_left(jnp.int32(1), 15 - it)
            c16 = cand.astype(jnp.int16)
            return jnp.where(count(ref, lambda a: a >= c16, one16, zero16) >= want, cand, tau)
        return lax.fori_loop(0, 16, bit, jnp.full((1, tq), -32768, jnp.int32))

    hi_star = bisect16(hi_scr, n_sel)
    h16 = hi_star.astype(jnp.int16)
    want_lo = n_sel - count(hi_scr, lambda a: a > h16, one16, zero16)

    def mask_lo(j, carry):
        lo_scr[rows_of(j), :] = jnp.where(hi_scr[rows_of(j), :] == h16, lo_scr[rows_of(j), :],
                                          jnp.int16(-32768))
        return carry

    lax.fori_loop(0, n_blk, mask_lo, 0)
    lo_star = bisect16(lo_scr, want_lo)
    thr_fast = _key_to_f32(lax.shift_left(hi_star, 16) + (lo_star + 32768))

    f_one, f_zero = jnp.float32(1.0), jnp.float32(0.0)
    count_ge = lambda t: count(sc_scr, lambda a: a >= t, f_one, f_zero)
    count_gt = lambda t: count(sc_scr, lambda a: a > t, f_one, f_zero)
    n_gt, n_ge = counts(sc_scr, [lambda a: a > thr_fast, lambda a: a >= thr_fast], f_one, f_zero)
    settled = jnp.min(jnp.where((n_gt < n_sel) & (n_ge >= n_sel), 1, 0)) == 1

    def float_bisection():
        def value_bit(it, tau):
            cand = tau + lax.shift_left(jnp.int32(1), 31 - it)
            return jnp.where(count_ge(_key_to_f32(cand)) >= n_sel, cand, tau)
        t = _key_to_f32(lax.fori_loop(0, 32, value_bit, jnp.full((1, tq), _INT_MIN, jnp.int32)))
        return t, count_gt(t)

    thr, n_gt = lax.cond(settled, lambda: (thr_fast, n_gt), float_bisection)
    need = (n_sel - n_gt).astype(F32)

    tri = tri_ref[...]
    fold = lambda a, op: op(a.reshape(tk // SUBLANE, SUBLANE, tq), axis=0)
    parts = lambda v: tuple(jnp.full((SUBLANE, tq), v, F32) for _ in range(N_HEADS))

    def logit_blk(j, carry):
        m_parts, ties_before = carry
        s = sc_scr[rows_of(j), :]
        tie = s == thr
        rank = _dot(tri, jnp.where(tie, 1.0, 0.0).astype(BF16)) + ties_before
        sel = admissible(j) & ((s > thr) | (tie & (rank <= need)))
        bias = jnp.where(sel, 0.0, _NEG_INF)
        kcb = kc_ref[rows_of(j), :].astype(BF16)
        new_parts = []
        for h in range(N_HEADS):
            lg = _dot(kcb, q_t[h * HEAD_DIM:(h + 1) * HEAD_DIM]) + bias
            lg_scr[h, rows_of(j), :] = lg
            new_parts.append(jnp.maximum(m_parts[h], fold(lg, jnp.max)))
        return tuple(new_parts), rank[tk - 1:tk, :]

    m_parts, _ = _fori_by_two(0, n_blk, logit_blk, (parts(_NEG_INF), jnp.zeros((1, tq), F32)))
    m_use = []
    for h in range(N_HEADS):
        m = jnp.max(m_parts[h], axis=0, keepdims=True)
        m_use.append(jnp.where(m == _NEG_INF, 0.0, m))
    acc_scr[...] = jnp.zeros_like(acc_scr)

    def pv_blk(j, l_parts):
        vct = vct_scr[:, rows_of(j)]
        new_parts = []
        for h in range(N_HEADS):
            hd = slice(h * HEAD_DIM, (h + 1) * HEAD_DIM)
            p = jnp.exp(lg_scr[h, rows_of(j), :] - m_use[h])
            new_parts.append(l_parts[h] + fold(p, jnp.sum))
            acc_scr[hd, :] += _dot(vct, p.astype(BF16))
        return tuple(new_parts)

    l_parts = _fori_by_two(0, n_blk, pv_blk, parts(0.0))
    out_t = jnp.concatenate(
        [acc_scr[h * HEAD_DIM:(h + 1) * HEAD_DIM, :] / jnp.sum(l_parts[h], axis=0, keepdims=True)
         for h in range(N_HEADS)], axis=0)
    o_ref[...] = out_t.T


def _dsa_call(q, qi, wi, kc, vc_t, ki, layer, past, l_real, tq, tk):
    b, t, _ = q.shape
    l_pad = kc.shape[1]
    n_sel = min(TOPK_MAX, l_real // 4)
    assert tk >= n_sel and l_pad % tk == 0 and tk % _COUNT_ROWS == 0
    row = lambda w: pl.BlockSpec((None, tq, w), lambda bi, i: (bi, i, 0))
    full = pl.BlockSpec((None, l_pad, HEAD_DIM), lambda bi, i: (bi, 0, 0))
    full_t = pl.BlockSpec((None, None, HEAD_DIM, l_pad), lambda bi, i: (layer, bi, 0, 0))
    const = lambda n: pl.BlockSpec((n, n), lambda bi, i: (0, 0))
    eye = np.eye(HW, dtype=np.float32)
    tri = (np.arange(tk)[:, None] >= np.arange(tk)[None, :]).astype(np.float32)
    return pl.pallas_call(
        functools.partial(_dsa_kernel, tq=tq, tk=tk, past=past, l_real=l_real, n_sel=n_sel),
        grid=(b, t // tq),
        in_specs=[row(HW), row(HW), row(LANE), full, full_t, full, const(HW), const(tk)],
        out_specs=row(HW),
        out_shape=jax.ShapeDtypeStruct((b, t, HW), F32),
        scratch_shapes=[pltpu.VMEM((l_pad, tq), F32),
                        pltpu.VMEM((l_pad, tq), jnp.int16), pltpu.VMEM((l_pad, tq), jnp.int16),
                        pltpu.VMEM((N_HEADS, l_pad, tq), F32),
                        pltpu.VMEM((HEAD_DIM, l_pad), BF16), pltpu.VMEM((HW, tq), F32)],
        compiler_params=_cparams(2),
        name="indexer_sparse_attn",
    )(q, qi, wi, kc, vc_t, ki, jnp.asarray(eye, BF16), jnp.asarray(tri, BF16))


def _pad_rows(a, n):
    return a if n == a.shape[1] else jnp.pad(a, ((0, 0), (0, n - a.shape[1]), (0, 0)))


def _pad_last(a, n):
    return a if n == a.shape[-1] else jnp.pad(a, [(0, 0)] * (a.ndim - 1) + [(0, n - a.shape[-1])])


def _encoder_layer(x, n_seq, t, mod, past, w, stacked=None):
    rows, d = x.shape
    tm = _pick_tile(t, 512, SUBLANE) if not mod.per_token else rows
    tm_ffn = _pick_tile(t, 1024, SUBLANE) if not mod.per_token else rows
    x = _ffn_call(x, mod, 0, w["gains"], 0, w["w_f1_in"], w["w_f1_out"], w["layer"], tm_ffn)
    seq = lambda a: a.reshape(n_seq, t, a.shape[-1])
    p_len = 0 if past is None else past["k_sb_t"].shape[3]
    l_real = p_len + t
    tk = 256 if l_real >= 256 else LANE
    l_pad = -(-l_real // tk) * tk
    tq = _pick_tile(t, 256, SUBLANE)
    proj = functools.partial(_proj_call, x, mod, w["gains"], w["w_in"], w["w_gate"], w["b_gate"], tm)
    if stacked is None:
        (sbq, sbk, sbv, gq, gk, gv, gr, la, dq, dqi, dkc, dvc, dki, dwi), _ = proj()
        lyr = past["layer"]
        keys_last = lambda new: jnp.swapaxes(seq(new), 1, 2)
        join_t = lambda old_t, new: _pad_last(jnp.concatenate([old_t, keys_last(new)], axis=2), l_pad)[None]
        join_r = lambda old, new: _pad_rows(jnp.concatenate([old, seq(new)], axis=1), l_pad)
        if p_len % tk == 0 and t <= tk:
            new_kv = tuple(_pad_last(keys_last(a), tk) for a in (sbk, sbv))
            o_a = _sb_call(seq(sbq), past["k_sb_t"], past["v_sb_t"], lyr, p_len, tq, tk, new_kv=new_kv)
        else:
            o_a = _sb_call(seq(sbq), join_t(past["k_sb_t"][lyr], sbk), join_t(past["v_sb_t"][lyr], sbv),
                           0, p_len, tq, tk)
        vc_t = join_t(past["v_dsa_t"], dvc)
        kc_rows, ki_rows = join_r(past["k_dsa"], dkc), join_r(past["k_idx"], dki)
        layer = 0
        new_rows = (seq(sbk).reshape(n_seq, t, N_HEADS, HEAD_DIM), seq(sbv).reshape(n_seq, t, N_HEADS, HEAD_DIM),
                    seq(dkc), seq(dvc), seq(dki))
    else:
        assert past is None and l_pad == t and t % tm == 0 and tm % LANE == 0
        layer, prev = stacked
        (sbq, gq, gk, gv, gr, la, dq, dqi, dkc, dki, dwi), new_rows = proj((layer, n_seq, prev))
        k_t, v_t, _, vc_t, _ = new_rows
        kc_rows, ki_rows = seq(dkc), seq(dki)
        o_a = _sb_call(seq(sbq), k_t, v_t, layer, p_len, tq, tk)

    c = CHUNK if t % CHUNK == 0 else t
    s0 = jnp.zeros((n_seq, HW, DV_B), F32) if past is None else past["gla"]
    o_b, s_new = _gla_call(seq(gq), seq(gk), seq(gv), seq(gr), seq(la), s0, w["g_gla"], c)

    t_c = t if t % LANE == 0 else -(-t // LANE) * LANE
    tq_c = _pick_tile(t_c, 256, LANE)
    qpad = lambda a: _pad_rows(seq(a), t_c)
    o_c = _dsa_call(qpad(dq), qpad(dqi), qpad(dwi), kc_rows, vc_t, ki_rows, layer, p_len, l_real, tq_c, tk)
    o_c = o_c if t_c == t else o_c[:, :t]

    flat = lambda a: a.reshape(rows, a.shape[-1])
    x = _ffn_call(x, mod, 6, w["gains"], 4, w["w_f2_in"], w["w_f2_out"], w["layer"], tm_ffn,
                  mix=(flat(o_a), flat(o_b), flat(o_c), w["w_out"]))
    return x, new_rows, s_new.reshape(n_seq, N_HEADS, HEAD_DIM, DV_B)


def kernel(x_prompt, x_sample, cache_k_sb, cache_v_sb, cache_k_dsa, cache_v_dsa, cache_k_idx, state_gla,
           c_prompt, c_sample, w_ada, b_ada, norm_gains, w_ffn1_in, w_ffn1_out, w_ffn2_in, w_ffn2_out,
           w_in, w_gla_gate, b_gla_gate, gla_norm, w_out):
    bp, tp, d = x_prompt.shape
    bs, ts, _ = x_sample.shape
    depth = w_ada.shape[0]
    p_len = cache_k_sb.shape[2]

    mods = _mod_call(jnp.concatenate([c_prompt, c_sample], axis=0), w_ada, b_ada)
    xp = x_prompt.reshape(bp * tp, d)
    xs = x_sample.reshape(bs * ts, d)
    ffn_w = [a.astype(BF16) for a in (w_ffn1_in, w_ffn1_out, w_ffn2_in, w_ffn2_out)]
    sb_cache_t = [jnp.transpose(c, (0, 1, 3, 4, 2)).reshape(depth, bs, HW, p_len) for c in (cache_k_sb, cache_v_sb)]
    shared_p, gla_p, acc_s = None, [], []
    for l in range(depth):
        w_gate = jnp.pad(w_gla_gate[l], ((_GB_LANE, LANE - GATE_RANK - _GB_LANE), (0, 0))).astype(BF16)
        w = dict(gains=norm_gains[l].reshape(-1, 1, d),
                 layer=l, w_f1_in=ffn_w[0], w_f1_out=ffn_w[1], w_f2_in=ffn_w[2], w_f2_out=ffn_w[3],
                 w_in=_relayout_w_in(w_in[l]).astype(BF16), w_gate=w_gate,
                 b_gate=b_gla_gate[l].reshape(1, HW), g_gla=gla_norm[l].reshape(1, N_HEADS * DV_B),
                 w_out=w_out[l].astype(BF16))
        mod_p = _Mod(mods[l, :bp].reshape(bp * N_MOD, 1, d), False, tp)
        mod_s_rows = jnp.repeat(mods[l, bp:].reshape(bs, N_MOD, d), ts, axis=0)
        mod_s = _Mod(jnp.transpose(mod_s_rows, (1, 0, 2)), True)
        past = dict(layer=l, k_sb_t=sb_cache_t[0], v_sb_t=sb_cache_t[1],
                    k_dsa=cache_k_dsa[l], v_dsa_t=jnp.swapaxes(cache_v_dsa[l], 1, 2), k_idx=cache_k_idx[l],
                    gla=state_gla[l].reshape(bs, HW, DV_B))
        xp, shared_p, s_p = _encoder_layer(xp, bp, tp, mod_p, None, w, stacked=(l, shared_p))
        xs, rows_s, s_s = _encoder_layer(xs, bs, ts, mod_s, past, w)
        gla_p.append(s_p)
        acc_s.append(rows_s + (s_s,))
    k_t, v_t, kc_t, vc_t, ki_t = shared_p
    heads_out = lambda a: jnp.transpose(a.reshape(depth, bp, N_HEADS, HEAD_DIM, tp), (0, 1, 4, 2, 3))
    tokens_out = lambda a: jnp.swapaxes(a, 2, 3)
    field = lambda acc, i: jnp.stack([r[i] for r in acc], axis=0)
    return (xp.reshape(bp, tp, d), xs.reshape(bs, ts, d),
            heads_out(k_t), heads_out(v_t), tokens_out(kc_t), tokens_out(vc_t), tokens_out(ki_t),
            jnp.stack(gla_p, axis=0), *(field(acc_s, i) for i in range(6)))
```

```python
import functools
import math

import numpy as np
import jax
import jax.numpy as jnp
from jax import lax
from jax.experimental import pallas as pl
from jax.experimental.pallas import tpu as pltpu

F32 = jnp.float32
BF16 = jnp.bfloat16

CHUNK = 64
N_HEADS = 4
HEAD_DIM = 64
HW = N_HEADS * HEAD_DIM
DV_B = 128
GATE_RANK = 16
GATE_TAU = 16.0
TOPK_MAX = 256
EPS = 1e-6
MACARON_W = 0.5
N_MOD = 9

LANE = 128
SUBLANE = 8
VMEM_BYTES = 64 * 1024 * 1024
VMEM_LIMIT_BYTES = VMEM_BYTES * 7 // 8

_NEG_INF = float("-inf")
_KEY_NEG_INF = -2**31 + 0x7FFFFF
_INT_MIN = -2**31


def _pick_tile(n, target, mult):
    if n <= target:
        return n
    t = (target // mult) * mult
    while t >= mult:
        if n % t == 0:
            return t
        t -= mult
    return n


def _cparams(n_axes):
    return pltpu.CompilerParams(dimension_semantics=("arbitrary",) * n_axes,
                                vmem_limit_bytes=VMEM_LIMIT_BYTES)


def _rms_rows(x, g):
    ms = jnp.mean(x * x, axis=-1, keepdims=True)
    return x * lax.rsqrt(ms + EPS) * g


def _silu(x):
    return x * (1.0 / (1.0 + jnp.exp(-x)))


def _log_sigmoid(x):
    return jnp.minimum(x, 0.0) - jnp.log(1.0 + jnp.exp(-jnp.abs(x)))


_LOG2_E = 1.4426950408889634


def _neg_abs(x):
    bits = lax.bitcast_convert_type(x, jnp.int32) | _INT_MIN
    return lax.bitcast_convert_type(bits, F32)


def _dot(a, b):
    return jnp.dot(a, b, preferred_element_type=F32)


def _dot_nt(a, b):
    return lax.dot_general(a, b, (((1,), (1,)), ((), ())), preferred_element_type=F32)


def _dot_tn(a, b):
    return lax.dot_general(a, b, (((0,), (0,)), ((), ())), preferred_element_type=F32)


def _fori_by_two(lo, hi, body, init):
    n = hi - lo

    def two(p, carry):
        j = lo + 2 * p
        return body(j + 1, body(j, carry))

    carry = lax.fori_loop(0, lax.shift_right_logical(n, 1), two, init)
    return lax.cond((n & 1) == 1, lambda c: body(hi - 1, c), lambda c: c, carry)


def _split_bf16(x):
    hi = x.astype(BF16)
    lo = (x - hi.astype(F32)).astype(BF16)
    return hi, lo


def _head_lane_id(width, per_head):
    return lax.broadcasted_iota(jnp.int32, (1, width), 1) // per_head


def _mod_kernel(c_ref, w_ref, b_ref, o_ref):
    a = _silu(c_ref[...]).astype(BF16)
    o_ref[...] = _dot(a, w_ref[...].astype(BF16)) + b_ref[...]


def _mod_call(c, w_ada, b_ada):
    depth, d, nd = w_ada.shape
    n = c.shape[0]
    tn = _pick_tile(nd, 1536, LANE)
    return pl.pallas_call(
        _mod_kernel,
        grid=(depth, nd // tn),
        in_specs=[pl.BlockSpec((n, d), lambda l, j: (0, 0)),
                  pl.BlockSpec((None, d, tn), lambda l, j: (l, 0, j)),
                  pl.BlockSpec((None, 1, tn), lambda l, j: (l, 0, j))],
        out_specs=pl.BlockSpec((None, n, tn), lambda l, j: (l, 0, j)),
        out_shape=jax.ShapeDtypeStruct((depth, n, nd), F32),
        compiler_params=_cparams(2),
        name="adaln_mod",
    )(c, w_ada, b_ada.reshape(depth, 1, nd))


class _Mod:
    def __init__(self, arr, per_token, seq_len=None):
        self.arr, self.per_token, self.seq_len = arr, per_token, seq_len

    def spec(self, k, tm, d):
        if self.per_token:
            return pl.BlockSpec((None, tm, d), lambda i, *_: (k, i, 0))
        assert self.seq_len % tm == 0
        tps = self.seq_len // tm
        return pl.BlockSpec((None, 1, d), lambda i, *_: ((i // tps) * N_MOD + k, 0, 0))


def _gain_spec(k, d):
    return pl.BlockSpec((None, 1, d), lambda i, *_: (k, 0, 0))


def _mix_residual(x, oa_ref, ob_ref, oc_ref, ga_ref, g_ref, w_ref):
    wa = HW
    wb = wa + N_HEADS * DV_B
    y = (_dot(oa_ref[...].astype(BF16), w_ref[0:wa, :])
         + _dot(ob_ref[...].astype(BF16), w_ref[wa:wb, :])
         + _dot(oc_ref[...].astype(BF16), w_ref[wb:wb + HW, :]))
    return x + ga_ref[...] * _rms_rows(y, g_ref[...])


class _Rows:
    def __init__(self, ref, r0, n, full):
        self.ref, self.rows = ref, (slice(r0, r0 + n) if ref.shape[0] == full else slice(None))

    def __getitem__(self, _):
        return self.ref[self.rows, :]


def _ffn_kernel(x_ref, *refs, f, tf, with_mix, n_split):
    tm = x_ref.shape[0]
    o_ref = refs[-1]
    for r0 in range(0, tm, tm // n_split):
        view = lambda ref: _Rows(ref, r0, tm // n_split, tm)
        x = x_ref[r0:r0 + tm // n_split, :]
        rest = refs[:-1]
        if with_mix:
            oa, ob, oc, ga2, g3, w_mix = rest[:6]
            x = _mix_residual(x, view(oa), view(ob), view(oc), view(ga2), g3, w_mix)
            rest = rest[6:]
        sh_ref, sc_ref, ga_ref, gin_ref, gout_ref, wi_ref, wo_ref = rest
        h = (_rms_rows(x, gin_ref[...]) * (1.0 + view(sc_ref)[...]) + view(sh_ref)[...]).astype(BF16)
        y = None
        for c0 in range(0, f, tf):
            gate = _dot(h, wi_ref[:, c0:c0 + tf])
            up = _dot(h, wi_ref[:, f + c0:f + c0 + tf])
            part = _dot((_silu(gate) * up).astype(BF16), wo_ref[c0:c0 + tf, :])
            y = part if y is None else y + part
        o_ref[r0:r0 + tm // n_split, :] = x + MACARON_W * view(ga_ref)[...] * _rms_rows(y, gout_ref[...])


def _ffn_call(x, mod, mod_k, gains, gain_k, w_in, w_out, layer, tm, mix=None):
    rows, d = x.shape
    f = w_out.shape[1]
    tf = _pick_tile(f, 1408, LANE)
    row = lambda w: pl.BlockSpec((tm, w), lambda i: (i, 0))
    resident = lambda shape: pl.BlockSpec((None,) + shape, lambda i: (layer, 0, 0),
                                          pipeline_mode=pl.Buffered(1))
    in_specs, args = [row(d)], [x]
    if mix is not None:
        oa, ob, oc, w_mix = mix
        in_specs += [row(oa.shape[1]), row(ob.shape[1]), row(oc.shape[1]), mod.spec(5, tm, d), _gain_spec(3, d),
                     pl.BlockSpec(w_mix.shape, lambda i: (0, 0), pipeline_mode=pl.Buffered(1))]
        args += [oa, ob, oc, mod.arr, gains, w_mix]
    in_specs += [mod.spec(mod_k, tm, d), mod.spec(mod_k + 1, tm, d), mod.spec(mod_k + 2, tm, d),
                 _gain_spec(gain_k, d), _gain_spec(gain_k + 1, d),
                 resident((d, 2 * f)), resident((f, d))]
    args += [mod.arr, mod.arr, mod.arr, gains, gains, w_in, w_out]
    return pl.pallas_call(
        functools.partial(_ffn_kernel, f=f, tf=tf, with_mix=mix is not None,
                          n_split=2 if tm % (2 * SUBLANE) == 0 else 1),
        grid=(rows // tm,),
        in_specs=in_specs,
        out_specs=row(d),
        out_shape=jax.ShapeDtypeStruct((rows, d), F32),
        compiler_params=_cparams(1),
        name="macaron_ffn",
    )(*args)


_PROJ_GROUPS = (("sbq", HW), ("sbk", HW), ("sbv", HW),
                ("gq", HW), ("gk", HW), ("gv", N_HEADS * DV_B), ("gr", N_HEADS * DV_B),
                ("dq", HW), ("dqi", HW), ("dkv", LANE), ("dkiwg", LANE))
_PROJ_OFFSETS = tuple(int(v) for v in np.cumsum([0] + [w for _, w in _PROJ_GROUPS]))
_PROJ_WIDTH = _PROJ_OFFSETS[-1]
_WI_LANE = HEAD_DIM
_GB_LANE = _WI_LANE + N_HEADS
_GATE = "gate"
_PROJ_OUTPUTS = (("sbq", 0, HW), ("sbk", 0, HW), ("sbv", 0, HW), ("gq", 0, HW), ("gk", 0, HW),
                 ("gv", 0, N_HEADS * DV_B), ("gr", 0, N_HEADS * DV_B), (_GATE, 0, HW),
                 ("dq", 0, HW), ("dqi", 0, HW), ("dkv", 0, HEAD_DIM), ("dkv", HEAD_DIM, HEAD_DIM),
                 ("dkiwg", 0, HEAD_DIM), ("dkiwg", 0, LANE))
_PROJ_ROW_OUTPUTS = (("sbq", 0, HW), ("gq", 0, HW), ("gk", 0, HW), ("gv", 0, N_HEADS * DV_B),
                     ("gr", 0, N_HEADS * DV_B), (_GATE, 0, HW), ("dq", 0, HW), ("dqi", 0, HW),
                     ("dkv", 0, HEAD_DIM), ("dkiwg", 0, HEAD_DIM), ("dkiwg", 0, LANE))
_PROJ_T_OUTPUTS = (("sbk", 0, HW), ("sbv", 0, HW), ("dkv", 0, HEAD_DIM), ("dkv", HEAD_DIM, HEAD_DIM),
                   ("dkiwg", 0, HEAD_DIM))


def _relayout_w_in(w_in):
    d = w_in.shape[0]
    widths = (HW, HW, HW, HW, HW, N_HEADS * DV_B, N_HEADS * DV_B, GATE_RANK,
              HW, HEAD_DIM, HEAD_DIM, HW, HEAD_DIM, N_HEADS)
    offs = np.cumsum((0,) + widths)
    col = lambda i: w_in[:, offs[i]:offs[i + 1]]
    pad = lambda a, w: jnp.pad(a, ((0, 0), (0, w - a.shape[1])))
    parts = [col(0), col(1), col(2), col(3), col(4), col(5), col(6),
             col(8), col(11), jnp.concatenate([col(9), col(10)], axis=1),
             pad(jnp.concatenate([col(12), col(13), col(7)], axis=1), LANE)]
    out = jnp.concatenate(parts, axis=1)
    assert out.shape == (d, _PROJ_WIDTH)
    return out


def _proj_kernel(x_ref, sh_ref, sc_ref, g_ref, w_ref, wgate_ref, bgate_ref, *refs, row_outs, t_outs, n_prev):
    prev_refs = refs[:n_prev]
    row_refs = refs[n_prev:n_prev + len(row_outs)]
    t_refs = refs[n_prev + len(row_outs):]
    for o_ref, prev_ref in zip(t_refs, prev_refs):
        o_ref[:prev_ref.shape[0]] = prev_ref[...]
    h = (_rms_rows(x_ref[...], g_ref[...]) * (1.0 + sc_ref[...]) + sh_ref[...]).astype(BF16)
    for gi, (name, _) in enumerate(_PROJ_GROUPS):
        p = _dot(h, w_ref[:, _PROJ_OFFSETS[gi]:_PROJ_OFFSETS[gi + 1]])
        groups = {name: p}
        if name == "dkiwg":
            pre = _dot(p.astype(BF16), wgate_ref[...]) + bgate_ref[...]
            groups[_GATE] = _log_sigmoid(pre) * (1.0 / GATE_TAU)
        for o_ref, (out_group, start, width) in zip(row_refs, row_outs):
            if out_group in groups:
                o_ref[...] = groups[out_group][:, start:start + width]
        for o_ref, (out_group, start, width) in zip(t_refs, t_outs):
            if out_group in groups:
                o_ref[o_ref.shape[0] - 1] = groups[out_group].T[start:start + width, :]


def _proj_call(x, mod, gains, w_in, w_gate, b_gate, tm, stacked=None):
    rows, d = x.shape
    in_specs = [pl.BlockSpec((tm, d), lambda i: (i, 0)),
                mod.spec(3, tm, d), mod.spec(4, tm, d), _gain_spec(2, d),
                pl.BlockSpec((d, _PROJ_WIDTH), lambda i: (0, 0)),
                pl.BlockSpec((LANE, HW), lambda i: (0, 0)),
                pl.BlockSpec((1, HW), lambda i: (0, 0))]
    args = [x, mod.arr, mod.arr, gains, w_in, w_gate, b_gate]
    if stacked is None:
        row_outs, t_outs, prev = _PROJ_OUTPUTS, (), ()
        t_specs, t_shapes = [], []
    else:
        layer, n_seq, prev = stacked
        row_outs, t_outs = _PROJ_ROW_OUTPUTS, _PROJ_T_OUTPUTS
        t = rows // n_seq
        tps = t // tm
        tile = lambda n, w: pl.BlockSpec((n, None, w, tm), lambda i: (0, i // tps, 0, i % tps))
        t_specs = [tile(layer + 1, w) for _, _, w in t_outs]
        t_shapes = [jax.ShapeDtypeStruct((layer + 1, n_seq, w, t), F32) for _, _, w in t_outs]
        prev = () if prev is None else tuple(prev)
        in_specs += [tile(layer, w) for _, _, w in t_outs[:len(prev)]]
    outs = pl.pallas_call(
        functools.partial(_proj_kernel, row_outs=row_outs, t_outs=t_outs, n_prev=len(prev)),
        grid=(rows // tm,),
        in_specs=in_specs,
        out_specs=[pl.BlockSpec((tm, w), lambda i: (i, 0)) for _, _, w in row_outs] + t_specs,
        out_shape=[jax.ShapeDtypeStruct((rows, w), F32) for _, _, w in row_outs] + t_shapes,
        compiler_params=_cparams(1),
        name="mix_in_proj",
    )(*args, *prev)
    return outs[:len(row_outs)], outs[len(row_outs):]


def _sb_kernel(q_ref, k_ref, v_ref, *rest, tq, tk, past, new_apart):
    if new_apart:
        kn_ref, vn_ref, u_ref, o_ref, acc_scr, c_scr = rest
    else:
        u_ref, o_ref, acc_scr, c_scr = rest
    i = pl.program_id(1)
    head = _head_lane_id(HW, HEAD_DIM)
    q = q_ref[...] * (HEAD_DIM ** -0.5 * _LOG2_E)
    q_st = jnp.concatenate([jnp.where(head == h, q, 0.0) for h in range(N_HEADS)], axis=0).astype(BF16)
    qpos = past + i * tq + lax.broadcasted_iota(jnp.int32, (tq, 1), 0)
    last_blk = (past + (i + 1) * tq - 2) // tk
    acc_scr[...] = jnp.zeros_like(acc_scr)
    c_scr[...] = jnp.zeros_like(c_scr)
    u = u_ref[...]
    rows = N_HEADS * tq

    heads_per_tile = LANE // HEAD_DIM
    low_half = [(lax.broadcasted_iota(jnp.int32, (1, LANE), 1) // HEAD_DIM) == r
                for r in range(heads_per_tile)]

    def body(n, carry, *, masked):
        j = last_blk - n
        k0 = pl.multiple_of(j * tk, tk)
        if new_apart and masked:
            cols = pl.ds(pl.multiple_of(k0 - past, tk), tk)
            kb, vb = kn_ref[:, cols].astype(BF16), vn_ref[:, cols].astype(BF16)
        else:
            kb = k_ref[:, pl.ds(k0, tk)].astype(BF16)
            vb = v_ref[:, pl.ds(k0, tk)].astype(BF16)
        z = _dot(q_st, kb).reshape(N_HEADS, tq, tk)
        soft = jnp.log2(1.0 + jnp.exp2(-jnp.abs(z)))
        log_beta = jnp.minimum(z, 0.0) - soft
        log_keep = log_beta - z
        if masked:
            kpos = k0 + lax.broadcasted_iota(jnp.int32, (1, tk), 1)
            valid = (kpos < qpos)[None]
            log_keep = jnp.where(valid, log_keep, 0.0)
        newer = _dot(log_keep.astype(BF16).reshape(rows, tk), u).reshape(N_HEADS, tq, tk)
        c = c_scr[...]
        w = jnp.exp2(log_beta + newer + c)
        if masked:
            w = jnp.where(valid, w, 0.0)
        pv = _dot_nt(w.astype(BF16).reshape(rows, tk), vb).reshape(N_HEADS, tq, HW)
        tiles = []
        for t0 in range(HW // LANE):
            sl = slice(t0 * LANE, (t0 + 1) * LANE)
            col = pv[t0 * heads_per_tile][:, sl]
            for r in range(1, heads_per_tile):
                col = jnp.where(low_half[r], pv[t0 * heads_per_tile + r][:, sl], col)
            tiles.append(col)
        acc_scr[...] += jnp.concatenate(tiles, axis=1)
        c_scr[...] = c + newer[:, :, 0:1] + log_keep[:, :, 0:1]
        return carry

    n_masked = last_blk + 1 - jnp.minimum((past + i * tq) // tk, last_blk + 1)
    lax.fori_loop(0, n_masked, functools.partial(body, masked=True), 0)
    _fori_by_two(n_masked, last_blk + 1, functools.partial(body, masked=False), 0)
    o_ref[...] = acc_scr[...]


def _sb_call(q, k_t, v_t, layer, past, tq, tk, new_kv=None):
    b, t, _ = q.shape
    u = (np.arange(tk)[:, None] > np.arange(tk)[None, :]).astype(np.float32)
    kv_spec = pl.BlockSpec((None, None, HW, k_t.shape[3]), lambda bi, i: (layer, bi, 0, 0))
    in_specs = [pl.BlockSpec((None, tq, HW), lambda bi, i: (bi, i, 0)), kv_spec, kv_spec]
    args = [q, k_t, v_t]
    if new_kv is not None:
        assert past % tk == 0 and k_t.shape[3] == past and t <= tk
        in_specs += [pl.BlockSpec((None, HW, new_kv[0].shape[2]), lambda bi, i: (bi, 0, 0))] * 2
        args += list(new_kv)
    return pl.pallas_call(
        functools.partial(_sb_kernel, tq=tq, tk=tk, past=past, new_apart=new_kv is not None),
        grid=(b, t // tq),
        in_specs=in_specs + [pl.BlockSpec((tk, tk), lambda bi, i: (0, 0))],
        out_specs=pl.BlockSpec((None, tq, HW), lambda bi, i: (bi, i, 0)),
        out_shape=jax.ShapeDtypeStruct((b, t, HW), F32),
        scratch_shapes=[pltpu.VMEM((tq, HW), F32), pltpu.VMEM((N_HEADS, tq, 1), F32)],
        compiler_params=_cparams(2),
        name="stick_breaking_attn",
    )(*args, jnp.asarray(u, BF16))


def _gla_tables(c):
    n_lev = int(math.log2(c))
    assert 1 << n_lev == c
    t = np.arange(c)
    rows = []
    for lev in range(1, n_lev + 1):
        blk = c >> (lev - 1)
        ref = (t // blk) * blk + blk // 2 - 1
        lo, hi = np.minimum(t, ref), np.maximum(t, ref)
        rows.append(((t[None, :] > lo[:, None]) & (t[None, :] <= hi[:, None])).astype(np.float32))
    rows.append((t[None, :] <= t[:, None]).astype(np.float32))
    rows.append((t[None, :] > t[:, None]).astype(np.float32))
    table = np.concatenate(rows, axis=0)
    lvl = np.full((c, c), n_lev + 1, np.int32)
    for lev in range(1, n_lev + 1):
        blk = c >> (lev - 1)
        half = blk // 2
        same = (t[:, None] // blk) == (t[None, :] // blk)
        pair = same & ((t[:, None] % blk) >= half) & ((t[None, :] % blk) < half)
        lvl[pair] = lev
    lvl[t, t] = 0
    return np.concatenate([table, table], axis=1), np.tile(lvl.T, (1, N_HEADS)), n_lev


def _gla_kernel(q_seqs, k_seqs, v_seqs, r_seqs, la_seqs, s0_ref, tab_ref, lvl_ref, g_ref,
                o_seqs, s_ref, s_seqs, *, c, n_lev, n_chunks, n_steps):
    j = pl.program_id(1)

    @pl.when(j == 0)
    def _():
        s_seqs[...] = s0_ref[...]

    head = _head_lane_id(HW, HEAD_DIM)
    tab = tab_ref[...]
    lvl = lvl_ref[...]

    def chunk(ci, carry):
        for bi in range(q_seqs.shape[0]):
            seq_chunk(bi, pl.ds(pl.multiple_of(ci * c, c), c))
        return carry

    def seq_chunk(bi, rows):
        q_ref, k_ref, v_ref, r_ref, la_ref, o_ref, s_scr = (
            a.at[bi] for a in (q_seqs, k_seqs, v_seqs, r_seqs, la_seqs, o_seqs, s_seqs))
        q = q_ref[rows, :] * (HEAD_DIM ** -0.5)
        k = k_ref[rows, :]
        decays = jnp.exp(_dot(tab, jnp.concatenate(_split_bf16(la_ref[rows, :]), axis=0)))
        by_head = lambda a: jnp.concatenate([jnp.where(head == h, a, 0.0) for h in range(N_HEADS)],
                                            axis=0).astype(BF16)
        att_t = jnp.zeros((c, N_HEADS * c), F32)
        for lev in range(n_lev + 1):
            if lev == 0:
                ql, kl = q, k
            else:
                e = decays[(lev - 1) * c:lev * c]
                ql, kl = q * e, k * e
            a_t = _dot_nt(kl.astype(BF16), by_head(ql))
            att_t = jnp.where(lvl == lev, a_t, att_t)
        q_decay = decays[n_lev * c:(n_lev + 1) * c]
        k_out = (k * decays[(n_lev + 1) * c:(n_lev + 2) * c]).astype(BF16)
        s = s_scr[...]
        v_all = v_ref[rows, :].astype(BF16)
        intra = _dot_tn(att_t.astype(BF16), v_all)
        inter = _dot(by_head(q * q_decay), s.astype(BF16))
        for h in range(N_HEADS):
            o = intra[h * c:(h + 1) * c, h * DV_B:(h + 1) * DV_B] + inter[h * c:(h + 1) * c]
            y = _rms_rows(o, g_ref[:, h * DV_B:(h + 1) * DV_B])
            o_ref[rows, h * DV_B:(h + 1) * DV_B] = y * _silu(r_ref[rows, h * DV_B:(h + 1) * DV_B])
        kv = _dot_tn(k_out, v_all)
        ds = jnp.concatenate([kv[h * HEAD_DIM:(h + 1) * HEAD_DIM, h * DV_B:(h + 1) * DV_B]
                              for h in range(N_HEADS)], axis=0)
        chunk_decay = jnp.broadcast_to(q_decay[c - 1:c, :], (DV_B, HW)).T
        s_scr[...] = chunk_decay * s + ds

    lax.fori_loop(0, n_chunks, chunk, 0)

    @pl.when(j == n_steps - 1)
    def _():
        s_ref[...] = s_seqs[...]


def _gla_call(q, k, v, r, la, s0, g_gla, c):
    b, t, _ = q.shape
    vw = N_HEADS * DV_B
    tc = _pick_tile(t, 512, c)
    n_steps = t // tc
    nb = max(n for n in (4, 2, 1) if b % n == 0)
    table, lvl, n_lev = _gla_tables(c)
    qk_spec = pl.BlockSpec((nb, tc, HW), lambda bi, j: (bi, j, 0))
    vr_spec = pl.BlockSpec((nb, tc, vw), lambda bi, j: (bi, j, 0))
    st_spec = pl.BlockSpec((nb, HW, DV_B), lambda bi, j: (bi, 0, 0))
    const = lambda shape: pl.BlockSpec(shape, lambda bi, j: (0, 0))
    return pl.pallas_call(
        functools.partial(_gla_kernel, c=c, n_lev=n_lev, n_chunks=tc // c, n_steps=n_steps),
        grid=(b // nb, n_steps),
        in_specs=[qk_spec, qk_spec, vr_spec, vr_spec, qk_spec, st_spec,
                  const(table.shape), const(lvl.shape), const((1, vw))],
        out_specs=[vr_spec, st_spec],
        out_shape=[jax.ShapeDtypeStruct((b, t, vw), F32), jax.ShapeDtypeStruct((b, HW, DV_B), F32)],
        scratch_shapes=[pltpu.VMEM((nb, HW, DV_B), F32)],
        compiler_params=_cparams(2),
        name="gated_linear_attn",
    )(q, k, v, r, la, s0, jnp.asarray(table, BF16), jnp.asarray(lvl), g_gla)


_COUNT_ROWS = 32


def _key_to_f32(key):
    key = jnp.maximum(key, _KEY_NEG_INF)
    bits = jnp.where(key < 0, key ^ 0x7FFFFFFF, key)
    return lax.bitcast_convert_type(bits, F32)


def _dsa_kernel(q_ref, qi_ref, wi_ref, kc_ref, vc_ref, ki_ref, eye_ref, tri_ref, o_ref,
                sc_scr, hi_scr, lo_scr, lg_scr, vct_scr, acc_scr, *, tq, tk, past, l_real, n_sel):
    i = pl.program_id(1)
    scale = HEAD_DIM ** -0.5
    eye = eye_ref[...]
    l_pad = sc_scr.shape[0]

    @pl.when(i == 0)
    def _():
        vct_scr[...] = vc_ref[...].astype(BF16)

    q_t = _dot_nt(eye, (q_ref[...] * (scale * _LOG2_E)).astype(BF16)).astype(BF16)
    qi_t = _dot_nt(eye, (qi_ref[...] * scale).astype(BF16)).astype(BF16)
    w_t = wi_ref[...].T * (N_HEADS ** -0.5)
    qpos = past + i * tq + lax.broadcasted_iota(jnp.int32, (1, tq), 1)
    chunk_shift = int(math.log2(CHUNK))
    qchunk = lax.shift_right_logical(qpos, chunk_shift)
    k_end = ((past + (i + 1) * tq - 1) // CHUNK + 1) * CHUNK
    n_blk = jnp.minimum((k_end + tk - 1) // tk, l_pad // tk)

    def rows_of(j):
        return pl.ds(pl.multiple_of(j * tk, tk), tk)

    def admissible(j):
        kpos = j * tk + lax.broadcasted_iota(jnp.int32, (tk, 1), 0)
        return (lax.shift_right_logical(kpos, chunk_shift) <= qchunk) & (kpos < l_real)

    def score_blk(j, carry):
        kib = ki_ref[rows_of(j), :].astype(BF16)
        s = jnp.zeros((tk, tq), F32)
        for h in range(N_HEADS):
            sh = _dot(kib, qi_t[h * HEAD_DIM:(h + 1) * HEAD_DIM])
            s = s + w_t[_WI_LANE + h:_WI_LANE + h + 1] * jnp.maximum(sh, 0.0)
        s = jnp.where(admissible(j), jnp.where(s == 0.0, 0.0, s), _NEG_INF)
        sc_scr[rows_of(j), :] = s
        bits = lax.bitcast_convert_type(s, jnp.int32)
        key = jnp.where(bits < 0, bits ^ 0x7FFFFFFF, bits)
        hi_scr[rows_of(j), :] = lax.shift_right_arithmetic(key, 16).astype(jnp.int16)
        lo_scr[rows_of(j), :] = ((key & 0xFFFF) - 32768).astype(jnp.int16)
        return carry

    _fori_by_two(0, n_blk, score_blk, 0)

    def counts(ref, preds, one, zero):
        def blk(j, accs):
            a = ref[rows_of(j), :]
            out = []
            for pred, acc in zip(preds, accs):
                m = jnp.where(pred(a), one, zero).reshape(tk // _COUNT_ROWS, _COUNT_ROWS, tq)
                for r in range(tk // _COUNT_ROWS):
                    acc = acc + m[r]
                out.append(acc)
            return tuple(out)
        accs = _fori_by_two(0, n_blk, blk, tuple(jnp.zeros((_COUNT_ROWS, tq), one.dtype) for _ in preds))
        return [jnp.sum(acc.astype(jnp.int32), axis=0, keepdims=True) for acc in accs]

    def count(ref, pred, one, zero):
        return counts(ref, [pred], one, zero)[0]

    one16, zero16 = jnp.int16(1), jnp.int16(0)

    def bisect16(ref, want):
        def bit(it, tau):
            cand = tau + lax.shift_left(jnp.int32(1), 15 - it)
            c16 = cand.astype(jnp.int16)
            return jnp.where(count(ref, lambda a: a >= c16, one16, zero16) >= want, cand, tau)
        return lax.fori_loop(0, 16, bit, jnp.full((1, tq), -32768, jnp.int32))

    hi_star = bisect16(hi_scr, n_sel)
    h16 = hi_star.astype(jnp.int16)
    want_lo = n_sel - count(hi_scr, lambda a: a > h16, one16, zero16)

    def mask_lo(j, carry):
        lo_scr[rows_of(j), :] = jnp.where(hi_scr[rows_of(j), :] == h16, lo_scr[rows_of(j), :],
                                          jnp.int16(-32768))
        return carry

    lax.fori_loop(0, n_blk, mask_lo, 0)
    lo_star = bisect16(lo_scr, want_lo)
    thr_fast = _key_to_f32(lax.shift_left(hi_star, 16) + (lo_star + 32768))

    f_one, f_zero = jnp.float32(1.0), jnp.float32(0.0)
    count_ge = lambda t: count(sc_scr, lambda a: a >= t, f_one, f_zero)
    count_gt = lambda t: count(sc_scr, lambda a: a > t, f_one, f_zero)
    n_gt, n_ge = counts(sc_scr, [lambda a: a > thr_fast, lambda a: a >= thr_fast], f_one, f_zero)
    settled = jnp.min(jnp.where((n_gt < n_sel) & (n_ge >= n_sel), 1, 0)) == 1

    def float_bisection():
        def value_bit(it, tau):
            cand = tau + lax.shift_left(jnp.int32(1), 31 - it)
            return jnp.where(count_ge(_key_to_f32(cand)) >= n_sel, cand, tau)
        t = _key_to_f32(lax.fori_loop(0, 32, value_bit, jnp.full((1, tq), _INT_MIN, jnp.int32)))
        return t, count_gt(t)

    thr, n_gt = lax.cond(settled, lambda: (thr_fast, n_gt), float_bisection)
    need = (n_sel - n_gt).astype(F32)

    tri = tri_ref[...]
    fold = lambda a, op: op(a.reshape(tk // SUBLANE, SUBLANE, tq), axis=0)
    parts = lambda v: tuple(jnp.full((SUBLANE, tq), v, F32) for _ in range(N_HEADS))

    def logit_blk(j, carry):
        m_parts, ties_before = carry
        s = sc_scr[rows_of(j), :]
        tie = s == thr
        rank = _dot(tri, jnp.where(tie, 1.0, 0.0).astype(BF16)) + ties_before
        sel = admissible(j) & ((s > thr) | (tie & (rank <= need)))
        bias = jnp.where(sel, 0.0, _NEG_INF)
        kcb = kc_ref[rows_of(j), :].astype(BF16)
        new_parts = []
        for h in range(N_HEADS):
            lg = _dot(kcb, q_t[h * HEAD_DIM:(h + 1) * HEAD_DIM]) + bias
            lg_scr[h, rows_of(j), :] = lg
            new_parts.append(jnp.maximum(m_parts[h], fold(lg, jnp.max)))
        return tuple(new_parts), rank[tk - 1:tk, :]

    m_parts, _ = _fori_by_two(0, n_blk, logit_blk, (parts(_NEG_INF), jnp.zeros((1, tq), F32)))
    m_use = []
    for h in range(N_HEADS):
        m = jnp.max(m_parts[h], axis=0, keepdims=True)
        m_use.append(jnp.where(m == _NEG_INF, 0.0, m))
    acc_scr[...] = jnp.zeros_like(acc_scr)

    def pv_blk(j, l_parts):
        vct = vct_scr[:, rows_of(j)]
        new_parts = []
        for h in range(N_HEADS):
            hd = slice(h * HEAD_DIM, (h + 1) * HEAD_DIM)
            p = jnp.exp2(lg_scr[h, rows_of(j), :] - m_use[h])
            new_parts.append(l_parts[h] + fold(p, jnp.sum))
            acc_scr[hd, :] += _dot(vct, p.astype(BF16))
        return tuple(new_parts)

    l_parts = _fori_by_two(0, n_blk, pv_blk, parts(0.0))
    out_t = jnp.concatenate(
        [acc_scr[h * HEAD_DIM:(h + 1) * HEAD_DIM, :] / jnp.sum(l_parts[h], axis=0, keepdims=True)
         for h in range(N_HEADS)], axis=0)
    o_ref[...] = out_t.T


def _dsa_call(q, qi, wi, kc, vc_t, ki, layer, past, l_real, tq, tk):
    b, t, _ = q.shape
    l_pad = kc.shape[1]
    n_sel = min(TOPK_MAX, l_real // 4)
    assert tk >= n_sel and l_pad % tk == 0 and tk % _COUNT_ROWS == 0
    row = lambda w: pl.BlockSpec((None, tq, w), lambda bi, i: (bi, i, 0))
    full = pl.BlockSpec((None, l_pad, HEAD_DIM), lambda bi, i: (bi, 0, 0))
    full_t = pl.BlockSpec((None, None, HEAD_DIM, l_pad), lambda bi, i: (layer, bi, 0, 0))
    const = lambda n: pl.BlockSpec((n, n), lambda bi, i: (0, 0))
    eye = np.eye(HW, dtype=np.float32)
    tri = (np.arange(tk)[:, None] >= np.arange(tk)[None, :]).astype(np.float32)
    return pl.pallas_call(
        functools.partial(_dsa_kernel, tq=tq, tk=tk, past=past, l_real=l_real, n_sel=n_sel),
        grid=(b, t // tq),
        in_specs=[row(HW), row(HW), row(LANE), full, full_t, full, const(HW), const(tk)],
        out_specs=row(HW),
        out_shape=jax.ShapeDtypeStruct((b, t, HW), F32),
        scratch_shapes=[pltpu.VMEM((l_pad, tq), F32),
                        pltpu.VMEM((l_pad, tq), jnp.int16), pltpu.VMEM((l_pad, tq), jnp.int16),
                        pltpu.VMEM((N_HEADS, l_pad, tq), F32),
                        pltpu.VMEM((HEAD_DIM, l_pad), BF16), pltpu.VMEM((HW, tq), F32)],
        compiler_params=_cparams(2),
        name="indexer_sparse_attn",
    )(q, qi, wi, kc, vc_t, ki, jnp.asarray(eye, BF16), jnp.asarray(tri, BF16))


def _pad_rows(a, n):
    return a if n == a.shape[1] else jnp.pad(a, ((0, 0), (0, n - a.shape[1]), (0, 0)))


def _pad_last(a, n):
    return a if n == a.shape[-1] else jnp.pad(a, [(0, 0)] * (a.ndim - 1) + [(0, n - a.shape[-1])])


def _encoder_layer(x, n_seq, t, mod, past, w, stacked=None):
    rows, d = x.shape
    tm = _pick_tile(t, 512, SUBLANE) if not mod.per_token else rows
    tm_ffn = _pick_tile(t, 1024, SUBLANE) if not mod.per_token else rows
    x = _ffn_call(x, mod, 0, w["gains"], 0, w["w_f1_in"], w["w_f1_out"], w["layer"], tm_ffn)
    seq = lambda a: a.reshape(n_seq, t, a.shape[-1])
    p_len = 0 if past is None else past["k_sb_t"].shape[3]
    l_real = p_len + t
    tk = 256 if l_real >= 256 else LANE
    l_pad = -(-l_real // tk) * tk
    tq = _pick_tile(t, 256, SUBLANE)
    proj = functools.partial(_proj_call, x, mod, w["gains"], w["w_in"], w["w_gate"], w["b_gate"], tm)
    if stacked is None:
        (sbq, sbk, sbv, gq, gk, gv, gr, la, dq, dqi, dkc, dvc, dki, dwi), _ = proj()
        lyr = past["layer"]
        keys_last = lambda new: jnp.swapaxes(seq(new), 1, 2)
        join_t = lambda old_t, new: _pad_last(jnp.concatenate([old_t, keys_last(new)], axis=2), l_pad)[None]
        join_r = lambda old, new: _pad_rows(jnp.concatenate([old, seq(new)], axis=1), l_pad)
        if p_len % tk == 0 and t <= tk:
            new_kv = tuple(_pad_last(keys_last(a), tk) for a in (sbk, sbv))
            o_a = _sb_call(seq(sbq), past["k_sb_t"], past["v_sb_t"], lyr, p_len, tq, tk, new_kv=new_kv)
        else:
            o_a = _sb_call(seq(sbq), join_t(past["k_sb_t"][lyr], sbk), join_t(past["v_sb_t"][lyr], sbv),
                           0, p_len, tq, tk)
        vc_t = join_t(past["v_dsa_t"], dvc)
        kc_rows, ki_rows = join_r(past["k_dsa"], dkc), join_r(past["k_idx"], dki)
        layer = 0
        new_rows = (seq(sbk).reshape(n_seq, t, N_HEADS, HEAD_DIM), seq(sbv).reshape(n_seq, t, N_HEADS, HEAD_DIM),
                    seq(dkc), seq(dvc), seq(dki))
    else:
        assert past is None and l_pad == t and t % tm == 0 and tm % LANE == 0
        layer, prev = stacked
        (sbq, gq, gk, gv, gr, la, dq, dqi, dkc, dki, dwi), new_rows = proj((layer, n_seq, prev))
        k_t, v_t, _, vc_t, _ = new_rows
        kc_rows, ki_rows = seq(dkc), seq(dki)
        o_a = _sb_call(seq(sbq), k_t, v_t, layer, p_len, tq, tk)

    c = CHUNK if t % CHUNK == 0 else t
    s0 = jnp.zeros((n_seq, HW, DV_B), F32) if past is None else past["gla"]
    o_b, s_new = _gla_call(seq(gq), seq(gk), seq(gv), seq(gr), seq(la), s0, w["g_gla"], c)

    t_c = t if t % LANE == 0 else -(-t // LANE) * LANE
    tq_c = _pick_tile(t_c, 256, LANE)
    qpad = lambda a: _pad_rows(seq(a), t_c)
    o_c = _dsa_call(qpad(dq), qpad(dqi), qpad(dwi), kc_rows, vc_t, ki_rows, layer, p_len, l_real, tq_c, tk)
    o_c = o_c if t_c == t else o_c[:, :t]

    flat = lambda a: a.reshape(rows, a.shape[-1])
    x = _ffn_call(x, mod, 6, w["gains"], 4, w["w_f2_in"], w["w_f2_out"], w["layer"], tm_ffn,
                  mix=(flat(o_a), flat(o_b), flat(o_c), w["w_out"]))
    return x, new_rows, s_new.reshape(n_seq, N_HEADS, HEAD_DIM, DV_B)


def kernel(x_prompt, x_sample, cache_k_sb, cache_v_sb, cache_k_dsa, cache_v_dsa, cache_k_idx, state_gla,
           c_prompt, c_sample, w_ada, b_ada, norm_gains, w_ffn1_in, w_ffn1_out, w_ffn2_in, w_ffn2_out,
           w_in, w_gla_gate, b_gla_gate, gla_norm, w_out):
    bp, tp, d = x_prompt.shape
    bs, ts, _ = x_sample.shape
    depth = w_ada.shape[0]
    p_len = cache_k_sb.shape[2]

    mods = _mod_call(jnp.concatenate([c_prompt, c_sample], axis=0), w_ada, b_ada)
    xp = x_prompt.reshape(bp * tp, d)
    xs = x_sample.reshape(bs * ts, d)
    ffn_w = [a.astype(BF16) for a in (w_ffn1_in, w_ffn1_out, w_ffn2_in, w_ffn2_out)]
    sb_cache_t = [jnp.transpose(c, (0, 1, 3, 4, 2)).reshape(depth, bs, HW, p_len) for c in (cache_k_sb, cache_v_sb)]
    shared_p, gla_p, acc_s = None, [], []
    for l in range(depth):
        w_gate = jnp.pad(w_gla_gate[l], ((_GB_LANE, LANE - GATE_RANK - _GB_LANE), (0, 0))).astype(BF16)
        w = dict(gains=norm_gains[l].reshape(-1, 1, d),
                 layer=l, w_f1_in=ffn_w[0], w_f1_out=ffn_w[1], w_f2_in=ffn_w[2], w_f2_out=ffn_w[3],
                 w_in=_relayout_w_in(w_in[l]).astype(BF16), w_gate=w_gate,
                 b_gate=b_gla_gate[l].reshape(1, HW), g_gla=gla_norm[l].reshape(1, N_HEADS * DV_B),
                 w_out=w_out[l].astype(BF16))
        mod_p = _Mod(mods[l, :bp].reshape(bp * N_MOD, 1, d), False, tp)
        mod_s_rows = jnp.repeat(mods[l, bp:].reshape(bs, N_MOD, d), ts, axis=0)
        mod_s = _Mod(jnp.transpose(mod_s_rows, (1, 0, 2)), True)
        past = dict(layer=l, k_sb_t=sb_cache_t[0], v_sb_t=sb_cache_t[1],
                    k_dsa=cache_k_dsa[l], v_dsa_t=jnp.swapaxes(cache_v_dsa[l], 1, 2), k_idx=cache_k_idx[l],
                    gla=state_gla[l].reshape(bs, HW, DV_B))
        xp, shared_p, s_p = _encoder_layer(xp, bp, tp, mod_p, None, w, stacked=(l, shared_p))
        xs, rows_s, s_s = _encoder_layer(xs, bs, ts, mod_s, past, w)
        gla_p.append(s_p)
        acc_s.append(rows_s + (s_s,))
    k_t, v_t, kc_t, vc_t, ki_t = shared_p
    heads_out = lambda a: jnp.transpose(a.reshape(depth, bp, N_HEADS, HEAD_DIM, tp), (0, 1, 4, 2, 3))
    tokens_out = lambda a: jnp.swapaxes(a, 2, 3)
    field = lambda acc, i: jnp.stack([r[i] for r in acc], axis=0)
    return (xp.reshape(bp, tp, d), xs.reshape(bs, ts, d),
            heads_out(k_t), heads_out(v_t), tokens_out(kc_t), tokens_out(vc_t), tokens_out(ki_t),
            jnp.stack(gla_p, axis=0), *(field(acc_s, i) for i in range(6)))
```
